```python
import math
import jax, jax.numpy as jnp
from jax import lax
import numpy as np

D_MODEL = 1024
BATCH = 2
SEQ = 16384
DEPTH = 2

CTX_LEN = 256
GRID_W = 64
HEAD_DIM = 64
ROPE_PAIRS = HEAD_DIM // 4
ROPE_BASE = 10000.0
EPS = 1e-6
N_MOD = 6

NA_HEADS = 4
NA_WIN_ROWS = 8
NA_WIN_COLS = 16
NA_QBLK_W = 16
NA_KBAND_W = 32

DIFF_HEADS = 4
DIFF_QK_DIM = HEAD_DIM
DIFF_V_DIM = 2 * HEAD_DIM
DENSE_QBLOCK = 128

SWA_HEADS = 4
SWA_KV_HEADS = 2
SWA_GROUP = SWA_HEADS // SWA_KV_HEADS
SWA_WINDOW = 128
SWA_BLOCK = 128

N_BRANCH = 3
Q_NA = NA_HEADS * HEAD_DIM
Q_DIFF = DIFF_HEADS * 2 * DIFF_QK_DIM
Q_SWA = SWA_HEADS * HEAD_DIM
K_NA = NA_HEADS * HEAD_DIM
K_DIFF = DIFF_HEADS * 2 * DIFF_QK_DIM
K_SWA = SWA_KV_HEADS * HEAD_DIM
V_NA = NA_HEADS * HEAD_DIM
V_DIFF = DIFF_HEADS * DIFF_V_DIM
V_SWA = SWA_KV_HEADS * HEAD_DIM
O_NA = V_NA
O_DIFF = V_DIFF
O_SWA = SWA_HEADS * HEAD_DIM
Q_W = Q_NA + Q_DIFF + Q_SWA
K_W = K_NA + K_DIFF + K_SWA
KV_W = K_W + V_NA + V_DIFF + V_SWA
GATE_W = N_BRANCH * D_MODEL
IN_W = Q_W + KV_W + GATE_W
Q_SPLITS = (Q_NA, Q_NA + Q_DIFF)
KV_SPLITS = (K_NA, K_NA + K_DIFF, K_W, K_W + V_NA, K_W + V_NA + V_DIFF)

N_EXPERTS = 16
EXPERT_FF = D_MODEL
EC_CAPACITY = 2

kernel_name = "hybrid_na_diff_swa_ec_dit"


def rmsnorm(x, g):
    xf = x.astype(jnp.float32)
    y = xf * lax.rsqrt(jnp.mean(xf * xf, axis=-1, keepdims=True) + EPS)
    return (y * g.astype(jnp.float32)).astype(x.dtype)


def modulate(h, shift, scale):
    return h * (1.0 + scale) + shift


def rope_tables(n):
    t = jnp.arange(n)
    row = (t // GRID_W).astype(jnp.float32)[:, None]
    col = (t % GRID_W).astype(jnp.float32)[:, None]
    inv = ROPE_BASE ** (-jnp.arange(ROPE_PAIRS, dtype=jnp.float32) / ROPE_PAIRS)
    ang = jnp.concatenate([row * inv, row * inv, col * inv, col * inv], axis=-1)
    return jnp.cos(ang), jnp.sin(ang)


def rope2d(x, cos, sin):
    shp = (cos.shape[0],) + (1,) * (x.ndim - 3) + (HEAD_DIM,)
    cs, sn = cos.reshape(shp), sin.reshape(shp)
    xr = x.reshape(*x.shape[:-1], 2, 2, ROPE_PAIRS)
    rot = jnp.stack([-xr[..., 1, :], xr[..., 0, :]], axis=-2).reshape(x.shape)
    return (x * cs + rot * sn).astype(x.dtype)


def split_q(qblk):
    lead = qblk.shape[:-1]
    q_na, q_diff, q_swa = jnp.split(qblk, Q_SPLITS, axis=-1)
    return (q_na.reshape(*lead, NA_HEADS, HEAD_DIM),
            q_diff.reshape(*lead, DIFF_HEADS, 2, DIFF_QK_DIM),
            q_swa.reshape(*lead, SWA_HEADS, HEAD_DIM))


def split_kv(kvblk):
    lead = kvblk.shape[:-1]
    k_na, k_diff, k_swa, v_na, v_diff, v_swa = jnp.split(kvblk, KV_SPLITS, axis=-1)
    return (k_na.reshape(*lead, NA_HEADS, HEAD_DIM),
            k_diff.reshape(*lead, DIFF_HEADS, 2, DIFF_QK_DIM),
            k_swa.reshape(*lead, SWA_KV_HEADS, HEAD_DIM),
            v_na.reshape(*lead, NA_HEADS, HEAD_DIM),
            v_diff.reshape(*lead, DIFF_HEADS, DIFF_V_DIM),
            v_swa.reshape(*lead, SWA_KV_HEADS, HEAD_DIM))


def ctx_attend(q, k, v):
    B, L = q.shape[:2]
    s = jnp.einsum('bqhd,bkhd->bhqk', q, k).astype(jnp.float32) * (q.shape[-1] ** -0.5)
    p = jax.nn.softmax(s, axis=-1).astype(v.dtype)
    return jnp.einsum('bhqk,bkhd->bqhd', p, v).reshape(B, L, -1)


def na_latent(q, k, v, kc, vc, rpb):
    B, N, H, Dh = q.shape
    R = N // GRID_W
    kh = min(NA_WIN_ROWS, R)
    nj = GRID_W // NA_QBLK_W
    r = jnp.arange(R)
    row_idx = jnp.clip(r - kh // 2, 0, R - kh)[:, None] + jnp.arange(kh)[None, :]
    j = jnp.arange(nj)
    band0 = jnp.clip(j * NA_QBLK_W - NA_WIN_COLS // 2, 0, GRID_W - NA_KBAND_W)
    col_idx = band0[:, None] + jnp.arange(NA_KBAND_W)[None, :]
    qcol = j[:, None] * NA_QBLK_W + jnp.arange(NA_QBLK_W)[None, :]
    cstart = jnp.clip(qcol - NA_WIN_COLS // 2, 0, GRID_W - NA_WIN_COLS)[..., None]
    kcol = col_idx[:, None, :]
    col_ok = (kcol >= cstart) & (kcol < cstart + NA_WIN_COLS)
    dr = row_idx - r[:, None]
    dc = jnp.clip(kcol - qcol[..., None], -(NA_WIN_COLS - 1), NA_WIN_COLS - 1)
    bias = rpb[:, (dr + NA_WIN_ROWS - 1)[:, None, None, :, None],
               (dc + NA_WIN_COLS - 1)[None, :, :, None, :]]
    qg = q.reshape(B, R, nj, NA_QBLK_W, H, Dh)
    ri, ci = row_idx[:, None, :, None], col_idx[None, :, None, :]
    kw = k.reshape(B, R, GRID_W, H, Dh)[:, ri, ci]
    vw = v.reshape(B, R, GRID_W, H, Dh)[:, ri, ci]
    scale = Dh ** -0.5
    s_loc = jnp.einsum('brjqhd,brjkwhd->bhrjqkw', qg, kw).astype(jnp.float32) * scale + bias[None].astype(jnp.float32)
    s_loc = jnp.where(col_ok[:, :, None, :], s_loc, -jnp.inf)
    nloc = kh * NA_KBAND_W
    s_loc = s_loc.reshape(B, H, R, nj, NA_QBLK_W, nloc)
    s_ctx = jnp.einsum('brjqhd,bchd->bhrjqc', qg, kc).astype(jnp.float32) * scale
    p = jax.nn.softmax(jnp.concatenate([s_loc, s_ctx], axis=-1), axis=-1).astype(v.dtype)
    p_loc = p[..., :nloc].reshape(B, H, R, nj, NA_QBLK_W, kh, NA_KBAND_W)
    o = (jnp.einsum('bhrjqkw,brjkwhd->brjqhd', p_loc, vw)
         + jnp.einsum('bhrjqc,bchd->brjqhd', p[..., nloc:], vc))
    return o.reshape(B, N, H * Dh)


def diff_core(q, k, v, lam):
    s = jnp.einsum('bqhmd,bkhmd->bhmqk', q, k).astype(jnp.float32) * (q.shape[-1] ** -0.5)
    p = jax.nn.softmax(s, axis=-1)
    a = (p[:, :, 0] - lam * p[:, :, 1]).astype(v.dtype)
    return jnp.einsum('bhqk,bkhd->bqhd', a, v)


def diff_latent(q, k, v, kc, vc, lam):
    B, N = q.shape[:2]
    k_all = jnp.concatenate([kc, k], axis=1)
    v_all = jnp.concatenate([vc, v], axis=1)
    nb = N // DENSE_QBLOCK
    qb = jnp.moveaxis(q.reshape(B, nb, DENSE_QBLOCK, DIFF_HEADS, 2, DIFF_QK_DIM), 1, 0)
    o = lax.map(lambda qi: diff_core(qi, k_all, v_all, lam), qb)
    return jnp.moveaxis(o, 0, 1).reshape(B, N, DIFF_HEADS, DIFF_V_DIM)


def diff_finish(o, g, lam_init):
    y = rmsnorm(o, g) * (1.0 - lam_init)
    return y.reshape(*o.shape[:-2], DIFF_HEADS * DIFF_V_DIM)


def swa_latent(q, k, v, kc, vc, sink):
    B, N = q.shape[:2]
    nb = N // SWA_BLOCK
    qb = q.reshape(B, nb, SWA_BLOCK, SWA_KV_HEADS, SWA_GROUP, HEAD_DIM)

    def band(t):
        tb = t.reshape(B, nb, SWA_BLOCK, SWA_KV_HEADS, HEAD_DIM)
        tp = jnp.pad(tb, ((0, 0), (1, 1), (0, 0), (0, 0), (0, 0)))
        return jnp.concatenate([tp[:, :-2], tp[:, 1:-1], tp[:, 2:]], axis=2)

    kw, vw = band(k), band(v)
    blk = jnp.arange(nb)[:, None, None]
    qpos = blk * SWA_BLOCK + jnp.arange(SWA_BLOCK)[None, :, None]
    kpos = (blk - 1) * SWA_BLOCK + jnp.arange(3 * SWA_BLOCK)[None, None, :]
    ok = (kpos >= 0) & (kpos < N) & (jnp.abs(qpos - kpos) <= SWA_WINDOW)
    scale = HEAD_DIM ** -0.5
    s_loc = jnp.einsum('bnqgmd,bnkgd->bgmnqk', qb, kw).astype(jnp.float32) * scale
    s_loc = jnp.where(ok, s_loc, -jnp.inf)
    s_ctx = jnp.einsum('bnqgmd,bcgd->bgmnqc', qb, kc).astype(jnp.float32) * scale
    sk = jnp.broadcast_to(sink.astype(jnp.float32).reshape(1, SWA_KV_HEADS, SWA_GROUP, 1, 1, 1),
                          s_loc.shape[:-1] + (1,))
    p = jax.nn.softmax(jnp.concatenate([s_loc, s_ctx, sk], axis=-1), axis=-1).astype(v.dtype)
    nloc = 3 * SWA_BLOCK
    o = (jnp.einsum('bgmnqk,bnkgd->bnqgmd', p[..., :nloc], vw)
         + jnp.einsum('bgmnqc,bcgd->bnqgmd', p[..., nloc:-1], vc))
    return o.reshape(B, N, SWA_HEADS * HEAD_DIM)


def swa_ctx(q, k, v, sink):
    B, L = q.shape[:2]
    qg = q.reshape(B, L, SWA_KV_HEADS, SWA_GROUP, HEAD_DIM)
    s = jnp.einsum('bqgmd,bkgd->bgmqk', qg, k).astype(jnp.float32) * (HEAD_DIM ** -0.5)
    sk = jnp.broadcast_to(sink.astype(jnp.float32).reshape(1, SWA_KV_HEADS, SWA_GROUP, 1, 1),
                          s.shape[:-1] + (1,))
    p = jax.nn.softmax(jnp.concatenate([s, sk], axis=-1), axis=-1)[..., :-1].astype(v.dtype)
    return jnp.einsum('bgmqk,bkgd->bqgmd', p, v).reshape(B, L, SWA_HEADS * HEAD_DIM)


def merge(o_na, o_diff, o_swa, gates, w_na, w_diff, w_swa, w_o):
    g = jax.nn.sigmoid(gates.astype(jnp.float32)).astype(o_na.dtype)
    g = g.reshape(*gates.shape[:-1], N_BRANCH, D_MODEL)
    y = g[..., 0, :] * (o_na @ w_na) + g[..., 1, :] * (o_diff @ w_diff) + g[..., 2, :] * (o_swa @ w_swa)
    return y @ w_o


def ec_ffn(h, w_router, w_gate, w_up, w_down):
    B, n, _ = h.shape
    cap = max(1, EC_CAPACITY * n // N_EXPERTS)
    aff = jax.nn.softmax(jnp.einsum('bnd,de->bne', h, w_router).astype(jnp.float32), axis=-1)
    top_w, top_i = lax.top_k(jnp.swapaxes(aff, 1, 2), cap)
    bidx = jnp.arange(B)[:, None, None]
    xs = h[bidx, top_i]
    hid = jax.nn.silu(jnp.einsum('becd,edf->becf', xs, w_gate)) * jnp.einsum('becd,edf->becf', xs, w_up)
    ye = jnp.einsum('becf,efd->becd', hid, w_down) * top_w[..., None].astype(h.dtype)
    return jnp.zeros_like(h).at[bidx, top_i].add(ye)


def diff_lambda_init(layer):
    return 0.8 - 0.6 * math.exp(-0.3 * layer)


def setup_inputs(seed: int = 0) -> dict:
    key = jax.random.key(seed)
    ks = jax.random.split(key, 22)
    D = D_MODEL

    def nrm(k, shape, s):
        return jax.random.normal(k, shape, jnp.float32) * s

    return {
        "x": nrm(ks[0], (BATCH, SEQ, D), 1.0),
        "c": nrm(ks[1], (BATCH, D), 1.0),
        "ctx": nrm(ks[2], (BATCH, CTX_LEN, D), 1.0),
        "c_ctx": nrm(ks[3], (D,), 1.0),
        "w_mod": nrm(ks[4], (DEPTH, D, N_MOD * D), 0.5 * D ** -0.5),
        "b_mod": nrm(ks[5], (DEPTH, N_MOD * D), 0.01),
        "norm1_g": 1.0 + nrm(ks[6], (DEPTH, D), 0.01),
        "w_in": nrm(ks[7], (DEPTH, D, IN_W), D ** -0.5),
        "na_rpb": nrm(ks[8], (DEPTH, NA_HEADS, 2 * NA_WIN_ROWS - 1, 2 * NA_WIN_COLS - 1), 0.1),
        "diff_lambda": nrm(ks[9], (DEPTH, 4, DIFF_QK_DIM), 0.1),
        "diff_subln_g": 1.0 + nrm(ks[10], (DEPTH, DIFF_V_DIM), 0.01),
        "swa_sink": nrm(ks[11], (DEPTH, SWA_HEADS), 0.5),
        "w_branch_na": nrm(ks[12], (DEPTH, O_NA, D), O_NA ** -0.5),
        "w_branch_diff": nrm(ks[13], (DEPTH, O_DIFF, D), O_DIFF ** -0.5),
        "w_branch_swa": nrm(ks[14], (DEPTH, O_SWA, D), O_SWA ** -0.5),
        "w_out": nrm(ks[15], (DEPTH, D, D), D ** -0.5),
        "norm2_g": 1.0 + nrm(ks[16], (DEPTH, D), 0.01),
        "w_router": nrm(ks[17], (DEPTH, D, N_EXPERTS), D ** -0.5),
        "w_expert_gate": nrm(ks[18], (DEPTH, N_EXPERTS, D, EXPERT_FF), D ** -0.5),
        "w_expert_up": nrm(ks[19], (DEPTH, N_EXPERTS, D, EXPERT_FF), D ** -0.5),
        "w_expert_down": nrm(ks[20], (DEPTH, N_EXPERTS, EXPERT_FF, D), EXPERT_FF ** -0.5),
        "final_g": 1.0 + nrm(ks[21], (D,), 0.01),
    }


def reference(x, c, ctx, c_ctx, w_mod, b_mod, norm1_g, w_in, na_rpb, diff_lambda, diff_subln_g,
              swa_sink, w_branch_na, w_branch_diff, w_branch_swa, w_out, norm2_g, w_router,
              w_expert_gate, w_expert_up, w_expert_down, final_g):
    N = x.shape[1]
    cos, sin = rope_tables(N)
    xc = ctx
    for l in range(DEPTH):
        last = l == DEPTH - 1
        lam_init = diff_lambda_init(l)
        mod = jax.nn.silu(c) @ w_mod[l] + b_mod[l]
        mod_c = jax.nn.silu(c_ctx) @ w_mod[l] + b_mod[l]
        sh1, sc1, gt1, sh2, sc2, gt2 = jnp.split(mod[:, None, :], N_MOD, axis=-1)
        sh1c, sc1c, gt1c, sh2c, sc2c, gt2c = jnp.split(mod_c, N_MOD, axis=-1)

        lam_p = diff_lambda[l].astype(jnp.float32)
        lam = (jnp.exp(jnp.sum(lam_p[0] * lam_p[1])) - jnp.exp(jnp.sum(lam_p[2] * lam_p[3]))
               + lam_init)

        h = modulate(rmsnorm(x, norm1_g[l]), sh1, sc1)
        hc = modulate(rmsnorm(xc, norm1_g[l]), sh1c, sc1c)
        proj = h @ w_in[l]
        q_na, q_diff, q_swa = split_q(proj[..., :Q_W])
        k_na, k_diff, k_swa, v_na, v_diff, v_swa = split_kv(proj[..., Q_W:Q_W + KV_W])
        gates = proj[..., Q_W + KV_W:]
        q_diff, k_diff = rope2d(q_diff, cos, sin), rope2d(k_diff, cos, sin)
        q_swa, k_swa = rope2d(q_swa, cos, sin), rope2d(k_swa, cos, sin)
        if last:
            kv_c = hc @ w_in[l][:, Q_W:Q_W + KV_W]
        else:
            proj_c = hc @ w_in[l]
            kv_c = proj_c[..., Q_W:Q_W + KV_W]
        kc_na, kc_diff, kc_swa, vc_na, vc_diff, vc_swa = split_kv(kv_c)

        o_na = na_latent(q_na, k_na, v_na, kc_na, vc_na, na_rpb[l])
        o_diff = diff_finish(diff_latent(q_diff, k_diff, v_diff, kc_diff, vc_diff, lam),
                             diff_subln_g[l], lam_init)
        o_swa = swa_latent(q_swa, k_swa, v_swa, kc_swa, vc_swa, swa_sink[l])
        y = merge(o_na, o_diff, o_swa, gates, w_branch_na[l], w_branch_diff[l], w_branch_swa[l], w_out[l])
        x = x + gt1 * y

        if not last:
            qc_na, qc_diff, qc_swa = split_q(proj_c[..., :Q_W])
            gates_c = proj_c[..., Q_W + KV_W:]
            oc_na = ctx_attend(qc_na, kc_na, vc_na)
            oc_diff = diff_finish(diff_core(qc_diff, kc_diff, vc_diff, lam), diff_subln_g[l], lam_init)
            oc_swa = swa_ctx(qc_swa, kc_swa, vc_swa, swa_sink[l])
            yc = merge(oc_na, oc_diff, oc_swa, gates_c, w_branch_na[l], w_branch_diff[l],
                       w_branch_swa[l], w_out[l])
            xc = xc + gt1c * yc

        h2 = modulate(rmsnorm(x, norm2_g[l]), sh2, sc2)
        x = x + gt2 * ec_ffn(h2, w_router[l], w_expert_gate[l], w_expert_up[l], w_expert_down[l])
        if not last:
            h2c = modulate(rmsnorm(xc, norm2_g[l]), sh2c, sc2c)
            xc = xc + gt2c * ec_ffn(h2c, w_router[l], w_expert_gate[l], w_expert_up[l], w_expert_down[l])

    return rmsnorm(x, final_g)
```

```python
import functools
import math

import numpy as np
import jax
import jax.numpy as jnp
from jax import lax
from jax.experimental import pallas as pl
from jax.experimental.pallas import tpu as pltpu

F32 = jnp.float32
BF16 = jnp.bfloat16
I32 = jnp.int32
HIGHEST = lax.Precision.HIGHEST

D_MODEL = 1024
CTX = 256
TILE = 256
GRID_W = 64
HEAD_DIM = 64
ROPE_PAIRS = HEAD_DIM // 4
ROPE_BASE = 10000.0
EPS = 1e-6
N_MOD = 6
NA_HEADS = 4
NA_WIN_ROWS = 8
NA_WIN_COLS = 16
NA_TILE_ROWS = TILE // GRID_W
DIFF_HEADS = 4
DIFF_V = 2 * HEAD_DIM
SWA_HEADS = 4
SWA_KV_HEADS = 2
SWA_WINDOW = 128
N_EXPERTS = 16
EC_CAPACITY = 2
NEG = -1e30
LANES = 128
BF16_ROWS = 16

C_QNA, C_QD, C_QS = 0, 256, 768
C_KNA, C_KD, C_KS = 1024, 1280, 1792
C_VNA, C_VD, C_VS = 1920, 2176, 2688
C_GATE, C_END = 2816, 5888

VMEM_LIMIT = 56 * 1024 * 1024


def _params(n_axes, vmem=VMEM_LIMIT):
    return pltpu.CompilerParams(dimension_semantics=("arbitrary",) * n_axes, vmem_limit_bytes=vmem)


def _nt_dot(a, b):
    return lax.dot_general(a, b, (((1,), (1,)), ((), ())), preferred_element_type=F32)


def _dot(a, b):
    return jnp.dot(a, b, preferred_element_type=F32)


def _mod_kernel(c_ref, w_ref, b_ref, o_ref):
    c = c_ref[...]
    s = c / (1.0 + jnp.exp(-c))
    o_ref[0] = jnp.dot(s, w_ref[0], preferred_element_type=F32, precision=HIGHEST) + b_ref[0]


def _modulation(cin, w_mod, b_mod):
    depth, d, w = w_mod.shape
    tn = 1024
    return pl.pallas_call(
        _mod_kernel,
        out_shape=jax.ShapeDtypeStruct((depth, 8, w), F32),
        grid=(depth, w // tn),
        in_specs=[pl.BlockSpec((8, d), lambda l, j: (0, 0)),
                  pl.BlockSpec((1, d, tn), lambda l, j: (l, 0, j)),
                  pl.BlockSpec((1, 1, tn), lambda l, j: (l, 0, j))],
        out_specs=pl.BlockSpec((1, 8, tn), lambda l, j: (l, 0, j)),
        compiler_params=_params(2),
        name="modulation",
    )(cin, w_mod, b_mod.reshape(depth, 1, w))


def _proj_kernel(x_ref, mod_ref, g_ref, w_ref, cos_ref, sin_ref,
                 qna_ref, kna_ref, vna_ref, qd_ref, kd_ref, vd_ref, qs_ref, ks_ref, vs_ref, gs_ref):
    x = x_ref[0]
    mod = mod_ref[0, 0]
    y = x * lax.rsqrt(jnp.mean(x * x, axis=-1, keepdims=True) + EPS) * g_ref[...]
    h = (y * (1.0 + mod[1:2]) + mod[0:1]).astype(BF16)

    def proj(a, b):
        return _dot(h, w_ref[:, a:b])

    cos = cos_ref[...]
    sin = sin_ref[...]
    lane = lax.broadcasted_iota(I32, cos.shape, 1)
    first_half = (lane % (2 * ROPE_PAIRS)) < ROPE_PAIRS

    def rope(t):
        outs = []
        for j in range(t.shape[1] // LANES):
            c = t[:, j * LANES:(j + 1) * LANES]
            r = jnp.where(first_half, pltpu.roll(c, LANES - ROPE_PAIRS, 1), pltpu.roll(c, ROPE_PAIRS, 1))
            outs.append(c * cos + r * sin)
        return outs[0] if len(outs) == 1 else jnp.concatenate(outs, axis=1)

    scale = HEAD_DIM ** -0.5
    qna_ref[0] = (proj(C_QNA, C_QD) * scale).astype(BF16)
    qd_ref[0] = (rope(proj(C_QD, C_QS)) * scale).astype(BF16)
    qs_ref[0] = (rope(proj(C_QS, C_KNA)) * scale).astype(BF16)
    kna_ref[0] = proj(C_KNA, C_KD).astype(BF16)
    kd_ref[0] = rope(proj(C_KD, C_KS)).astype(BF16)
    ks_ref[0] = rope(proj(C_KS, C_VNA)).astype(BF16)
    vna_ref[0] = proj(C_VNA, C_VD).astype(BF16)
    vd = proj(C_VD, C_VS).astype(BF16)
    ones = jnp.ones((vd.shape[0], DIFF_V), BF16)
    vd_ref[0] = jnp.concatenate(
        [blk for hd in range(DIFF_HEADS) for blk in (vd[:, hd * DIFF_V:(hd + 1) * DIFF_V], ones)], axis=1)
    vs_ref[0] = proj(C_VS, C_GATE).astype(BF16)
    gates = proj(C_GATE, C_END)
    gs_ref[0] = (1.0 / (1.0 + jnp.exp(-gates))).astype(BF16)


def _project(xu, modtab, g, w_in, cos_u, sin_u):
    b, m, d = xu.shape
    nt = m // TILE
    widths = (256, 256, 256, 512, 512, 2 * DIFF_HEADS * DIFF_V, 256, 128, 128, C_END - C_GATE)
    row = lambda w: pl.BlockSpec((1, TILE, w), lambda bb, i: (bb, i, 0))
    return pl.pallas_call(
        _proj_kernel,
        out_shape=[jax.ShapeDtypeStruct((b, m, w), BF16) for w in widths],
        grid=(b, nt),
        in_specs=[row(d),
                  pl.BlockSpec((1, 1, N_MOD, d), lambda bb, i: (bb, jnp.minimum(i, 1), 0, 0)),
                  pl.BlockSpec((1, d), lambda bb, i: (0, 0)),
                  pl.BlockSpec((d, C_END), lambda bb, i: (0, 0)),
                  pl.BlockSpec((TILE, LANES), lambda bb, i: (i, 0)),
                  pl.BlockSpec((TILE, LANES), lambda bb, i: (i, 0))],
        out_specs=[row(w) for w in widths],
        compiler_params=_params(2),
        name="norm_project_rope",
    )(xu, modtab, g, w_in, cos_u, sin_u)


def _na_kernel(q_ref, kc_ref, kp_ref, kcur_ref, kn_ref, vc_ref, vp_ref, vcur_ref, vn_ref, bias_ref, o_ref):
    q = q_ref[0]
    kc, vc = kc_ref[0], vc_ref[0]
    kl = jnp.concatenate([kp_ref[0], kcur_ref[0], kn_ref[0]], axis=0)
    vl = jnp.concatenate([vp_ref[0], vcur_ref[0], vn_ref[0]], axis=0)
    outs = []
    for hd in range(NA_HEADS):
        sl = slice(hd * HEAD_DIM, (hd + 1) * HEAD_DIM)
        qh = q[:, sl]
        s_c = _nt_dot(qh, kc[:, sl])
        s_l = _nt_dot(qh, kl[:, sl]) + bias_ref[0, hd]
        mx = jnp.maximum(jnp.max(s_c, axis=1, keepdims=True), jnp.max(s_l, axis=1, keepdims=True))
        p_c = jnp.exp(s_c - mx)
        p_l = jnp.exp(s_l - mx)
        den = jnp.sum(p_c, axis=1, keepdims=True) + jnp.sum(p_l, axis=1, keepdims=True)
        o = _dot(p_c.astype(BF16), vc[:, sl]) + _dot(p_l.astype(BF16), vl[:, sl])
        outs.append(o / den)
    o_ref[0] = jnp.concatenate(outs, axis=1).astype(BF16)


def _na_bias_index():
    tr = NA_TILE_ROWS
    qr = np.arange(tr)[:, None, None, None, None]
    qc = np.arange(GRID_W)[None, :, None, None, None]
    seg = np.arange(3)[None, None, :, None, None]
    kr = np.arange(tr)[None, None, None, :, None]
    kc = np.arange(GRID_W)[None, None, None, None, :]
    krow = (seg - 1) * tr + kr
    cstart = np.clip(qc - NA_WIN_COLS // 2, 0, GRID_W - NA_WIN_COLS)
    col_ok = (kc >= cstart) & (kc < cstart + NA_WIN_COLS)
    dc = np.clip(kc - qc, -(NA_WIN_COLS - 1), NA_WIN_COLS - 1) + NA_WIN_COLS - 1
    drs, dcs, oks = [], [], []
    shape = (tr, GRID_W, 3, tr, GRID_W)
    for start in (0 * qr, qr - NA_WIN_ROWS // 2, 0 * qr + tr - NA_WIN_ROWS):
        row_ok = (krow >= start) & (krow < start + NA_WIN_ROWS)
        dr = np.clip(krow - qr, -(NA_WIN_ROWS - 1), NA_WIN_ROWS - 1) + NA_WIN_ROWS - 1
        n = tr * GRID_W
        drs.append(np.broadcast_to(dr, shape).reshape(n, 3 * n))
        dcs.append(np.broadcast_to(dc, shape).reshape(n, 3 * n))
        oks.append(np.broadcast_to(row_ok & col_ok, shape).reshape(n, 3 * n))
    return np.stack(drs), np.stack(dcs), np.stack(oks)


def _na_bias_table(rpb):
    dr, dc, ok = _na_bias_index()
    t = jnp.where(ok[None], rpb.astype(F32)[:, dr, dc], NEG)
    t = jnp.swapaxes(t, 0, 1)
    return jnp.concatenate([jnp.full_like(t[:1], NEG), t], axis=0)


def _na_attention(q, k, v, bias):
    b, m, w = q.shape
    nt = m // TILE
    nb = nt - 1

    def lat(off):
        return lambda bb, i: (bb, jnp.clip(i - 1 + off, 0, nb - 1) + 1, 0)

    def kind(bb, i):
        return (jnp.where(i == 0, 0, jnp.where(i == 1, 1, jnp.where(i == nb, 3, 2))), 0, 0, 0)

    blk = lambda f: pl.BlockSpec((1, TILE, w), f)
    kv_specs = [blk(lambda bb, i: (bb, 0, 0)), blk(lat(-1)), blk(lat(0)), blk(lat(1))]
    return pl.pallas_call(
        _na_kernel,
        out_shape=jax.ShapeDtypeStruct((b, m, w), BF16),
        grid=(b, nt),
        in_specs=[blk(lambda bb, i: (bb, i, 0))] + kv_specs + kv_specs
                 + [pl.BlockSpec((1, NA_HEADS, TILE, 3 * TILE), kind)],
        out_specs=blk(lambda bb, i: (bb, i, 0)),
        compiler_params=_params(2),
        name="neighbourhood_attention",
    )(q, k, k, k, k, v, v, v, v, bias)


def _swa_kernel(sink_ref, q_ref, kc_ref, kp_ref, kcur_ref, kn_ref, vc_ref, vp_ref, vcur_ref, vn_ref, o_ref,
                *, n_lat):
    i = pl.program_id(1)
    q = q_ref[0]
    kc, vc = kc_ref[0], vc_ref[0]
    kl = jnp.concatenate([kp_ref[0], kcur_ref[0], kn_ref[0]], axis=0)
    vl = jnp.concatenate([vp_ref[0], vcur_ref[0], vn_ref[0]], axis=0)
    group = SWA_HEADS // SWA_KV_HEADS
    rows = group * TILE
    qpos = (i - 1) * TILE + lax.broadcasted_iota(I32, (rows, 3 * TILE), 0) % TILE
    kpos = (i - 2) * TILE + lax.broadcasted_iota(I32, (rows, 3 * TILE), 1)
    ok = (kpos >= 0) & (kpos < n_lat) & (jnp.abs(qpos - kpos) <= SWA_WINDOW) & (i > 0)
    rid = lax.broadcasted_iota(I32, (rows, 1), 0)
    outs = []
    for g in range(SWA_KV_HEADS):
        ksl = slice(g * HEAD_DIM, (g + 1) * HEAD_DIM)
        qg = jnp.concatenate([q[:, (g * group + j) * HEAD_DIM:(g * group + j + 1) * HEAD_DIM]
                              for j in range(group)], axis=0)
        sink = jnp.zeros((rows, 1), F32)
        for j in range(group):
            sink = jnp.where(rid // TILE == j, sink_ref[g * group + j], sink)
        s_c = _nt_dot(qg, kc[:, ksl])
        s_l = jnp.where(ok, _nt_dot(qg, kl[:, ksl]), NEG)
        mx = jnp.maximum(jnp.maximum(jnp.max(s_c, axis=1, keepdims=True), jnp.max(s_l, axis=1, keepdims=True)), sink)
        p_c = jnp.exp(s_c - mx)
        p_l = jnp.exp(s_l - mx)
        den = jnp.sum(p_c, axis=1, keepdims=True) + jnp.sum(p_l, axis=1, keepdims=True) + jnp.exp(sink - mx)
        o = (_dot(p_c.astype(BF16), vc[:, ksl]) + _dot(p_l.astype(BF16), vl[:, ksl])) / den
        outs.extend(o[j * TILE:(j + 1) * TILE] for j in range(group))
    o_ref[0] = jnp.concatenate(outs, axis=1).astype(BF16)


def _swa_attention(q, k, v, sink):
    b, m, wq = q.shape
    wk = k.shape[2]
    nt = m // TILE
    nb = nt - 1

    def lat(off):
        return lambda bb, i: (bb, jnp.clip(i - 1 + off, 0, nb - 1) + 1, 0)

    blk = lambda w, f: pl.BlockSpec((1, TILE, w), f)
    kv_specs = [blk(wk, lambda bb, i: (bb, 0, 0)), blk(wk, lat(-1)), blk(wk, lat(0)), blk(wk, lat(1))]
    return pl.pallas_call(
        functools.partial(_swa_kernel, n_lat=m - CTX),
        out_shape=jax.ShapeDtypeStruct((b, m, wq), BF16),
        grid=(b, nt),
        in_specs=[pl.BlockSpec(memory_space=pltpu.SMEM), blk(wq, lambda bb, i: (bb, i, 0))] + kv_specs + kv_specs,
        out_specs=blk(wq, lambda bb, i: (bb, i, 0)),
        compiler_params=_params(2),
        name="windowed_attention",
    )(sink, q, k, k, k, k, v, v, v, v)


def _diff_kernel(lam_ref, q_ref, k_ref, v_ref, g_ref, o_ref, acc_ref, m_ref, *, n_chunks, kblk, lam_init):
    i = pl.program_id(2)
    q = q_ref[0]
    lane = lax.broadcasted_iota(I32, q.shape, 1)
    zero = jnp.zeros_like(q)
    qq = jnp.concatenate([jnp.where(lane < HEAD_DIM, q, zero), jnp.where(lane >= HEAD_DIM, q, zero)], axis=0)

    def step(start, size):
        k = k_ref[0, pl.ds(start, size), :]
        v = v_ref[0, pl.ds(start, size), :]
        s = _nt_dot(qq, k)
        m_prev = m_ref[...]
        m_new = jnp.maximum(m_prev, jnp.max(s, axis=1, keepdims=True))
        p = jnp.exp(s - m_new)
        acc_ref[...] = jnp.exp(m_prev - m_new) * acc_ref[...] + _dot(p.astype(BF16), v)
        m_ref[...] = m_new

    m_ref[...] = jnp.full(m_ref.shape, NEG, F32)
    acc_ref[...] = jnp.zeros(acc_ref.shape, F32)
    step(0, CTX)

    def body(c, carry):
        step(pl.multiple_of(CTX + c * kblk, kblk), kblk)
        return carry

    lax.fori_loop(0, jnp.where(i == 0, 0, n_chunks), body, 0)
    acc = acc_ref[...]
    o = acc[:, :DIFF_V] / acc[:, DIFF_V:]
    d = o[:TILE] - lam_ref[0] * o[TILE:]
    y = d * lax.rsqrt(jnp.mean(d * d, axis=-1, keepdims=True) + EPS) * g_ref[...]
    o_ref[0] = (y * (1.0 - lam_init)).astype(BF16)


def _diff_attention(q, k, v, lam, g, lam_init, kblk=256):
    b, m, _ = q.shape
    nt = m // TILE
    n_lat = m - CTX
    return pl.pallas_call(
        functools.partial(_diff_kernel, n_chunks=n_lat // kblk, kblk=kblk, lam_init=lam_init),
        out_shape=jax.ShapeDtypeStruct((b, m, DIFF_HEADS * DIFF_V), BF16),
        grid=(b, DIFF_HEADS, nt),
        in_specs=[pl.BlockSpec(memory_space=pltpu.SMEM),
                  pl.BlockSpec((1, TILE, 2 * HEAD_DIM), lambda bb, hh, i: (bb, i, hh)),
                  pl.BlockSpec((1, m, 2 * HEAD_DIM), lambda bb, hh, i: (bb, 0, hh)),
                  pl.BlockSpec((1, m, 2 * DIFF_V), lambda bb, hh, i: (bb, 0, hh)),
                  pl.BlockSpec((1, DIFF_V), lambda bb, hh, i: (0, 0))],
        out_specs=pl.BlockSpec((1, TILE, DIFF_V), lambda bb, hh, i: (bb, i, hh)),
        scratch_shapes=[pltpu.VMEM((2 * TILE, 2 * DIFF_V), F32), pltpu.VMEM((2 * TILE, 1), F32)],
        compiler_params=_params(3),
        name="differential_attention",
    )(lam, q, k, v, g)


def _merge_kernel(ona_ref, od_ref, osw_ref, gs_ref, x_ref, mod_ref, wna_ref, wd_ref, wsw_ref, wo_ref,
                  g2_ref, wr_ref, wrt_ref, xo_ref, h2_ref, aff_ref, afft_ref):
    d = x_ref.shape[2]
    gs = gs_ref[0]
    y = (gs[:, :d].astype(F32) * _dot(ona_ref[0], wna_ref[...])
         + gs[:, d:2 * d].astype(F32) * _dot(od_ref[0], wd_ref[...])
         + gs[:, 2 * d:].astype(F32) * _dot(osw_ref[0], wsw_ref[...]))
    mod = mod_ref[0, 0]
    xn = x_ref[0] + mod[2:3] * _dot(y.astype(BF16), wo_ref[...])
    xo_ref[0] = xn
    r = xn * lax.rsqrt(jnp.mean(xn * xn, axis=-1, keepdims=True) + EPS) * g2_ref[...]
    h2 = r * (1.0 + mod[4:5]) + mod[3:4]
    h2_ref[0] = h2.astype(BF16)
    logits = jnp.dot(h2, wr_ref[...], preferred_element_type=F32, precision=HIGHEST)
    e = jnp.exp(logits - jnp.max(logits, axis=1, keepdims=True))
    aff_ref[0] = e / jnp.sum(e, axis=1, keepdims=True)
    lt = lax.dot_general(wrt_ref[...], h2, (((1,), (1,)), ((), ())),
                         preferred_element_type=F32, precision=HIGHEST)
    et = jnp.exp(lt - jnp.max(lt, axis=0, keepdims=True))
    afft_ref[0] = et / jnp.sum(et, axis=0, keepdims=True)


def _merge(o_na, o_d, o_sw, gs, xu, modtab, w_na, w_d, w_sw, w_o, g2, w_r):
    b, m, d = xu.shape
    nt = m // TILE
    ne = w_r.shape[1]
    row = lambda w: pl.BlockSpec((1, TILE, w), lambda bb, i: (bb, i, 0))
    full = lambda a: pl.BlockSpec(a.shape, lambda bb, i: (0,) * a.ndim)
    w_rt = w_r.T
    return pl.pallas_call(
        _merge_kernel,
        out_shape=[jax.ShapeDtypeStruct((b, m, d), F32), jax.ShapeDtypeStruct((b, m, d), BF16),
                   jax.ShapeDtypeStruct((b, m, ne), F32), jax.ShapeDtypeStruct((b, ne, m), F32)],
        grid=(b, nt),
        in_specs=[row(o_na.shape[2]), row(o_d.shape[2]), row(o_sw.shape[2]), row(gs.shape[2]), row(d),
                  pl.BlockSpec((1, 1, N_MOD, d), lambda bb, i: (bb, jnp.minimum(i, 1), 0, 0)),
                  full(w_na), full(w_d), full(w_sw), full(w_o), full(g2), full(w_r), full(w_rt)],
        out_specs=[row(d), row(d), row(ne), pl.BlockSpec((1, ne, TILE), lambda bb, i: (bb, 0, i))],
        compiler_params=_params(2),
        name="merge_residual_router",
    )(o_na, o_d, o_sw, gs, xu, modtab, w_na, w_d, w_sw, w_o, g2, w_r, w_rt)


def _route_kernel(a_ref, gpos_ref, gfull_ref, *, n_latent_sets, cap_lat, cap_ctx):
    ne, c, w = a_ref.shape[1:]
    a = a_ref[0]
    bits = lax.bitcast_convert_type(a, I32)
    cap = jnp.where(pl.program_id(0) < n_latent_sets, cap_lat, cap_ctx).astype(F32)

    def count(mask):
        return jnp.sum(jnp.sum(mask.astype(F32), axis=2, keepdims=True), axis=1, keepdims=True)

    def search(it, thr):
        cand = thr | jnp.left_shift(jnp.int32(1), 30 - it)
        return jnp.where(count(bits >= cand) >= cap, cand, thr)

    thr = lax.fori_loop(0, 31, search, jnp.zeros((ne, 1, 1), I32))
    gt = bits > thr
    eq = bits == thr
    need = cap - count(gt)

    upper = (lax.broadcasted_iota(I32, (w, w), 0) <= lax.broadcasted_iota(I32, (w, w), 1)).astype(BF16)
    lower = (lax.broadcasted_iota(I32, (c, c), 1) < lax.broadcasted_iota(I32, (c, c), 0)).astype(BF16)

    def exclusive_cumsum(mask):
        x = mask.astype(F32).reshape(ne * c, w)
        within = _dot(x.astype(BF16), upper)
        tot = jnp.broadcast_to(within[:, w - 1:w], (ne * c, w)).astype(BF16)
        before = jnp.concatenate([_dot(lower, tot[e * c:(e + 1) * c]) for e in range(ne)], axis=0)
        return (before + within - x).reshape(ne, c, w)

    sel = gt | (eq & (exclusive_cumsum(eq) < need))
    g = exclusive_cumsum(sel).astype(I32)
    gfull_ref[0] = g
    gpos_ref[0] = jnp.where(sel, g, -1)


def _route(aff_sets, n_latent_sets, cap_lat, cap_ctx):
    s, ne, n = aff_sets.shape
    c = n // LANES
    a4 = aff_sets.reshape(s, ne, c, LANES)
    blk = pl.BlockSpec((1, ne, c, LANES), lambda i: (i, 0, 0, 0))
    gpos, gfull = pl.pallas_call(
        functools.partial(_route_kernel, n_latent_sets=n_latent_sets, cap_lat=cap_lat, cap_ctx=cap_ctx),
        out_shape=[jax.ShapeDtypeStruct(a4.shape, I32)] * 2,
        grid=(s,),
        in_specs=[blk],
        out_specs=[blk, blk],
        compiler_params=_params(1),
        name="expert_choice_select",
    )(a4)
    return gpos.reshape(s, ne, n), gfull.reshape(s, ne, n)


def _window(cap):
    w = min(TILE + BF16_ROWS, cap)
    return w, cap - w


def _gather_kernel(a_ref, g_ref, h_ref, o_ref, *, ne, nblk, eg, win):
    b, egi, j = pl.program_id(0), pl.program_id(1), pl.program_id(2)

    @pl.when(j == 0)
    def _():
        o_ref[...] = jnp.zeros(o_ref.shape, o_ref.dtype)

    hb = h_ref[0]
    t = hb.shape[0]
    rows = lax.broadcasted_iota(I32, (win, t), 0)
    for k in range(eg):
        a = pl.multiple_of(a_ref[(b * ne + egi * eg + k) * nblk + j], BF16_ROWS)
        onehot = jnp.where(rows == g_ref[0, k] - a, 1.0, 0.0).astype(BF16)
        picked = _dot(onehot, hb).astype(BF16)
        o_ref[0, k, pl.ds(a, win), :] = o_ref[0, k, pl.ds(a, win), :] + picked


def _gather(h2u, gpos, astart, cap, tile_off, eg):
    b, ne, n = gpos.shape
    d = h2u.shape[2]
    nblk = n // TILE
    win, _ = _window(cap)
    return pl.pallas_call(
        functools.partial(_gather_kernel, ne=ne, nblk=nblk, eg=eg, win=win),
        out_shape=jax.ShapeDtypeStruct((b, ne, cap, d), BF16),
        grid_spec=pltpu.PrefetchScalarGridSpec(
            num_scalar_prefetch=1,
            grid=(b, ne // eg, nblk),
            in_specs=[pl.BlockSpec((1, eg, 1, TILE), lambda bb, e, j, a: (bb, e, 0, j)),
                      pl.BlockSpec((1, TILE, d), lambda bb, e, j, a: (bb, tile_off + j, 0))],
            out_specs=pl.BlockSpec((1, eg, cap, d), lambda bb, e, j, a: (bb, e, 0, 0))),
        compiler_params=_params(3),
        name="expert_gather",
    )(astart.reshape(-1), gpos.reshape(b, ne, 1, n), h2u)


def _ffn_kernel(x_ref, wg_ref, wu_ref, wd_ref, o_ref):
    x = x_ref[0, 0]
    gate = _dot(x, wg_ref[0])
    up = _dot(x, wu_ref[0])
    hid = (gate / (1.0 + jnp.exp(-gate)) * up).astype(BF16)
    o_ref[0, 0] = _dot(hid, wd_ref[0]).astype(BF16)


def _expert_ffn(xs, w_gate, w_up, w_down):
    b, ne, cap, d = xs.shape
    tr = min(cap, 512)
    wspec = lambda w: pl.BlockSpec((1,) + w.shape[1:], lambda e, bb, r: (e, 0, 0))
    xspec = pl.BlockSpec((1, 1, tr, d), lambda e, bb, r: (bb, e, r, 0))
    return pl.pallas_call(
        _ffn_kernel,
        out_shape=jax.ShapeDtypeStruct(xs.shape, BF16),
        grid=(ne, b, cap // tr),
        in_specs=[xspec, wspec(w_gate), wspec(w_up), wspec(w_down)],
        out_specs=xspec,
        compiler_params=_params(3),
        name="expert_swiglu",
    )(xs, w_gate, w_up, w_down)


def _combine_kernel(a_ref, x_ref, aff_ref, gt_ref, mod_ref, fg_ref, ye_ref, o_ref, win_ref, sem_ref,
                    *, ne, nblk, win, final):
    b, j = pl.program_id(0), pl.program_id(1)

    def start_of(e):
        return pl.multiple_of(a_ref[(b * ne + e) * nblk + j], BF16_ROWS)

    def window_copy(e, slot):
        return pltpu.make_async_copy(ye_ref.at[b, e, pl.ds(start_of(e), win)], win_ref.at[slot], sem_ref.at[slot])

    window_copy(0, 0).start()
    aff = aff_ref[0]
    gt = gt_ref[0]
    t = aff.shape[0]
    cols = lax.broadcasted_iota(I32, (t, win), 1)
    acc = jnp.zeros(o_ref.shape[1:], F32)
    for e in range(ne):
        slot = e % 2
        if e + 1 < ne:
            window_copy(e + 1, 1 - slot).start()
        window_copy(e, slot).wait()
        onehot = jnp.where(cols == gt[:, e:e + 1] - start_of(e), 1.0, 0.0).astype(BF16)
        acc = acc + aff[:, e:e + 1] * _dot(onehot, win_ref[slot])
    x = x_ref[0] + mod_ref[0, 0][5:6] * acc
    if final:
        x = x * lax.rsqrt(jnp.mean(x * x, axis=-1, keepdims=True) + EPS) * fg_ref[...]
    o_ref[0] = x


def _combine(xu, affu, gpos_t, astart, ye, modtab, final_g, tile_off, kind, final):
    b, m, d = xu.shape
    _, n, ne = gpos_t.shape
    cap = ye.shape[2]
    nblk = n // TILE
    win, _ = _window(cap)
    row = lambda w: pl.BlockSpec((1, TILE, w), lambda bb, j, a: (bb, tile_off + j, 0))
    out_spec = pl.BlockSpec((1, TILE, d), (lambda bb, j, a: (bb, j, 0)) if final else (lambda bb, j, a: (bb, tile_off + j, 0)))
    return pl.pallas_call(
        functools.partial(_combine_kernel, ne=ne, nblk=nblk, win=win, final=final),
        out_shape=jax.ShapeDtypeStruct((b, n, d) if final else xu.shape, F32),
        grid_spec=pltpu.PrefetchScalarGridSpec(
            num_scalar_prefetch=1,
            grid=(b, nblk),
            in_specs=[row(d), row(ne),
                      pl.BlockSpec((1, TILE, ne), lambda bb, j, a: (bb, j, 0)),
                      pl.BlockSpec((1, 1, N_MOD, d), lambda bb, j, a: (bb, kind, 0, 0)),
                      pl.BlockSpec((1, d), lambda bb, j, a: (0, 0)),
                      pl.BlockSpec(memory_space=pl.ANY)],
            out_specs=out_spec,
            scratch_shapes=[pltpu.VMEM((2, win, d), BF16), pltpu.SemaphoreType.DMA((2,))]),
        input_output_aliases={} if final else {1: 0},
        compiler_params=_params(2),
        name="expert_combine",
    )(astart.reshape(-1), xu, affu, gpos_t, modtab, final_g, ye)


def _rope_tables(n):
    t = np.arange(n)
    row = (t // GRID_W).astype(np.float32)[:, None]
    col = (t % GRID_W).astype(np.float32)[:, None]
    inv = (ROPE_BASE ** (-np.arange(ROPE_PAIRS, dtype=np.float32) / ROPE_PAIRS)).astype(np.float32)
    ang = np.concatenate([row * inv, row * inv, col * inv, col * inv], axis=-1)
    cos, sin = np.cos(ang), np.sin(ang)
    half = (np.arange(HEAD_DIM) % (2 * ROPE_PAIRS)) < ROPE_PAIRS
    sin = np.where(half[None], -sin, sin)
    cos = np.concatenate([np.ones((CTX, HEAD_DIM)), cos], axis=0)
    sin = np.concatenate([np.zeros((CTX, HEAD_DIM)), sin], axis=0)
    tile2 = lambda a: jnp.asarray(np.concatenate([a, a], axis=1), F32)
    return tile2(cos), tile2(sin)


def _moe(xu, h2u, affu, afft, modtab, weights, final_g, final):
    b, m, d = xu.shape
    n = m - CTX
    w_gate, w_up, w_down = weights
    cap_lat = max(1, EC_CAPACITY * n // N_EXPERTS)
    cap_ctx = max(1, EC_CAPACITY * CTX // N_EXPERTS)
    sets = [afft[:, :, CTX:]]
    if not final:
        sets.append(jnp.concatenate([afft[:, :, :CTX], jnp.full((b, N_EXPERTS, n - CTX), -1.0, F32)], axis=2))
    gpos, gfull = _route(jnp.concatenate(sets, axis=0), b, cap_lat, cap_ctx)

    def run(idx, n_set, cap, tile_off, kind, x_in, fin):
        gp = gpos[idx * b:(idx + 1) * b, :, :n_set]
        gf = gfull[idx * b:(idx + 1) * b, :, :n_set]
        win, max_start = _window(cap)
        astart = jnp.minimum(gf[:, :, ::TILE] // BF16_ROWS * BF16_ROWS, max_start)
        xs = _gather(h2u, gp, astart, cap, tile_off, eg=4)
        ye = _expert_ffn(xs, w_gate, w_up, w_down)
        return _combine(x_in, affu, jnp.swapaxes(gp, 1, 2), astart, ye, modtab, final_g, tile_off, kind, fin)

    if final:
        return run(0, n, cap_lat, 1, 1, xu, True)
    xu = run(0, n, cap_lat, 1, 1, xu, False)
    return run(1, CTX, cap_ctx, 0, 0, xu, False)


def kernel(x, c, ctx, c_ctx, w_mod, b_mod, norm1_g, w_in, na_rpb, diff_lambda, diff_subln_g, swa_sink,
           w_branch_na, w_branch_diff, w_branch_swa, w_out, norm2_g, w_router, w_expert_gate, w_expert_up,
           w_expert_down, final_g):
    b, n, d = x.shape
    depth = w_mod.shape[0]
    assert ctx.shape[1] == CTX and d == D_MODEL and n % (2 * TILE) == 0 and n // TILE >= 3

    cin = jnp.concatenate([c, c_ctx[None], jnp.zeros((8 - b - 1, d), F32)], axis=0)
    mod_all = _modulation(cin, w_mod, b_mod)
    cos_u, sin_u = _rope_tables(n)
    xu = jnp.concatenate([ctx, x], axis=1)
    final_g2 = final_g.reshape(1, d)

    out = None
    for l in range(depth):
        final = l == depth - 1
        lam_init = 0.8 - 0.6 * math.exp(-0.3 * l)
        mod_l = mod_all[l].reshape(8, N_MOD, d)
        modtab = jnp.stack([jnp.broadcast_to(mod_l[b], (b, N_MOD, d)), mod_l[:b]], axis=1)
        lam_p = diff_lambda[l].astype(F32)
        lam = (jnp.exp(jnp.sum(lam_p[0] * lam_p[1])) - jnp.exp(jnp.sum(lam_p[2] * lam_p[3])) + lam_init).reshape(1)

        (q_na, k_na, v_na, q_d, k_d, v_d, q_s, k_s, v_s, gs) = _project(
            xu, modtab, norm1_g[l].reshape(1, d), w_in[l].astype(BF16), cos_u, sin_u)
        o_na = _na_attention(q_na, k_na, v_na, _na_bias_table(na_rpb[l]))
        o_d = _diff_attention(q_d, k_d, v_d, lam, diff_subln_g[l].reshape(1, DIFF_V), lam_init)
        o_s = _swa_attention(q_s, k_s, v_s, swa_sink[l].astype(F32))
        xu, h2u, affu, afft = _merge(
            o_na, o_d, o_s, gs, xu, modtab, w_branch_na[l].astype(BF16), w_branch_diff[l].astype(BF16),
            w_branch_swa[l].astype(BF16), w_out[l].astype(BF16), norm2_g[l].reshape(1, d), w_router[l])
        weights = (w_expert_gate[l].astype(BF16), w_expert_up[l].astype(BF16), w_expert_down[l].astype(BF16))
        res = _moe(xu, h2u, affu, afft, modtab, weights, final_g2, final)
        if final:
            out = res
        else:
            xu = res
    return out
```

```python
import functools
import math

import numpy as np
import jax
import jax.numpy as jnp
from jax import lax
from jax.experimental import pallas as pl
from jax.experimental.pallas import tpu as pltpu

F32 = jnp.float32
BF16 = jnp.bfloat16
I32 = jnp.int32
HIGHEST = lax.Precision.HIGHEST

D_MODEL = 1024
CTX = 256
TILE = 256
GRID_W = 64
HEAD_DIM = 64
ROPE_PAIRS = HEAD_DIM // 4
ROPE_BASE = 10000.0
EPS = 1e-6
N_MOD = 6
NA_HEADS = 4
NA_WIN_ROWS = 8
NA_WIN_COLS = 16
NA_TILE_ROWS = TILE // GRID_W
DIFF_HEADS = 4
DIFF_V = 2 * HEAD_DIM
SWA_HEADS = 4
SWA_KV_HEADS = 2
SWA_WINDOW = 128
N_EXPERTS = 16
EC_CAPACITY = 2
NEG = -1e30
LANES = 128
LOG2E = math.log2(math.e)
BF16_ROWS = 16
COMBINE_MAIN_ROWS = TILE

C_QNA, C_QD, C_QS = 0, 256, 768
C_KNA, C_KD, C_KS = 1024, 1280, 1792
C_VNA, C_VD, C_VS = 1920, 2176, 2688
C_GATE, C_END = 2816, 5888

VMEM_LIMIT = 56 * 1024 * 1024


def _params(n_axes, vmem=VMEM_LIMIT):
    return pltpu.CompilerParams(dimension_semantics=("arbitrary",) * n_axes, vmem_limit_bytes=vmem)


def _nt_dot(a, b):
    return lax.dot_general(a, b, (((1,), (1,)), ((), ())), preferred_element_type=F32)


def _dot(a, b):
    return jnp.dot(a, b, preferred_element_type=F32)


def _mod_kernel(c_ref, w_ref, b_ref, o_ref):
    c = c_ref[...]
    s = c / (1.0 + jnp.exp(-c))
    o_ref[0] = jnp.dot(s, w_ref[0], preferred_element_type=F32, precision=HIGHEST) + b_ref[0]


def _modulation(cin, w_mod, b_mod):
    depth, d, w = w_mod.shape
    tn = 1024
    return pl.pallas_call(
        _mod_kernel,
        out_shape=jax.ShapeDtypeStruct((depth, 8, w), F32),
        grid=(depth, w // tn),
        in_specs=[pl.BlockSpec((8, d), lambda l, j: (0, 0)),
                  pl.BlockSpec((1, d, tn), lambda l, j: (l, 0, j)),
                  pl.BlockSpec((1, 1, tn), lambda l, j: (l, 0, j))],
        out_specs=pl.BlockSpec((1, 8, tn), lambda l, j: (l, 0, j)),
        compiler_params=_params(2),
        name="modulation",
    )(cin, w_mod, b_mod.reshape(depth, 1, w))


def _proj_kernel(x_ref, mod_ref, g_ref, w_ref, cos_ref, sin_ref,
                 qna_ref, kna_ref, vna_ref, qd_ref, kd_ref, vd_ref, qs_ref, ks_ref, vs_ref, gs_ref):
    x = x_ref[0]
    mod = mod_ref[0, 0]
    y = x * lax.rsqrt(jnp.mean(x * x, axis=-1, keepdims=True) + EPS) * g_ref[...]
    h = (y * (1.0 + mod[1:2]) + mod[0:1]).astype(BF16)

    def proj(a, b):
        return _dot(h, w_ref[:, a:b])

    cos = cos_ref[...]
    sin = sin_ref[...]
    lane = lax.broadcasted_iota(I32, cos.shape, 1)
    first_half = (lane % (2 * ROPE_PAIRS)) < ROPE_PAIRS

    def rope(t):
        outs = []
        for j in range(t.shape[1] // LANES):
            c = t[:, j * LANES:(j + 1) * LANES]
            r = jnp.where(first_half, pltpu.roll(c, LANES - ROPE_PAIRS, 1), pltpu.roll(c, ROPE_PAIRS, 1))
            outs.append(c * cos + r * sin)
        return outs[0] if len(outs) == 1 else jnp.concatenate(outs, axis=1)

    scale = HEAD_DIM ** -0.5
    qna_ref[0] = (proj(C_QNA, C_QD) * scale).astype(BF16)
    qd_ref[0] = (rope(proj(C_QD, C_QS)) * (scale * LOG2E)).astype(BF16)
    qs_ref[0] = (rope(proj(C_QS, C_KNA)) * scale).astype(BF16)
    kna_ref[0] = proj(C_KNA, C_KD).astype(BF16)
    kd_ref[0] = rope(proj(C_KD, C_KS)).astype(BF16)
    ks_ref[0] = rope(proj(C_KS, C_VNA)).astype(BF16)
    vna_ref[0] = proj(C_VNA, C_VD).astype(BF16)
    vd = proj(C_VD, C_VS).astype(BF16)
    ones = jnp.ones((vd.shape[0], DIFF_V), BF16)
    vd_ref[0] = jnp.concatenate(
        [blk for hd in range(DIFF_HEADS) for blk in (vd[:, hd * DIFF_V:(hd + 1) * DIFF_V], ones)], axis=1)
    vs_ref[0] = proj(C_VS, C_GATE).astype(BF16)
    gates = proj(C_GATE, C_END)
    gs_ref[0] = (1.0 / (1.0 + jnp.exp(-gates))).astype(BF16)


def _project(xu, modtab, g, w_in, cos_u, sin_u):
    b, m, d = xu.shape
    nt = m // TILE
    widths = (256, 256, 256, 512, 512, 2 * DIFF_HEADS * DIFF_V, 256, 128, 128, C_END - C_GATE)
    row = lambda w: pl.BlockSpec((1, TILE, w), lambda bb, i: (bb, i, 0))
    return pl.pallas_call(
        _proj_kernel,
        out_shape=[jax.ShapeDtypeStruct((b, m, w), BF16) for w in widths],
        grid=(b, nt),
        in_specs=[row(d),
                  pl.BlockSpec((1, 1, N_MOD, d), lambda bb, i: (bb, jnp.minimum(i, 1), 0, 0)),
                  pl.BlockSpec((1, d), lambda bb, i: (0, 0)),
                  pl.BlockSpec((d, C_END), lambda bb, i: (0, 0)),
                  pl.BlockSpec((TILE, LANES), lambda bb, i: (i, 0)),
                  pl.BlockSpec((TILE, LANES), lambda bb, i: (i, 0))],
        out_specs=[row(w) for w in widths],
        compiler_params=_params(2),
        name="norm_project_rope",
    )(xu, modtab, g, w_in, cos_u, sin_u)


def _na_kernel(q_ref, kc_ref, kp_ref, kcur_ref, kn_ref, vc_ref, vp_ref, vcur_ref, vn_ref, bias_ref, o_ref):
    q = q_ref[0]
    kc, vc = kc_ref[0], vc_ref[0]
    kl = jnp.concatenate([kp_ref[0], kcur_ref[0], kn_ref[0]], axis=0)
    vl = jnp.concatenate([vp_ref[0], vcur_ref[0], vn_ref[0]], axis=0)
    outs = []
    for hd in range(NA_HEADS):
        sl = slice(hd * HEAD_DIM, (hd + 1) * HEAD_DIM)
        qh = q[:, sl]
        s_c = _nt_dot(qh, kc[:, sl])
        s_l = _nt_dot(qh, kl[:, sl]) + bias_ref[0, hd]
        mx = jnp.maximum(jnp.max(s_c, axis=1, keepdims=True), jnp.max(s_l, axis=1, keepdims=True))
        p_c = jnp.exp(s_c - mx)
        p_l = jnp.exp(s_l - mx)
        den = jnp.sum(p_c, axis=1, keepdims=True) + jnp.sum(p_l, axis=1, keepdims=True)
        o = _dot(p_c.astype(BF16), vc[:, sl]) + _dot(p_l.astype(BF16), vl[:, sl])
        outs.append(o / den)
    o_ref[0] = jnp.concatenate(outs, axis=1).astype(BF16)


def _na_bias_table(rpb):
    tr, nk = NA_TILE_ROWS, 3 * NA_TILE_ROWS
    qc = np.arange(GRID_W)[:, None]
    kc = np.arange(GRID_W)[None, :]
    cstart = np.clip(qc - NA_WIN_COLS // 2, 0, GRID_W - NA_WIN_COLS)
    col_ok = (kc >= cstart) & (kc < cstart + NA_WIN_COLS)
    dc = np.clip(kc - qc, -(NA_WIN_COLS - 1), NA_WIN_COLS - 1) + NA_WIN_COLS - 1
    onehot = jnp.asarray(np.arange(2 * NA_WIN_COLS - 1)[:, None, None] == dc[None], F32)
    cols = jnp.einsum("hrc,cqk->hrqk", rpb.astype(F32), onehot, precision=HIGHEST)
    qr = np.arange(tr)[:, None]
    krow = np.arange(nk)[None, :] - tr
    dr = np.clip(krow - qr, -(NA_WIN_ROWS - 1), NA_WIN_ROWS - 1) + NA_WIN_ROWS - 1
    starts = (0 * qr, qr - NA_WIN_ROWS // 2, 0 * qr + tr - NA_WIN_ROWS)
    row_ok = np.stack([(krow >= st) & (krow < st + NA_WIN_ROWS) for st in starts])
    t = jnp.take(cols, jnp.asarray(dr.reshape(-1)), axis=1)
    t = t.reshape(NA_HEADS, tr, nk, GRID_W, GRID_W).transpose(0, 1, 3, 2, 4)
    ok = row_ok[:, None, :, None, :, None] & col_ok[None, None, None, :, None, :]
    t = jnp.where(ok, t[None], NEG).reshape(3, NA_HEADS, TILE, 3 * TILE)
    return jnp.concatenate([jnp.full_like(t[:1], NEG), t], axis=0)


def _na_attention(q, k, v, bias):
    b, m, w = q.shape
    nt = m // TILE
    nb = nt - 1

    def lat(off):
        return lambda bb, i: (bb, jnp.clip(i - 1 + off, 0, nb - 1) + 1, 0)

    def kind(bb, i):
        return (jnp.where(i == 0, 0, jnp.where(i == 1, 1, jnp.where(i == nb, 3, 2))), 0, 0, 0)

    blk = lambda f: pl.BlockSpec((1, TILE, w), f)
    kv_specs = [blk(lambda bb, i: (bb, 0, 0)), blk(lat(-1)), blk(lat(0)), blk(lat(1))]
    return pl.pallas_call(
        _na_kernel,
        out_shape=jax.ShapeDtypeStruct((b, m, w), BF16),
        grid=(b, nt),
        in_specs=[blk(lambda bb, i: (bb, i, 0))] + kv_specs + kv_specs
                 + [pl.BlockSpec((1, NA_HEADS, TILE, 3 * TILE), kind)],
        out_specs=blk(lambda bb, i: (bb, i, 0)),
        compiler_params=_params(2),
        name="neighbourhood_attention",
    )(q, k, k, k, k, v, v, v, v, bias)


def _swa_kernel(sink_ref, q_ref, kc_ref, kp_ref, kcur_ref, kn_ref, vc_ref, vp_ref, vcur_ref, vn_ref, o_ref,
                *, n_lat):
    i = pl.program_id(1)
    q = q_ref[0]
    kc, vc = kc_ref[0], vc_ref[0]
    kl = jnp.concatenate([kp_ref[0], kcur_ref[0], kn_ref[0]], axis=0)
    vl = jnp.concatenate([vp_ref[0], vcur_ref[0], vn_ref[0]], axis=0)
    group = SWA_HEADS // SWA_KV_HEADS
    rows = group * TILE
    qpos = (i - 1) * TILE + lax.broadcasted_iota(I32, (rows, 3 * TILE), 0) % TILE
    kpos = (i - 2) * TILE + lax.broadcasted_iota(I32, (rows, 3 * TILE), 1)
    ok = (kpos >= 0) & (kpos < n_lat) & (jnp.abs(qpos - kpos) <= SWA_WINDOW) & (i > 0)
    rid = lax.broadcasted_iota(I32, (rows, 1), 0)
    outs = []
    for g in range(SWA_KV_HEADS):
        ksl = slice(g * HEAD_DIM, (g + 1) * HEAD_DIM)
        qg = jnp.concatenate([q[:, (g * group + j) * HEAD_DIM:(g * group + j + 1) * HEAD_DIM]
                              for j in range(group)], axis=0)
        sink = jnp.zeros((rows, 1), F32)
        for j in range(group):
            sink = jnp.where(rid // TILE == j, sink_ref[g * group + j], sink)
        s_c = _nt_dot(qg, kc[:, ksl])
        s_l = jnp.where(ok, _nt_dot(qg, kl[:, ksl]), NEG)
        mx = jnp.maximum(jnp.maximum(jnp.max(s_c, axis=1, keepdims=True), jnp.max(s_l, axis=1, keepdims=True)), sink)
        p_c = jnp.exp(s_c - mx)
        p_l = jnp.exp(s_l - mx)
        den = jnp.sum(p_c, axis=1, keepdims=True) + jnp.sum(p_l, axis=1, keepdims=True) + jnp.exp(sink - mx)
        o = (_dot(p_c.astype(BF16), vc[:, ksl]) + _dot(p_l.astype(BF16), vl[:, ksl])) / den
        outs.extend(o[j * TILE:(j + 1) * TILE] for j in range(group))
    o_ref[0] = jnp.concatenate(outs, axis=1).astype(BF16)


def _swa_attention(q, k, v, sink):
    b, m, wq = q.shape
    wk = k.shape[2]
    nt = m // TILE
    nb = nt - 1

    def lat(off):
        return lambda bb, i: (bb, jnp.clip(i - 1 + off, 0, nb - 1) + 1, 0)

    blk = lambda w, f: pl.BlockSpec((1, TILE, w), f)
    kv_specs = [blk(wk, lambda bb, i: (bb, 0, 0)), blk(wk, lat(-1)), blk(wk, lat(0)), blk(wk, lat(1))]
    return pl.pallas_call(
        functools.partial(_swa_kernel, n_lat=m - CTX),
        out_shape=jax.ShapeDtypeStruct((b, m, wq), BF16),
        grid=(b, nt),
        in_specs=[pl.BlockSpec(memory_space=pltpu.SMEM), blk(wq, lambda bb, i: (bb, i, 0))] + kv_specs + kv_specs,
        out_specs=blk(wq, lambda bb, i: (bb, i, 0)),
        compiler_params=_params(2),
        name="windowed_attention",
    )(sink, q, k, k, k, k, v, v, v, v)


def _diff_kernel(lam_ref, q_ref, k_ref, v_ref, g_ref, o_ref, acc_ref, m_ref, s_ref, cmax_ref,
                 *, n_chunks, kblk, lam_init):
    i = pl.program_id(2)
    q = q_ref[0]
    lane = lax.broadcasted_iota(I32, q.shape, 1)
    zero = jnp.zeros_like(q)
    qq = jnp.concatenate([jnp.where(lane < HEAD_DIM, q, zero), jnp.where(lane >= HEAD_DIM, q, zero)], axis=0)

    def chunk_start(c):
        return pl.multiple_of(CTX + c * kblk, CTX)

    def scores(c, slot):
        s = _nt_dot(qq, k_ref[0, pl.ds(chunk_start(c), kblk), :])
        s_ref[slot] = s
        cmax_ref[slot] = jnp.max(s, axis=1, keepdims=True)

    def accumulate(s, cmax, v):
        m_prev = m_ref[...]
        m_new = jnp.maximum(m_prev, cmax)
        p = jnp.exp2(s - m_new).astype(BF16)
        acc_ref[...] = jnp.exp2(m_prev - m_new) * acc_ref[...] + _dot(p, v)
        m_ref[...] = m_new

    m_ref[...] = jnp.full(m_ref.shape, NEG, F32)
    acc_ref[...] = jnp.zeros(acc_ref.shape, F32)
    scores(0, 0)
    s_ctx = _nt_dot(qq, k_ref[0, 0:CTX, :])
    accumulate(s_ctx, jnp.max(s_ctx, axis=1, keepdims=True), v_ref[0, 0:CTX, :])

    def body(c2, carry):
        for cur in (0, 1):
            c = 2 * c2 + cur
            scores(jnp.minimum(c + 1, n_chunks - 1), 1 - cur)
            accumulate(s_ref[cur], cmax_ref[cur], v_ref[0, pl.ds(chunk_start(c), kblk), :])
        return carry

    lax.fori_loop(0, jnp.where(i == 0, 0, n_chunks // 2), body, 0)

    acc = acc_ref[...]
    o = acc[:, :DIFF_V] / acc[:, DIFF_V:]
    d = o[:TILE] - lam_ref[0] * o[TILE:]
    y = d * lax.rsqrt(jnp.mean(d * d, axis=-1, keepdims=True) + EPS) * g_ref[...]
    o_ref[0] = (y * (1.0 - lam_init)).astype(BF16)


def _diff_attention(q, k, v, lam, g, lam_init, kblk=2048):
    b, m, _ = q.shape
    nt = m // TILE
    n_lat = m - CTX
    kblk = min(kblk, n_lat // 2)
    assert n_lat % (2 * kblk) == 0
    return pl.pallas_call(
        functools.partial(_diff_kernel, n_chunks=n_lat // kblk, kblk=kblk, lam_init=lam_init),
        out_shape=jax.ShapeDtypeStruct((b, m, DIFF_HEADS * DIFF_V), BF16),
        grid=(b, DIFF_HEADS, nt),
        in_specs=[pl.BlockSpec(memory_space=pltpu.SMEM),
                  pl.BlockSpec((1, TILE, 2 * HEAD_DIM), lambda bb, hh, i: (bb, i, hh)),
                  pl.BlockSpec((1, m, 2 * HEAD_DIM), lambda bb, hh, i: (bb, 0, hh)),
                  pl.BlockSpec((1, m, 2 * DIFF_V), lambda bb, hh, i: (bb, 0, hh)),
                  pl.BlockSpec((1, DIFF_V), lambda bb, hh, i: (0, 0))],
        out_specs=pl.BlockSpec((1, TILE, DIFF_V), lambda bb, hh, i: (bb, i, hh)),
        scratch_shapes=[pltpu.VMEM((2 * TILE, 2 * DIFF_V), F32), pltpu.VMEM((2 * TILE, 1), F32),
                        pltpu.VMEM((2, 2 * TILE, kblk), F32), pltpu.VMEM((2, 2 * TILE, 1), F32)],
        compiler_params=_params(3),
        name="differential_attention",
    )(lam, q, k, v, g)


def _merge_kernel(ona_ref, od_ref, osw_ref, gs_ref, x_ref, mod_ref, wna_ref, wd_ref, wsw_ref, wo_ref,
                  g2_ref, wr_ref, wrt_ref, xo_ref, h2_ref, aff_ref, afft_ref):
    d = x_ref.shape[2]
    gs = gs_ref[0]
    y = (gs[:, :d].astype(F32) * _dot(ona_ref[0], wna_ref[...])
         + gs[:, d:2 * d].astype(F32) * _dot(od_ref[0], wd_ref[...])
         + gs[:, 2 * d:].astype(F32) * _dot(osw_ref[0], wsw_ref[...]))
    mod = mod_ref[0, 0]
    xn = x_ref[0] + mod[2:3] * _dot(y.astype(BF16), wo_ref[...])
    xo_ref[0] = xn
    r = xn * lax.rsqrt(jnp.mean(xn * xn, axis=-1, keepdims=True) + EPS) * g2_ref[...]
    h2 = r * (1.0 + mod[4:5]) + mod[3:4]
    h2_ref[0] = h2.astype(BF16)
    logits = jnp.dot(h2, wr_ref[...], preferred_element_type=F32, precision=HIGHEST)
    e = jnp.exp(logits - jnp.max(logits, axis=1, keepdims=True))
    aff_ref[0] = e / jnp.sum(e, axis=1, keepdims=True)
    lt = lax.dot_general(wrt_ref[...], h2, (((1,), (1,)), ((), ())),
                         preferred_element_type=F32, precision=HIGHEST)
    et = jnp.exp(lt - jnp.max(lt, axis=0, keepdims=True))
    afft_ref[0] = et / jnp.sum(et, axis=0, keepdims=True)


def _merge(o_na, o_d, o_sw, gs, xu, modtab, w_na, w_d, w_sw, w_o, g2, w_r):
    b, m, d = xu.shape
    nt = m // TILE
    ne = w_r.shape[1]
    row = lambda w: pl.BlockSpec((1, TILE, w), lambda bb, i: (bb, i, 0))
    full = lambda a: pl.BlockSpec(a.shape, lambda bb, i: (0,) * a.ndim)
    w_rt = w_r.T
    return pl.pallas_call(
        _merge_kernel,
        out_shape=[jax.ShapeDtypeStruct((b, m, d), F32), jax.ShapeDtypeStruct((b, m, d), BF16),
                   jax.ShapeDtypeStruct((b, m, ne), F32), jax.ShapeDtypeStruct((b, ne, m), F32)],
        grid=(b, nt),
        in_specs=[row(o_na.shape[2]), row(o_d.shape[2]), row(o_sw.shape[2]), row(gs.shape[2]), row(d),
                  pl.BlockSpec((1, 1, N_MOD, d), lambda bb, i: (bb, jnp.minimum(i, 1), 0, 0)),
                  full(w_na), full(w_d), full(w_sw), full(w_o), full(g2), full(w_r), full(w_rt)],
        out_specs=[row(d), row(d), row(ne), pl.BlockSpec((1, ne, TILE), lambda bb, i: (bb, 0, i))],
        compiler_params=_params(2),
        name="merge_residual_router",
    )(o_na, o_d, o_sw, gs, xu, modtab, w_na, w_d, w_sw, w_o, g2, w_r, w_rt)


def _route_kernel(a_ref, gpos_ref, gfull_ref, *, n_latent_sets, cap_lat, cap_ctx):
    ne, c, w = a_ref.shape[1:]
    a = a_ref[0]
    bits = lax.bitcast_convert_type(a, I32)
    cap = jnp.where(pl.program_id(0) < n_latent_sets, cap_lat, cap_ctx).astype(F32)

    def count(mask):
        return jnp.sum(jnp.sum(mask.astype(F32), axis=2, keepdims=True), axis=1, keepdims=True)

    def search(it, thr):
        cand = thr | jnp.left_shift(jnp.int32(1), 30 - it)
        return jnp.where(count(bits >= cand) >= cap, cand, thr)

    thr = lax.fori_loop(0, 31, search, jnp.zeros((ne, 1, 1), I32))
    gt = bits > thr
    eq = bits == thr
    need = cap - count(gt)

    upper = (lax.broadcasted_iota(I32, (w, w), 0) <= lax.broadcasted_iota(I32, (w, w), 1)).astype(BF16)
    lower = (lax.broadcasted_iota(I32, (c, c), 1) < lax.broadcasted_iota(I32, (c, c), 0)).astype(BF16)

    def exclusive_cumsum(mask):
        x = mask.astype(F32).reshape(ne * c, w)
        within = _dot(x.astype(BF16), upper)
        tot = jnp.broadcast_to(within[:, w - 1:w], (ne * c, w)).astype(BF16)
        before = jnp.concatenate([_dot(lower, tot[e * c:(e + 1) * c]) for e in range(ne)], axis=0)
        return (before + within - x).reshape(ne, c, w)

    sel = gt | (eq & (exclusive_cumsum(eq) < need))
    g = exclusive_cumsum(sel).astype(I32)
    gfull_ref[0] = g
    gpos_ref[0] = jnp.where(sel, g, -1)


def _route(aff_sets, n_latent_sets, cap_lat, cap_ctx):
    s, ne, n = aff_sets.shape
    c = n // LANES
    a4 = aff_sets.reshape(s, ne, c, LANES)
    blk = pl.BlockSpec((1, ne, c, LANES), lambda i: (i, 0, 0, 0))
    gpos, gfull = pl.pallas_call(
        functools.partial(_route_kernel, n_latent_sets=n_latent_sets, cap_lat=cap_lat, cap_ctx=cap_ctx),
        out_shape=[jax.ShapeDtypeStruct(a4.shape, I32)] * 2,
        grid=(s,),
        in_specs=[blk],
        out_specs=[blk, blk],
        compiler_params=_params(1),
        name="expert_choice_select",
    )(a4)
    return gpos.reshape(s, ne, n), gfull.reshape(s, ne, n)


def _window(cap):
    w = min(TILE + BF16_ROWS, cap)
    return w, cap - w


def _gather_kernel(a_ref, g_ref, h_ref, o_ref, *, ne, nblk, eg, win):
    b, egi, j = pl.program_id(0), pl.program_id(1), pl.program_id(2)

    @pl.when(j == 0)
    def _():
        o_ref[...] = jnp.zeros(o_ref.shape, o_ref.dtype)

    hb = h_ref[0]
    t = hb.shape[0]
    rows = lax.broadcasted_iota(I32, (win, t), 0)
    for k in range(eg):
        a = pl.multiple_of(a_ref[(b * ne + egi * eg + k) * nblk + j], BF16_ROWS)
        onehot = jnp.where(rows == g_ref[0, k] - a, 1.0, 0.0).astype(BF16)
        picked = _dot(onehot, hb).astype(BF16)
        o_ref[0, k, pl.ds(a, win), :] = o_ref[0, k, pl.ds(a, win), :] + picked


def _gather(h2u, gpos, astart, cap, tile_off, eg):
    b, ne, n = gpos.shape
    d = h2u.shape[2]
    nblk = n // TILE
    win, _ = _window(cap)
    return pl.pallas_call(
        functools.partial(_gather_kernel, ne=ne, nblk=nblk, eg=eg, win=win),
        out_shape=jax.ShapeDtypeStruct((b, ne, cap, d), BF16),
        grid_spec=pltpu.PrefetchScalarGridSpec(
            num_scalar_prefetch=1,
            grid=(b, ne // eg, nblk),
            in_specs=[pl.BlockSpec((1, eg, 1, TILE), lambda bb, e, j, a: (bb, e, 0, j)),
                      pl.BlockSpec((1, TILE, d), lambda bb, e, j, a: (bb, tile_off + j, 0))],
            out_specs=pl.BlockSpec((1, eg, cap, d), lambda bb, e, j, a: (bb, e, 0, 0))),
        compiler_params=_params(3),
        name="expert_gather",
    )(astart.reshape(-1), gpos.reshape(b, ne, 1, n), h2u)


def _ffn_kernel(x_ref, wg_ref, wu_ref, wd_ref, o_ref):
    x = x_ref[0, 0]
    gate = _dot(x, wg_ref[0])
    up = _dot(x, wu_ref[0])
    hid = (gate / (1.0 + jnp.exp(-gate)) * up).astype(BF16)
    o_ref[0, 0] = _dot(hid, wd_ref[0]).astype(BF16)


def _expert_ffn(xs, w_gate, w_up, w_down):
    b, ne, cap, d = xs.shape
    tr = min(cap, 512)
    wspec = lambda w: pl.BlockSpec((1,) + w.shape[1:], lambda e, bb, r: (e, 0, 0))
    xspec = pl.BlockSpec((1, 1, tr, d), lambda e, bb, r: (bb, e, r, 0))
    return pl.pallas_call(
        _ffn_kernel,
        out_shape=jax.ShapeDtypeStruct(xs.shape, BF16),
        grid=(ne, b, cap // tr),
        in_specs=[xspec, wspec(w_gate), wspec(w_up), wspec(w_down)],
        out_specs=xspec,
        compiler_params=_params(3),
        name="expert_swiglu",
    )(xs, w_gate, w_up, w_down)


def _combine_kernel(a_ref, t_ref, x_ref, aff_ref, gt_ref, mod_ref, fg_ref, ye_ref, o_ref,
                    win_ref, tail_ref, tacc_ref, sem_ref, tsem_ref, *, ne, nblk, main, tail, final):
    b, j = pl.program_id(0), pl.program_id(1)

    def start_of(e):
        return pl.multiple_of(a_ref[(b * ne + e) * nblk + j], BF16_ROWS)

    def window_copy(e, slot):
        return pltpu.make_async_copy(ye_ref.at[b, e, pl.ds(start_of(e), main)], win_ref.at[slot], sem_ref.at[slot])

    window_copy(0, 0).start()
    aff = aff_ref[0]
    gt = gt_ref[0]
    t = aff.shape[0]
    cols = lax.broadcasted_iota(I32, (t, main), 1)
    acc = jnp.zeros(o_ref.shape[1:], F32)
    if tail:
        tacc_ref[...] = jnp.zeros(tacc_ref.shape, F32)
    for e in range(ne):
        slot = e % 2
        if e + 1 < ne:
            window_copy(e + 1, 1 - slot).start()
        window_copy(e, slot).wait()
        pos = gt[:, e:e + 1] - start_of(e)
        onehot = jnp.where(cols == pos, 1.0, 0.0).astype(BF16)
        acc = acc + aff[:, e:e + 1] * _dot(onehot, win_ref[slot])
        if tail:
            @pl.when(t_ref[(b * ne + e) * nblk + j] != 0)
            def _(e=e, pos=pos):
                cp = pltpu.make_async_copy(ye_ref.at[b, e, pl.ds(start_of(e) + main, tail)], tail_ref, tsem_ref.at[0])
                cp.start()
                cp.wait()
                tcols = lax.broadcasted_iota(I32, (t, tail), 1)
                oh = jnp.where(tcols == pos - main, 1.0, 0.0).astype(BF16)
                tacc_ref[...] += aff[:, e:e + 1] * _dot(oh, tail_ref[...])
    if tail:
        acc = acc + tacc_ref[...]
    x = x_ref[0] + mod_ref[0, 0][5:6] * acc
    if final:
        x = x * lax.rsqrt(jnp.mean(x * x, axis=-1, keepdims=True) + EPS) * fg_ref[...]
    o_ref[0] = x


def _combine(xu, affu, gpos_t, astart, gfull, ye, modtab, final_g, tile_off, kind, final):
    b, m, d = xu.shape
    _, n, ne = gpos_t.shape
    cap = ye.shape[2]
    nblk = n // TILE
    win, _ = _window(cap)
    main = min(COMBINE_MAIN_ROWS, win)
    tail = win - main
    seg_end = jnp.concatenate([gfull[:, :, TILE::TILE], jnp.full((b, ne, 1), cap, I32)], axis=2)
    need_tail = (seg_end > astart + main).astype(I32)
    row = lambda w: pl.BlockSpec((1, TILE, w), lambda bb, j, a, nt: (bb, tile_off + j, 0))
    out_spec = pl.BlockSpec((1, TILE, d), (lambda bb, j, a, nt: (bb, j, 0)) if final
                            else (lambda bb, j, a, nt: (bb, tile_off + j, 0)))
    return pl.pallas_call(
        functools.partial(_combine_kernel, ne=ne, nblk=nblk, main=main, tail=tail, final=final),
        out_shape=jax.ShapeDtypeStruct((b, n, d) if final else xu.shape, F32),
        grid_spec=pltpu.PrefetchScalarGridSpec(
            num_scalar_prefetch=2,
            grid=(b, nblk),
            in_specs=[row(d), row(ne),
                      pl.BlockSpec((1, TILE, ne), lambda bb, j, a, nt: (bb, j, 0)),
                      pl.BlockSpec((1, 1, N_MOD, d), lambda bb, j, a, nt: (bb, kind, 0, 0)),
                      pl.BlockSpec((1, d), lambda bb, j, a, nt: (0, 0)),
                      pl.BlockSpec(memory_space=pl.ANY)],
            out_specs=out_spec,
            scratch_shapes=[pltpu.VMEM((2, main, d), BF16), pltpu.VMEM((max(tail, BF16_ROWS), d), BF16),
                            pltpu.VMEM((TILE, d), F32), pltpu.SemaphoreType.DMA((2,)),
                            pltpu.SemaphoreType.DMA((1,))]),
        input_output_aliases={} if final else {2: 0},
        compiler_params=_params(2),
        name="expert_combine",
    )(astart.reshape(-1), need_tail.reshape(-1), xu, affu, gpos_t, modtab, final_g, ye)


def _rope_tables(n):
    t = np.arange(n)
    row = (t // GRID_W).astype(np.float32)[:, None]
    col = (t % GRID_W).astype(np.float32)[:, None]
    inv = (ROPE_BASE ** (-np.arange(ROPE_PAIRS, dtype=np.float32) / ROPE_PAIRS)).astype(np.float32)
    ang = np.concatenate([row * inv, row * inv, col * inv, col * inv], axis=-1)
    cos, sin = np.cos(ang), np.sin(ang)
    half = (np.arange(HEAD_DIM) % (2 * ROPE_PAIRS)) < ROPE_PAIRS
    sin = np.where(half[None], -sin, sin)
    cos = np.concatenate([np.ones((CTX, HEAD_DIM)), cos], axis=0)
    sin = np.concatenate([np.zeros((CTX, HEAD_DIM)), sin], axis=0)
    tile2 = lambda a: jnp.asarray(np.concatenate([a, a], axis=1), F32)
    return tile2(cos), tile2(sin)


def _moe(xu, h2u, affu, afft, modtab, weights, final_g, final):
    b, m, d = xu.shape
    n = m - CTX
    w_gate, w_up, w_down = weights
    cap_lat = max(1, EC_CAPACITY * n // N_EXPERTS)
    cap_ctx = max(1, EC_CAPACITY * CTX // N_EXPERTS)
    sets = [afft[:, :, CTX:]]
    if not final:
        sets.append(jnp.concatenate([afft[:, :, :CTX], jnp.full((b, N_EXPERTS, n - CTX), -1.0, F32)], axis=2))
    gpos, gfull = _route(jnp.concatenate(sets, axis=0), b, cap_lat, cap_ctx)

    def run(idx, n_set, cap, tile_off, kind, x_in, fin):
        gp = gpos[idx * b:(idx + 1) * b, :, :n_set]
        gf = gfull[idx * b:(idx + 1) * b, :, :n_set]
        win, max_start = _window(cap)
        astart = jnp.minimum(gf[:, :, ::TILE] // BF16_ROWS * BF16_ROWS, max_start)
        xs = _gather(h2u, gp, astart, cap, tile_off, eg=4)
        ye = _expert_ffn(xs, w_gate, w_up, w_down)
        return _combine(x_in, affu, jnp.swapaxes(gp, 1, 2), astart, gf, ye, modtab, final_g, tile_off, kind, fin)

    if final:
        return run(0, n, cap_lat, 1, 1, xu, True)
    xu = run(0, n, cap_lat, 1, 1, xu, False)
    return run(1, CTX, cap_ctx, 0, 0, xu, False)


def kernel(x, c, ctx, c_ctx, w_mod, b_mod, norm1_g, w_in, na_rpb, diff_lambda, diff_subln_g, swa_sink,
           w_branch_na, w_branch_diff, w_branch_swa, w_out, norm2_g, w_router, w_expert_gate, w_expert_up,
           w_expert_down, final_g):
    b, n, d = x.shape
    depth = w_mod.shape[0]
    assert ctx.shape[1] == CTX and d == D_MODEL and n % (2 * TILE) == 0 and n // TILE >= 3

    cin = jnp.concatenate([c, c_ctx[None], jnp.zeros((8 - b - 1, d), F32)], axis=0)
    mod_all = _modulation(cin, w_mod, b_mod)
    cos_u, sin_u = _rope_tables(n)
    xu = jnp.concatenate([ctx, x], axis=1)
    final_g2 = final_g.reshape(1, d)

    out = None
    for l in range(depth):
        final = l == depth - 1
        lam_init = 0.8 - 0.6 * math.exp(-0.3 * l)
        mod_l = mod_all[l].reshape(8, N_MOD, d)
        modtab = jnp.stack([jnp.broadcast_to(mod_l[b], (b, N_MOD, d)), mod_l[:b]], axis=1)
        lam_p = diff_lambda[l].astype(F32)
        lam = (jnp.exp(jnp.sum(lam_p[0] * lam_p[1])) - jnp.exp(jnp.sum(lam_p[2] * lam_p[3])) + lam_init).reshape(1)

        (q_na, k_na, v_na, q_d, k_d, v_d, q_s, k_s, v_s, gs) = _project(
            xu, modtab, norm1_g[l].reshape(1, d), w_in[l].astype(BF16), cos_u, sin_u)
        o_na = _na_attention(q_na, k_na, v_na, _na_bias_table(na_rpb[l]))
        o_d = _diff_attention(q_d, k_d, v_d, lam, diff_subln_g[l].reshape(1, DIFF_V), lam_init)
        o_s = _swa_attention(q_s, k_s, v_s, swa_sink[l].astype(F32))
        xu, h2u, affu, afft = _merge(
            o_na, o_d, o_s, gs, xu, modtab, w_branch_na[l].astype(BF16), w_branch_diff[l].astype(BF16),
            w_branch_swa[l].astype(BF16), w_out[l].astype(BF16), norm2_g[l].reshape(1, d), w_router[l])
        weights = (w_expert_gate[l].astype(BF16), w_expert_up[l].astype(BF16), w_expert_down[l].astype(BF16))
        res = _moe(xu, h2u, affu, afft, modtab, weights, final_g2, final)
        if final:
            out = res
        else:
            xu = res
    return out
```

```python
import functools
import math

import numpy as np
import jax
import jax.numpy as jnp
from jax import lax
from jax.experimental import pallas as pl
from jax.experimental.pallas import tpu as pltpu

F32 = jnp.float32
BF16 = jnp.bfloat16
I32 = jnp.int32
HIGHEST = lax.Precision.HIGHEST

D_MODEL = 1024
CTX = 256
TILE = 256
GRID_W = 64
HEAD_DIM = 64
ROPE_PAIRS = HEAD_DIM // 4
ROPE_BASE = 10000.0
EPS = 1e-6
N_MOD = 6
NA_HEADS = 4
NA_WIN_ROWS = 8
NA_WIN_COLS = 16
NA_TILE_ROWS = TILE // GRID_W
DIFF_HEADS = 4
DIFF_V = 2 * HEAD_DIM
SWA_HEADS = 4
SWA_KV_HEADS = 2
SWA_WINDOW = 128
N_EXPERTS = 16
EC_CAPACITY = 2
NEG = -1e30
LANES = 128
LOG2E = math.log2(math.e)
BF16_ROWS = 16
WINDOW_MAIN_ROWS = 128

C_QNA, C_QD, C_QS = 0, 256, 768
C_KNA, C_KD, C_KS = 1024, 1280, 1792
C_VNA, C_VD, C_VS = 1920, 2176, 2688
C_GATE, C_END = 2816, 5888

VMEM_LIMIT = 56 * 1024 * 1024


def _params(n_axes, vmem=VMEM_LIMIT):
    return pltpu.CompilerParams(dimension_semantics=("arbitrary",) * n_axes, vmem_limit_bytes=vmem)


def _nt_dot(a, b):
    return lax.dot_general(a, b, (((1,), (1,)), ((), ())), preferred_element_type=F32)


def _dot(a, b):
    return jnp.dot(a, b, preferred_element_type=F32)


def _mod_kernel(c_ref, w_ref, b_ref, o_ref):
    c = c_ref[...]
    s = c / (1.0 + jnp.exp(-c))
    o_ref[0] = jnp.dot(s, w_ref[0], preferred_element_type=F32, precision=HIGHEST) + b_ref[0]


def _modulation(cin, w_mod, b_mod):
    depth, d, w = w_mod.shape
    tn = 1024
    return pl.pallas_call(
        _mod_kernel,
        out_shape=jax.ShapeDtypeStruct((depth, 8, w), F32),
        grid=(depth, w // tn),
        in_specs=[pl.BlockSpec((8, d), lambda l, j: (0, 0)),
                  pl.BlockSpec((1, d, tn), lambda l, j: (l, 0, j)),
                  pl.BlockSpec((1, 1, tn), lambda l, j: (l, 0, j))],
        out_specs=pl.BlockSpec((1, 8, tn), lambda l, j: (l, 0, j)),
        compiler_params=_params(2),
        name="modulation",
    )(cin, w_mod, b_mod.reshape(depth, 1, w))


def _proj_kernel(x_ref, mod_ref, g_ref, w_ref, cos_ref, sin_ref,
                 qna_ref, kna_ref, vna_ref, qd_ref, kd_ref, vd_ref, qs_ref, ks_ref, vs_ref, gs_ref):
    x = x_ref[0]
    mod = mod_ref[0, 0]
    y = x * lax.rsqrt(jnp.mean(x * x, axis=-1, keepdims=True) + EPS) * g_ref[...]
    h = (y * (1.0 + mod[1:2]) + mod[0:1]).astype(BF16)

    def proj(a, b):
        return _dot(h, w_ref[:, a:b])

    cos = cos_ref[...]
    sin = sin_ref[...]
    lane = lax.broadcasted_iota(I32, cos.shape, 1)
    first_half = (lane % (2 * ROPE_PAIRS)) < ROPE_PAIRS

    def rope(t):
        outs = []
        for j in range(t.shape[1] // LANES):
            c = t[:, j * LANES:(j + 1) * LANES]
            r = jnp.where(first_half, pltpu.roll(c, LANES - ROPE_PAIRS, 1), pltpu.roll(c, ROPE_PAIRS, 1))
            outs.append(c * cos + r * sin)
        return outs[0] if len(outs) == 1 else jnp.concatenate(outs, axis=1)

    scale = HEAD_DIM ** -0.5
    qna_ref[0] = (proj(C_QNA, C_QD) * scale).astype(BF16)
    qd_ref[0] = (rope(proj(C_QD, C_QS)) * (scale * LOG2E)).astype(BF16)
    qs_ref[0] = (rope(proj(C_QS, C_KNA)) * scale).astype(BF16)
    kna_ref[0] = proj(C_KNA, C_KD).astype(BF16)
    kd_ref[0] = rope(proj(C_KD, C_KS)).astype(BF16)
    ks_ref[0] = rope(proj(C_KS, C_VNA)).astype(BF16)
    vna_ref[0] = proj(C_VNA, C_VD).astype(BF16)
    vd = proj(C_VD, C_VS).astype(BF16)
    ones = jnp.ones((vd.shape[0], DIFF_V), BF16)
    vd_ref[0] = jnp.concatenate(
        [blk for hd in range(DIFF_HEADS) for blk in (vd[:, hd * DIFF_V:(hd + 1) * DIFF_V], ones)], axis=1)
    vs_ref[0] = proj(C_VS, C_GATE).astype(BF16)
    gates = proj(C_GATE, C_END)
    gs_ref[0] = (1.0 / (1.0 + jnp.exp(-gates))).astype(BF16)


def _project(xu, modtab, g, w_in, cos_u, sin_u):
    b, m, d = xu.shape
    nt = m // TILE
    widths = (256, 256, 256, 512, 512, 2 * DIFF_HEADS * DIFF_V, 256, 128, 128, C_END - C_GATE)
    row = lambda w: pl.BlockSpec((1, TILE, w), lambda bb, i: (bb, i, 0))
    return pl.pallas_call(
        _proj_kernel,
        out_shape=[jax.ShapeDtypeStruct((b, m, w), BF16) for w in widths],
        grid=(b, nt),
        in_specs=[row(d),
                  pl.BlockSpec((1, 1, N_MOD, d), lambda bb, i: (bb, jnp.minimum(i, 1), 0, 0)),
                  pl.BlockSpec((1, d), lambda bb, i: (0, 0)),
                  pl.BlockSpec((d, C_END), lambda bb, i: (0, 0)),
                  pl.BlockSpec((TILE, LANES), lambda bb, i: (i, 0)),
                  pl.BlockSpec((TILE, LANES), lambda bb, i: (i, 0))],
        out_specs=[row(w) for w in widths],
        compiler_params=_params(2),
        name="norm_project_rope",
    )(xu, modtab, g, w_in, cos_u, sin_u)


def _na_kernel(q_ref, kc_ref, kp_ref, kcur_ref, kn_ref, vc_ref, vp_ref, vcur_ref, vn_ref, bias_ref, o_ref):
    q = q_ref[0]
    kc, vc = kc_ref[0], vc_ref[0]
    kl = jnp.concatenate([kp_ref[0], kcur_ref[0], kn_ref[0]], axis=0)
    vl = jnp.concatenate([vp_ref[0], vcur_ref[0], vn_ref[0]], axis=0)
    outs = []
    for hd in range(NA_HEADS):
        sl = slice(hd * HEAD_DIM, (hd + 1) * HEAD_DIM)
        qh = q[:, sl]
        s_c = _nt_dot(qh, kc[:, sl])
        s_l = _nt_dot(qh, kl[:, sl]) + bias_ref[0, hd]
        mx = jnp.maximum(jnp.max(s_c, axis=1, keepdims=True), jnp.max(s_l, axis=1, keepdims=True))
        p_c = jnp.exp(s_c - mx)
        p_l = jnp.exp(s_l - mx)
        den = jnp.sum(p_c, axis=1, keepdims=True) + jnp.sum(p_l, axis=1, keepdims=True)
        o = _dot(p_c.astype(BF16), vc[:, sl]) + _dot(p_l.astype(BF16), vl[:, sl])
        outs.append(o / den)
    o_ref[0] = jnp.concatenate(outs, axis=1).astype(BF16)


def _na_bias_table(rpb):
    tr, nk = NA_TILE_ROWS, 3 * NA_TILE_ROWS
    qc = np.arange(GRID_W)[:, None]
    kc = np.arange(GRID_W)[None, :]
    cstart = np.clip(qc - NA_WIN_COLS // 2, 0, GRID_W - NA_WIN_COLS)
    col_ok = (kc >= cstart) & (kc < cstart + NA_WIN_COLS)
    dc = np.clip(kc - qc, -(NA_WIN_COLS - 1), NA_WIN_COLS - 1) + NA_WIN_COLS - 1
    onehot = jnp.asarray(np.arange(2 * NA_WIN_COLS - 1)[:, None, None] == dc[None], F32)
    cols = jnp.einsum("hrc,cqk->hrqk", rpb.astype(F32), onehot, precision=HIGHEST)
    qr = np.arange(tr)[:, None]
    krow = np.arange(nk)[None, :] - tr
    dr = np.clip(krow - qr, -(NA_WIN_ROWS - 1), NA_WIN_ROWS - 1) + NA_WIN_ROWS - 1
    starts = (0 * qr, qr - NA_WIN_ROWS // 2, 0 * qr + tr - NA_WIN_ROWS)
    row_ok = np.stack([(krow >= st) & (krow < st + NA_WIN_ROWS) for st in starts])
    t = jnp.take(cols, jnp.asarray(dr.reshape(-1)), axis=1)
    t = t.reshape(NA_HEADS, tr, nk, GRID_W, GRID_W).transpose(0, 1, 3, 2, 4)
    ok = row_ok[:, None, :, None, :, None] & col_ok[None, None, None, :, None, :]
    t = jnp.where(ok, t[None], NEG).reshape(3, NA_HEADS, TILE, 3 * TILE)
    return jnp.concatenate([jnp.full_like(t[:1], NEG), t], axis=0)


def _na_attention(q, k, v, bias):
    b, m, w = q.shape
    nt = m // TILE
    nb = nt - 1

    def lat(off):
        return lambda bb, i: (bb, jnp.clip(i - 1 + off, 0, nb - 1) + 1, 0)

    def kind(bb, i):
        return (jnp.where(i == 0, 0, jnp.where(i == 1, 1, jnp.where(i == nb, 3, 2))), 0, 0, 0)

    blk = lambda f: pl.BlockSpec((1, TILE, w), f)
    kv_specs = [blk(lambda bb, i: (bb, 0, 0)), blk(lat(-1)), blk(lat(0)), blk(lat(1))]
    return pl.pallas_call(
        _na_kernel,
        out_shape=jax.ShapeDtypeStruct((b, m, w), BF16),
        grid=(b, nt),
        in_specs=[blk(lambda bb, i: (bb, i, 0))] + kv_specs + kv_specs
                 + [pl.BlockSpec((1, NA_HEADS, TILE, 3 * TILE), kind)],
        out_specs=blk(lambda bb, i: (bb, i, 0)),
        compiler_params=_params(2),
        name="neighbourhood_attention",
    )(q, k, k, k, k, v, v, v, v, bias)


def _swa_kernel(sink_ref, q_ref, kc_ref, kp_ref, kcur_ref, kn_ref, vc_ref, vp_ref, vcur_ref, vn_ref, o_ref,
                *, n_lat):
    i = pl.program_id(1)
    q = q_ref[0]
    kc, vc = kc_ref[0], vc_ref[0]
    kl = jnp.concatenate([kp_ref[0], kcur_ref[0], kn_ref[0]], axis=0)
    vl = jnp.concatenate([vp_ref[0], vcur_ref[0], vn_ref[0]], axis=0)
    group = SWA_HEADS // SWA_KV_HEADS
    rows = group * TILE
    qpos = (i - 1) * TILE + lax.broadcasted_iota(I32, (rows, 3 * TILE), 0) % TILE
    kpos = (i - 2) * TILE + lax.broadcasted_iota(I32, (rows, 3 * TILE), 1)
    ok = (kpos >= 0) & (kpos < n_lat) & (jnp.abs(qpos - kpos) <= SWA_WINDOW) & (i > 0)
    rid = lax.broadcasted_iota(I32, (rows, 1), 0)
    outs = []
    for g in range(SWA_KV_HEADS):
        ksl = slice(g * HEAD_DIM, (g + 1) * HEAD_DIM)
        qg = jnp.concatenate([q[:, (g * group + j) * HEAD_DIM:(g * group + j + 1) * HEAD_DIM]
                              for j in range(group)], axis=0)
        sink = jnp.zeros((rows, 1), F32)
        for j in range(group):
            sink = jnp.where(rid // TILE == j, sink_ref[g * group + j], sink)
        s_c = _nt_dot(qg, kc[:, ksl])
        s_l = jnp.where(ok, _nt_dot(qg, kl[:, ksl]), NEG)
        mx = jnp.maximum(jnp.maximum(jnp.max(s_c, axis=1, keepdims=True), jnp.max(s_l, axis=1, keepdims=True)), sink)
        p_c = jnp.exp(s_c - mx)
        p_l = jnp.exp(s_l - mx)
        den = jnp.sum(p_c, axis=1, keepdims=True) + jnp.sum(p_l, axis=1, keepdims=True) + jnp.exp(sink - mx)
        o = (_dot(p_c.astype(BF16), vc[:, ksl]) + _dot(p_l.astype(BF16), vl[:, ksl])) / den
        outs.extend(o[j * TILE:(j + 1) * TILE] for j in range(group))
    o_ref[0] = jnp.concatenate(outs, axis=1).astype(BF16)


def _swa_attention(q, k, v, sink):
    b, m, wq = q.shape
    wk = k.shape[2]
    nt = m // TILE
    nb = nt - 1

    def lat(off):
        return lambda bb, i: (bb, jnp.clip(i - 1 + off, 0, nb - 1) + 1, 0)

    blk = lambda w, f: pl.BlockSpec((1, TILE, w), f)
    kv_specs = [blk(wk, lambda bb, i: (bb, 0, 0)), blk(wk, lat(-1)), blk(wk, lat(0)), blk(wk, lat(1))]
    return pl.pallas_call(
        functools.partial(_swa_kernel, n_lat=m - CTX),
        out_shape=jax.ShapeDtypeStruct((b, m, wq), BF16),
        grid=(b, nt),
        in_specs=[pl.BlockSpec(memory_space=pltpu.SMEM), blk(wq, lambda bb, i: (bb, i, 0))] + kv_specs + kv_specs,
        out_specs=blk(wq, lambda bb, i: (bb, i, 0)),
        compiler_params=_params(2),
        name="windowed_attention",
    )(sink, q, k, k, k, k, v, v, v, v)


def _diff_kernel(lam_ref, q_ref, k_ref, v_ref, g_ref, o_ref, acc_ref, m_ref, s_ref, cmax_ref,
                 *, n_chunks, kblk, lam_init):
    i = pl.program_id(2)
    q = q_ref[0]
    lane = lax.broadcasted_iota(I32, q.shape, 1)
    zero = jnp.zeros_like(q)
    qq = jnp.concatenate([jnp.where(lane < HEAD_DIM, q, zero), jnp.where(lane >= HEAD_DIM, q, zero)], axis=0)

    def chunk(c):
        return pl.ds(pl.multiple_of(c * kblk, LANES), kblk)

    def scores(c, slot):
        s = _nt_dot(qq, k_ref[0, chunk(c), :])
        s_ref[slot] = s
        cmax_ref[slot] = jnp.max(s, axis=1, keepdims=True)

    def accumulate(s, cmax, v):
        m_prev = m_ref[...]
        m_new = jnp.maximum(m_prev, cmax)
        p = jnp.exp2(s - m_new).astype(BF16)
        acc_ref[...] = jnp.exp2(m_prev - m_new) * acc_ref[...] + _dot(p, v)
        m_ref[...] = m_new

    m_ref[...] = jnp.full(m_ref.shape, NEG, F32)
    acc_ref[...] = jnp.zeros(acc_ref.shape, F32)

    @pl.when(i == 0)
    def _():
        s = _nt_dot(qq, k_ref[0, 0:CTX, :])
        accumulate(s, jnp.max(s, axis=1, keepdims=True), v_ref[0, 0:CTX, :])

    @pl.when(i > 0)
    def _():
        scores(0, 0)

        def pair(c, last):
            for cur in (0, 1):
                if not (last and cur == 1):
                    scores(c + cur + 1, 1 - cur)
                accumulate(s_ref[cur], cmax_ref[cur], v_ref[0, chunk(c + cur), :])

        def body(c2, carry):
            pair(2 * c2, False)
            return carry

        lax.fori_loop(0, n_chunks // 2 - 1, body, 0)
        pair(n_chunks - 2, True)

    acc = acc_ref[...]
    o = acc[:, :DIFF_V] / acc[:, DIFF_V:]
    d = o[:TILE] - lam_ref[0] * o[TILE:]
    y = d * lax.rsqrt(jnp.mean(d * d, axis=-1, keepdims=True) + EPS) * g_ref[...]
    o_ref[0] = (y * (1.0 - lam_init)).astype(BF16)


def _diff_chunking(m, max_chunk=2048):
    for n_chunks in range(2, m // LANES + 1, 2):
        if m % (n_chunks * LANES) == 0 and m // n_chunks <= max_chunk:
            return n_chunks, m // n_chunks
    raise ValueError(f"no chunking for {m} keys")


def _diff_attention(q, k, v, lam, g, lam_init):
    b, m, _ = q.shape
    nt = m // TILE
    n_chunks, kblk = _diff_chunking(m)
    return pl.pallas_call(
        functools.partial(_diff_kernel, n_chunks=n_chunks, kblk=kblk, lam_init=lam_init),
        out_shape=jax.ShapeDtypeStruct((b, m, DIFF_HEADS * DIFF_V), BF16),
        grid=(b, DIFF_HEADS, nt),
        in_specs=[pl.BlockSpec(memory_space=pltpu.SMEM),
                  pl.BlockSpec((1, TILE, 2 * HEAD_DIM), lambda bb, hh, i: (bb, i, hh)),
                  pl.BlockSpec((1, m, 2 * HEAD_DIM), lambda bb, hh, i: (bb, 0, hh)),
                  pl.BlockSpec((1, m, 2 * DIFF_V), lambda bb, hh, i: (bb, 0, hh)),
                  pl.BlockSpec((1, DIFF_V), lambda bb, hh, i: (0, 0))],
        out_specs=pl.BlockSpec((1, TILE, DIFF_V), lambda bb, hh, i: (bb, i, hh)),
        scratch_shapes=[pltpu.VMEM((2 * TILE, 2 * DIFF_V), F32), pltpu.VMEM((2 * TILE, 1), F32),
                        pltpu.VMEM((2, 2 * TILE, kblk), F32), pltpu.VMEM((2, 2 * TILE, 1), F32)],
        compiler_params=_params(3),
        name="differential_attention",
    )(lam, q, k, v, g)


def _merge_kernel(ona_ref, od_ref, osw_ref, gs_ref, x_ref, mod_ref, wna_ref, wd_ref, wsw_ref, wo_ref,
                  g2_ref, wr_ref, wrt_ref, xo_ref, h2_ref, aff_ref, afft_ref):
    d = x_ref.shape[2]
    gs = gs_ref[0]
    y = (gs[:, :d].astype(F32) * _dot(ona_ref[0], wna_ref[...])
         + gs[:, d:2 * d].astype(F32) * _dot(od_ref[0], wd_ref[...])
         + gs[:, 2 * d:].astype(F32) * _dot(osw_ref[0], wsw_ref[...]))
    mod = mod_ref[0, 0]
    xn = x_ref[0] + mod[2:3] * _dot(y.astype(BF16), wo_ref[...])
    xo_ref[0] = xn
    r = xn * lax.rsqrt(jnp.mean(xn * xn, axis=-1, keepdims=True) + EPS) * g2_ref[...]
    h2 = r * (1.0 + mod[4:5]) + mod[3:4]
    h2_ref[0] = h2.astype(BF16)
    logits = jnp.dot(h2, wr_ref[...], preferred_element_type=F32, precision=HIGHEST)
    e = jnp.exp(logits - jnp.max(logits, axis=1, keepdims=True))
    aff_ref[0] = e / jnp.sum(e, axis=1, keepdims=True)
    lt = lax.dot_general(wrt_ref[...], h2, (((1,), (1,)), ((), ())),
                         preferred_element_type=F32, precision=HIGHEST)
    et = jnp.exp(lt - jnp.max(lt, axis=0, keepdims=True))
    afft_ref[0] = et / jnp.sum(et, axis=0, keepdims=True)


def _merge(o_na, o_d, o_sw, gs, xu, modtab, w_na, w_d, w_sw, w_o, g2, w_r):
    b, m, d = xu.shape
    nt = m // TILE
    ne = w_r.shape[1]
    row = lambda w: pl.BlockSpec((1, TILE, w), lambda bb, i: (bb, i, 0))
    full = lambda a: pl.BlockSpec(a.shape, lambda bb, i: (0,) * a.ndim)
    w_rt = w_r.T
    return pl.pallas_call(
        _merge_kernel,
        out_shape=[jax.ShapeDtypeStruct((b, m, d), F32), jax.ShapeDtypeStruct((b, m, d), BF16),
                   jax.ShapeDtypeStruct((b, m, ne), F32), jax.ShapeDtypeStruct((b, ne, m), F32)],
        grid=(b, nt),
        in_specs=[row(o_na.shape[2]), row(o_d.shape[2]), row(o_sw.shape[2]), row(gs.shape[2]), row(d),
                  pl.BlockSpec((1, 1, N_MOD, d), lambda bb, i: (bb, jnp.minimum(i, 1), 0, 0)),
                  full(w_na), full(w_d), full(w_sw), full(w_o), full(g2), full(w_r), full(w_rt)],
        out_specs=[row(d), row(d), row(ne), pl.BlockSpec((1, ne, TILE), lambda bb, i: (bb, 0, i))],
        compiler_params=_params(2),
        name="merge_residual_router",
    )(o_na, o_d, o_sw, gs, xu, modtab, w_na, w_d, w_sw, w_o, g2, w_r, w_rt)


def _route_kernel(a_ref, gpos_ref, gfull_ref, *, n_latent_sets, cap_lat, cap_ctx):
    ne, c, w = a_ref.shape[1:]
    a = a_ref[0]
    bits = lax.bitcast_convert_type(a, I32)
    cap = jnp.where(pl.program_id(0) < n_latent_sets, cap_lat, cap_ctx).astype(F32)

    def count(mask):
        return jnp.sum(jnp.sum(mask.astype(F32), axis=2, keepdims=True), axis=1, keepdims=True)

    def search(it, thr):
        cand = thr | jnp.left_shift(jnp.int32(1), 30 - it)
        return jnp.where(count(bits >= cand) >= cap, cand, thr)

    thr = lax.fori_loop(0, 31, search, jnp.zeros((ne, 1, 1), I32))
    gt = bits > thr
    eq = bits == thr
    need = cap - count(gt)

    upper = (lax.broadcasted_iota(I32, (w, w), 0) <= lax.broadcasted_iota(I32, (w, w), 1)).astype(BF16)
    lower = (lax.broadcasted_iota(I32, (c, c), 1) < lax.broadcasted_iota(I32, (c, c), 0)).astype(BF16)

    def exclusive_cumsum(mask):
        x = mask.astype(F32).reshape(ne * c, w)
        within = _dot(x.astype(BF16), upper)
        tot = jnp.broadcast_to(within[:, w - 1:w], (ne * c, w)).astype(BF16)
        before = jnp.concatenate([_dot(lower, tot[e * c:(e + 1) * c]) for e in range(ne)], axis=0)
        return (before + within - x).reshape(ne, c, w)

    sel = gt | (eq & (exclusive_cumsum(eq) < need))
    g = exclusive_cumsum(sel).astype(I32)
    gfull_ref[0] = g
    gpos_ref[0] = jnp.where(sel, g, -1)


def _route(aff_sets, n_latent_sets, cap_lat, cap_ctx):
    s, ne, n = aff_sets.shape
    c = n // LANES
    a4 = aff_sets.reshape(s, ne, c, LANES)
    blk = pl.BlockSpec((1, ne, c, LANES), lambda i: (i, 0, 0, 0))
    gpos, gfull = pl.pallas_call(
        functools.partial(_route_kernel, n_latent_sets=n_latent_sets, cap_lat=cap_lat, cap_ctx=cap_ctx),
        out_shape=[jax.ShapeDtypeStruct(a4.shape, I32)] * 2,
        grid=(s,),
        in_specs=[blk],
        out_specs=[blk, blk],
        compiler_params=_params(1),
        name="expert_choice_select",
    )(a4)
    return gpos.reshape(s, ne, n), gfull.reshape(s, ne, n)


def _window(cap):
    w = min(TILE + BF16_ROWS, cap)
    main = min(WINDOW_MAIN_ROWS, w)
    return main, w - main, cap - w


def _windows(gfull, cap):
    b, ne, _ = gfull.shape
    main, _, max_start = _window(cap)
    seg_start = gfull[:, :, ::TILE]
    seg_end = jnp.concatenate([gfull[:, :, TILE::TILE], jnp.full((b, ne, 1), cap, I32)], axis=2)
    astart = jnp.minimum(seg_start // BF16_ROWS * BF16_ROWS, max_start)
    return astart.reshape(-1), (seg_end > astart + main).astype(I32).reshape(-1)


def _gather_kernel(a_ref, t_ref, g_ref, h_ref, o_ref, *, ne, nblk, eg, main, tail):
    b, egi, j = pl.program_id(0), pl.program_id(1), pl.program_id(2)

    @pl.when(j == 0)
    def _():
        o_ref[...] = jnp.zeros(o_ref.shape, o_ref.dtype)

    hb = h_ref[0]
    t = hb.shape[0]

    def place(k, pos, first, rows):
        onehot = jnp.where(lax.broadcasted_iota(I32, (rows, t), 0) == pos - first, 1.0, 0.0).astype(BF16)
        picked = _dot(onehot, hb).astype(BF16)
        sl = (0, k, pl.ds(pl.multiple_of(a_ref[idx(k)] + first, BF16_ROWS), rows), slice(None))
        o_ref[sl] = o_ref[sl] + picked

    def idx(k):
        return (b * ne + egi * eg + k) * nblk + j

    for k in range(eg):
        pos = g_ref[0, k] - a_ref[idx(k)]
        place(k, pos, 0, main)
        if tail:
            @pl.when(t_ref[idx(k)] != 0)
            def _(k=k, pos=pos):
                place(k, pos, main, tail)


def _gather(h2u, gpos, astart, need_tail, cap, tile_off, eg):
    b, ne, n = gpos.shape
    d = h2u.shape[2]
    nblk = n // TILE
    main, tail, _ = _window(cap)
    return pl.pallas_call(
        functools.partial(_gather_kernel, ne=ne, nblk=nblk, eg=eg, main=main, tail=tail),
        out_shape=jax.ShapeDtypeStruct((b, ne, cap, d), BF16),
        grid_spec=pltpu.PrefetchScalarGridSpec(
            num_scalar_prefetch=2,
            grid=(b, ne // eg, nblk),
            in_specs=[pl.BlockSpec((1, eg, 1, TILE), lambda bb, e, j, a, nt: (bb, e, 0, j)),
                      pl.BlockSpec((1, TILE, d), lambda bb, e, j, a, nt: (bb, tile_off + j, 0))],
            out_specs=pl.BlockSpec((1, eg, cap, d), lambda bb, e, j, a, nt: (bb, e, 0, 0))),
        compiler_params=_params(3),
        name="expert_gather",
    )(astart, need_tail, gpos.reshape(b, ne, 1, n), h2u)


def _ffn_kernel(x_ref, wg_ref, wu_ref, wd_ref, o_ref, wg_s, wu_s, wd_s):
    @pl.when((pl.program_id(1) == 0) & (pl.program_id(2) == 0))
    def _():
        wg_s[...] = wg_ref[0, 0].astype(BF16)
        wu_s[...] = wu_ref[0, 0].astype(BF16)
        wd_s[...] = wd_ref[0, 0].astype(BF16)

    x = x_ref[0, 0]
    gate = _dot(x, wg_s[...])
    up = _dot(x, wu_s[...])
    hid = (gate / (1.0 + jnp.exp(-gate)) * up).astype(BF16)
    o_ref[0, 0] = _dot(hid, wd_s[...]).astype(BF16)


def _expert_ffn(xs, w_gate, w_up, w_down, layer):
    b, ne, cap, d = xs.shape
    tr = min(cap, 512)
    wspec = lambda w: pl.BlockSpec((1, 1) + w.shape[2:], lambda e, bb, r: (layer, e, 0, 0))
    xspec = pl.BlockSpec((1, 1, tr, d), lambda e, bb, r: (bb, e, r, 0))
    return pl.pallas_call(
        _ffn_kernel,
        out_shape=jax.ShapeDtypeStruct(xs.shape, BF16),
        grid=(ne, b, cap // tr),
        in_specs=[xspec, wspec(w_gate), wspec(w_up), wspec(w_down)],
        out_specs=xspec,
        scratch_shapes=[pltpu.VMEM(w.shape[2:], BF16) for w in (w_gate, w_up, w_down)],
        compiler_params=_params(3),
        name="expert_swiglu",
    )(xs, w_gate, w_up, w_down)


def _combine_kernel(a_ref, t_ref, x_ref, aff_ref, gt_ref, mod_ref, fg_ref, ye_ref, o_ref,
                    win_ref, tail_ref, tacc_ref, sem_ref, tsem_ref, *, ne, nblk, main, tail, final):
    b, j = pl.program_id(0), pl.program_id(1)

    def start_of(e):
        return pl.multiple_of(a_ref[(b * ne + e) * nblk + j], BF16_ROWS)

    def window_copy(e):
        return pltpu.make_async_copy(ye_ref.at[b, e, pl.ds(start_of(e), main)], win_ref.at[e], sem_ref.at[e])

    for e in range(ne):
        window_copy(e).start()
    aff = aff_ref[0]
    gt = gt_ref[0]
    t = aff.shape[0]
    cols = lax.broadcasted_iota(I32, (t, main), 1)
    acc = jnp.zeros(o_ref.shape[1:], F32)
    if tail:
        tacc_ref[...] = jnp.zeros(tacc_ref.shape, F32)
    for e in range(ne):
        window_copy(e).wait()
        pos = gt[:, e:e + 1] - start_of(e)
        onehot = jnp.where(cols == pos, 1.0, 0.0).astype(BF16)
        acc = acc + aff[:, e:e + 1] * _dot(onehot, win_ref[e])
        if tail:
            @pl.when(t_ref[(b * ne + e) * nblk + j] != 0)
            def _(e=e, pos=pos):
                cp = pltpu.make_async_copy(ye_ref.at[b, e, pl.ds(start_of(e) + main, tail)], tail_ref, tsem_ref.at[0])
                cp.start()
                cp.wait()
                tcols = lax.broadcasted_iota(I32, (t, tail), 1)
                oh = jnp.where(tcols == pos - main, 1.0, 0.0).astype(BF16)
                tacc_ref[...] += aff[:, e:e + 1] * _dot(oh, tail_ref[...])
    if tail:
        acc = acc + tacc_ref[...]
    x = x_ref[0] + mod_ref[0, 0][5:6] * acc
    if final:
        x = x * lax.rsqrt(jnp.mean(x * x, axis=-1, keepdims=True) + EPS) * fg_ref[...]
    o_ref[0] = x


def _combine(xu, affu, gpos_t, astart, need_tail, ye, modtab, final_g, tile_off, kind, final):
    b, m, d = xu.shape
    _, n, ne = gpos_t.shape
    cap = ye.shape[2]
    nblk = n // TILE
    main, tail, _ = _window(cap)
    row =lambda w: pl.BlockSpec((1, TILE, w), lambda bb, j, a, nt: (bb, tile_off + j, 0))
    out_spec = pl.BlockSpec((1, TILE, d), (lambda bb, j, a, nt: (bb, j, 0)) if final
                            else (lambda bb, j, a, nt: (bb, tile_off + j, 0)))
    return pl.pallas_call(
        functools.partial(_combine_kernel, ne=ne, nblk=nblk, main=main, tail=tail, final=final),
        out_shape=jax.ShapeDtypeStruct((b, n, d) if final else xu.shape, F32),
        grid_spec=pltpu.PrefetchScalarGridSpec(
            num_scalar_prefetch=2,
            grid=(b, nblk),
            in_specs=[row(d), row(ne),
                      pl.BlockSpec((1, TILE, ne), lambda bb, j, a, nt: (bb, j, 0)),
                      pl.BlockSpec((1, 1, N_MOD, d), lambda bb, j, a, nt: (bb, kind, 0, 0)),
                      pl.BlockSpec((1, d), lambda bb, j, a, nt: (0, 0)),
                      pl.BlockSpec(memory_space=pl.ANY)],
            out_specs=out_spec,
            scratch_shapes=[pltpu.VMEM((ne, main, d), BF16), pltpu.VMEM((max(tail, BF16_ROWS), d), BF16),
                            pltpu.VMEM((TILE, d), F32), pltpu.SemaphoreType.DMA((ne,)),
                            pltpu.SemaphoreType.DMA((1,))]),
        input_output_aliases={} if final else {2: 0},
        compiler_params=_params(2),
        name="expert_combine",
    )(astart, need_tail, xu, affu, gpos_t, modtab, final_g, ye)


def _rope_tables(n):
    t = np.arange(n)
    row = (t // GRID_W).astype(np.float32)[:, None]
    col = (t % GRID_W).astype(np.float32)[:, None]
    inv = (ROPE_BASE ** (-np.arange(ROPE_PAIRS, dtype=np.float32) / ROPE_PAIRS)).astype(np.float32)
    ang = np.concatenate([row * inv, row * inv, col * inv, col * inv], axis=-1)
    cos, sin = np.cos(ang), np.sin(ang)
    half = (np.arange(HEAD_DIM) % (2 * ROPE_PAIRS)) < ROPE_PAIRS
    sin = np.where(half[None], -sin, sin)
    cos = np.concatenate([np.ones((CTX, HEAD_DIM)), cos], axis=0)
    sin = np.concatenate([np.zeros((CTX, HEAD_DIM)), sin], axis=0)
    tile2 = lambda a: jnp.asarray(np.concatenate([a, a], axis=1), F32)
    return tile2(cos), tile2(sin)


def _moe(xu, h2u, affu, afft, modtab, weights, layer, final_g, final):
    b, m, d = xu.shape
    n = m - CTX
    w_gate, w_up, w_down = weights
    cap_lat = max(1, EC_CAPACITY * n // N_EXPERTS)
    cap_ctx = max(1, EC_CAPACITY * CTX // N_EXPERTS)
    sets = [afft[:, :, CTX:]]
    if not final:
        sets.append(jnp.concatenate([afft[:, :, :CTX], jnp.full((b, N_EXPERTS, n - CTX), -1.0, F32)], axis=2))
    gpos, gfull = _route(jnp.concatenate(sets, axis=0), b, cap_lat, cap_ctx)

    def run(idx, n_set, cap, tile_off, kind, x_in, fin):
        gp = gpos[idx * b:(idx + 1) * b, :, :n_set]
        gf = gfull[idx * b:(idx + 1) * b, :, :n_set]
        astart, need_tail = _windows(gf, cap)
        xs = _gather(h2u, gp, astart, need_tail, cap, tile_off, eg=4)
        ye = _expert_ffn(xs, w_gate, w_up, w_down, layer)
        return _combine(x_in, affu, jnp.swapaxes(gp, 1, 2), astart, need_tail, ye, modtab, final_g, tile_off, kind, fin)

    if final:
        return run(0, n, cap_lat, 1, 1, xu, True)
    xu = run(0, n, cap_lat, 1, 1, xu, False)
    return run(1, CTX, cap_ctx, 0, 0, xu, False)


def kernel(x, c, ctx, c_ctx, w_mod, b_mod, norm1_g, w_in, na_rpb, diff_lambda, diff_subln_g, swa_sink,
           w_branch_na, w_branch_diff, w_branch_swa, w_out, norm2_g, w_router, w_expert_gate, w_expert_up,
           w_expert_down, final_g):
    b, n, d = x.shape
    depth = w_mod.shape[0]
    assert ctx.shape[1] == CTX and d == D_MODEL and n % (2 * TILE) == 0 and n // TILE >= 3

    cin = jnp.concatenate([c, c_ctx[None], jnp.zeros((8 - b - 1, d), F32)], axis=0)
    mod_all = _modulation(cin, w_mod, b_mod)
    cos_u, sin_u = _rope_tables(n)
    xu = jnp.concatenate([ctx, x], axis=1)
    final_g2 = final_g.reshape(1, d)

    out = None
    for l in range(depth):
        final = l == depth - 1
        lam_init = 0.8 - 0.6 * math.exp(-0.3 * l)
        mod_l = mod_all[l].reshape(8, N_MOD, d)
        modtab = jnp.stack([jnp.broadcast_to(mod_l[b], (b, N_MOD, d)), mod_l[:b]], axis=1)
        lam_p = diff_lambda[l].astype(F32)
        lam = (jnp.exp(jnp.sum(lam_p[0] * lam_p[1])) - jnp.exp(jnp.sum(lam_p[2] * lam_p[3])) + lam_init).reshape(1)

        (q_na, k_na, v_na, q_d, k_d, v_d, q_s, k_s, v_s, gs) = _project(
            xu, modtab, norm1_g[l].reshape(1, d), w_in[l].astype(BF16), cos_u, sin_u)
        o_na = _na_attention(q_na, k_na, v_na, _na_bias_table(na_rpb[l]))
        o_d = _diff_attention(q_d, k_d, v_d, lam, diff_subln_g[l].reshape(1, DIFF_V), lam_init)
        o_s = _swa_attention(q_s, k_s, v_s, swa_sink[l].astype(F32))
        xu, h2u, affu, afft = _merge(
            o_na, o_d, o_s, gs, xu, modtab, w_branch_na[l].astype(BF16), w_branch_diff[l].astype(BF16),
            w_branch_swa[l].astype(BF16), w_out[l].astype(BF16), norm2_g[l].reshape(1, d), w_router[l])
        weights = (w_expert_gate, w_expert_up, w_expert_down)
        res = _moe(xu, h2u, affu, afft, modtab, weights, l, final_g2, final)
        if final:
            out = res
        else:
            xu = res
    return out
```

```python
import functools
import math

import numpy as np
import jax
import jax.numpy as jnp
from jax import lax
from jax.experimental import pallas as pl
from jax.experimental.pallas import tpu as pltpu

F32 = jnp.float32
BF16 = jnp.bfloat16
I32 = jnp.int32
HIGHEST = lax.Precision.HIGHEST

D_MODEL = 1024
CTX = 256
TILE = 256
GRID_W = 64
HEAD_DIM = 64
ROPE_PAIRS = HEAD_DIM // 4
ROPE_BASE = 10000.0
EPS = 1e-6
N_MOD = 6
NA_HEADS = 4
NA_WIN_ROWS = 8
NA_WIN_COLS = 16
NA_TILE_ROWS = TILE // GRID_W
DIFF_HEADS = 4
DIFF_V = 2 * HEAD_DIM
SWA_HEADS = 4
SWA_KV_HEADS = 2
SWA_WINDOW = 128
N_EXPERTS = 16
EC_CAPACITY = 2
NEG = -1e30
LANES = 128
LOG2E = math.log2(math.e)
DIFF_MAX_RISE = 32.0
BF16_ROWS = 16
WINDOW_MAIN_ROWS = 128

C_QNA, C_QD, C_QS = 0, 256, 768
C_KNA, C_KD, C_KS = 1024, 1280, 1792
C_VNA, C_VD, C_VS = 1920, 2176, 2688
C_GATE, C_END = 2816, 5888

VMEM_LIMIT = 56 * 1024 * 1024


def _params(n_axes, vmem=VMEM_LIMIT):
    return pltpu.CompilerParams(dimension_semantics=("arbitrary",) * n_axes, vmem_limit_bytes=vmem)


def _nt_dot(a, b):
    return lax.dot_general(a, b, (((1,), (1,)), ((), ())), preferred_element_type=F32)


def _dot(a, b):
    return jnp.dot(a, b, preferred_element_type=F32)


def _mod_kernel(c_ref, w_ref, b_ref, o_ref):
    c = c_ref[...]
    s = c / (1.0 + jnp.exp(-c))
    o_ref[0] = jnp.dot(s, w_ref[0], preferred_element_type=F32, precision=HIGHEST) + b_ref[0]


def _modulation(cin, w_mod, b_mod):
    depth, d, w = w_mod.shape
    tn = 1024
    return pl.pallas_call(
        _mod_kernel,
        out_shape=jax.ShapeDtypeStruct((depth, 8, w), F32),
        grid=(depth, w // tn),
        in_specs=[pl.BlockSpec((8, d), lambda l, j: (0, 0)),
                  pl.BlockSpec((1, d, tn), lambda l, j: (l, 0, j)),
                  pl.BlockSpec((1, 1, tn), lambda l, j: (l, 0, j))],
        out_specs=pl.BlockSpec((1, 8, tn), lambda l, j: (l, 0, j)),
        compiler_params=_params(2),
        name="modulation",
    )(cin, w_mod, b_mod.reshape(depth, 1, w))


def _proj_kernel(x_ref, mod_ref, g_ref, w_ref, cos_ref, sin_ref,
                 qna_ref, kna_ref, vna_ref, qd_ref, kd_ref, vd_ref, qs_ref, ks_ref, vs_ref, gs_ref):
    x = x_ref[0]
    mod = mod_ref[0, 0]
    y = x * lax.rsqrt(jnp.mean(x * x, axis=-1, keepdims=True) + EPS) * g_ref[...]
    h = (y * (1.0 + mod[1:2]) + mod[0:1]).astype(BF16)

    def proj(a, b):
        return _dot(h, w_ref[:, a:b])

    cos = cos_ref[...]
    sin = sin_ref[...]
    lane = lax.broadcasted_iota(I32, cos.shape, 1)
    first_half = (lane % (2 * ROPE_PAIRS)) < ROPE_PAIRS

    def rope(t):
        outs = []
        for j in range(t.shape[1] // LANES):
            c = t[:, j * LANES:(j + 1) * LANES]
            r = jnp.where(first_half, pltpu.roll(c, LANES - ROPE_PAIRS, 1), pltpu.roll(c, ROPE_PAIRS, 1))
            outs.append(c * cos + r * sin)
        return outs[0] if len(outs) == 1 else jnp.concatenate(outs, axis=1)

    scale = HEAD_DIM ** -0.5
    qna_ref[0] = (proj(C_QNA, C_QD) * scale).astype(BF16)
    qd_ref[0] = (rope(proj(C_QD, C_QS)) * (scale * LOG2E)).astype(BF16)
    qs_ref[0] = (rope(proj(C_QS, C_KNA)) * scale).astype(BF16)
    kna_ref[0] = proj(C_KNA, C_KD).astype(BF16)
    kd_ref[0] = rope(proj(C_KD, C_KS)).astype(BF16)
    ks_ref[0] = rope(proj(C_KS, C_VNA)).astype(BF16)
    vna_ref[0] = proj(C_VNA, C_VD).astype(BF16)
    vd = proj(C_VD, C_VS).astype(BF16)
    ones = jnp.ones((vd.shape[0], DIFF_V), BF16)
    vd_ref[0] = jnp.concatenate(
        [blk for hd in range(DIFF_HEADS) for blk in (vd[:, hd * DIFF_V:(hd + 1) * DIFF_V], ones)], axis=1)
    vs_ref[0] = proj(C_VS, C_GATE).astype(BF16)
    gates = proj(C_GATE, C_END)
    gs_ref[0] = (1.0 / (1.0 + jnp.exp(-gates))).astype(BF16)


def _project(xu, modtab, g, w_in, cos_u, sin_u):
    b, m, d = xu.shape
    nt = m // TILE
    widths = (256, 256, 256, 512, 512, 2 * DIFF_HEADS * DIFF_V, 256, 128, 128, C_END - C_GATE)
    row = lambda w: pl.BlockSpec((1, TILE, w), lambda bb, i: (bb, i, 0))
    return pl.pallas_call(
        _proj_kernel,
        out_shape=[jax.ShapeDtypeStruct((b, m, w), BF16) for w in widths],
        grid=(b, nt),
        in_specs=[row(d),
                  pl.BlockSpec((1, 1, N_MOD, d), lambda bb, i: (bb, jnp.minimum(i, 1), 0, 0)),
                  pl.BlockSpec((1, d), lambda bb, i: (0, 0)),
                  pl.BlockSpec((d, C_END), lambda bb, i: (0, 0)),
                  pl.BlockSpec((TILE, LANES), lambda bb, i: (i, 0)),
                  pl.BlockSpec((TILE, LANES), lambda bb, i: (i, 0))],
        out_specs=[row(w) for w in widths],
        compiler_params=_params(2),
        name="norm_project_rope",
    )(xu, modtab, g, w_in, cos_u, sin_u)


def _na_kernel(q_ref, kc_ref, kp_ref, kcur_ref, kn_ref, vc_ref, vp_ref, vcur_ref, vn_ref, bias_ref, o_ref):
    q = q_ref[0]
    kc, vc = kc_ref[0], vc_ref[0]
    kl = jnp.concatenate([kp_ref[0], kcur_ref[0], kn_ref[0]], axis=0)
    vl = jnp.concatenate([vp_ref[0], vcur_ref[0], vn_ref[0]], axis=0)
    outs = []
    for hd in range(NA_HEADS):
        sl = slice(hd * HEAD_DIM, (hd + 1) * HEAD_DIM)
        qh = q[:, sl]
        s_c = _nt_dot(qh, kc[:, sl])
        s_l = _nt_dot(qh, kl[:, sl]) + bias_ref[0, hd]
        mx = jnp.maximum(jnp.max(s_c, axis=1, keepdims=True), jnp.max(s_l, axis=1, keepdims=True))
        p_c = jnp.exp(s_c - mx)
        p_l = jnp.exp(s_l - mx)
        den = jnp.sum(p_c, axis=1, keepdims=True) + jnp.sum(p_l, axis=1, keepdims=True)
        o = _dot(p_c.astype(BF16), vc[:, sl]) + _dot(p_l.astype(BF16), vl[:, sl])
        outs.append(o / den)
    o_ref[0] = jnp.concatenate(outs, axis=1).astype(BF16)


def _na_bias_table(rpb):
    tr, nk = NA_TILE_ROWS, 3 * NA_TILE_ROWS
    qc = np.arange(GRID_W)[:, None]
    kc = np.arange(GRID_W)[None, :]
    cstart = np.clip(qc - NA_WIN_COLS // 2, 0, GRID_W - NA_WIN_COLS)
    col_ok = (kc >= cstart) & (kc < cstart + NA_WIN_COLS)
    dc = np.clip(kc - qc, -(NA_WIN_COLS - 1), NA_WIN_COLS - 1) + NA_WIN_COLS - 1
    onehot = jnp.asarray(np.arange(2 * NA_WIN_COLS - 1)[:, None, None] == dc[None], F32)
    cols = jnp.einsum("hrc,cqk->hrqk", rpb.astype(F32), onehot, precision=HIGHEST)
    qr = np.arange(tr)[:, None]
    krow = np.arange(nk)[None, :] - tr
    dr = np.clip(krow - qr, -(NA_WIN_ROWS - 1), NA_WIN_ROWS - 1) + NA_WIN_ROWS - 1
    starts = (0 * qr, qr - NA_WIN_ROWS // 2, 0 * qr + tr - NA_WIN_ROWS)
    row_ok = np.stack([(krow >= st) & (krow < st + NA_WIN_ROWS) for st in starts])
    t = jnp.take(cols, jnp.asarray(dr.reshape(-1)), axis=1)
    t = t.reshape(NA_HEADS, tr, nk, GRID_W, GRID_W).transpose(0, 1, 3, 2, 4)
    ok = row_ok[:, None, :, None, :, None] & col_ok[None, None, None, :, None, :]
    t = jnp.where(ok, t[None], NEG).reshape(3, NA_HEADS, TILE, 3 * TILE)
    return jnp.concatenate([jnp.full_like(t[:1], NEG), t], axis=0)


def _na_attention(q, k, v, bias):
    b, m, w = q.shape
    nt = m // TILE
    nb = nt - 1

    def lat(off):
        return lambda bb, i: (bb, jnp.clip(i - 1 + off, 0, nb - 1) + 1, 0)

    def kind(bb, i):
        return (jnp.where(i == 0, 0, jnp.where(i == 1, 1, jnp.where(i == nb, 3, 2))), 0, 0, 0)

    blk = lambda f: pl.BlockSpec((1, TILE, w), f)
    kv_specs = [blk(lambda bb, i: (bb, 0, 0)), blk(lat(-1)), blk(lat(0)), blk(lat(1))]
    return pl.pallas_call(
        _na_kernel,
        out_shape=jax.ShapeDtypeStruct((b, m, w), BF16),
        grid=(b, nt),
        in_specs=[blk(lambda bb, i: (bb, i, 0))] + kv_specs + kv_specs
                 + [pl.BlockSpec((1, NA_HEADS, TILE, 3 * TILE), kind)],
        out_specs=blk(lambda bb, i: (bb, i, 0)),
        compiler_params=_params(2),
        name="neighbourhood_attention",
    )(q, k, k, k, k, v, v, v, v, bias)


def _swa_kernel(sink_ref, q_ref, kc_ref, kp_ref, kcur_ref, kn_ref, vc_ref, vp_ref, vcur_ref, vn_ref, o_ref,
                *, n_lat):
    i = pl.program_id(1)
    q = q_ref[0]
    kc, vc = kc_ref[0], vc_ref[0]
    kl = jnp.concatenate([kp_ref[0], kcur_ref[0], kn_ref[0]], axis=0)
    vl = jnp.concatenate([vp_ref[0], vcur_ref[0], vn_ref[0]], axis=0)
    group = SWA_HEADS // SWA_KV_HEADS
    rows = group * TILE
    qpos = (i - 1) * TILE + lax.broadcasted_iota(I32, (rows, 3 * TILE), 0) % TILE
    kpos = (i - 2) * TILE + lax.broadcasted_iota(I32, (rows, 3 * TILE), 1)
    ok = (kpos >= 0) & (kpos < n_lat) & (jnp.abs(qpos - kpos) <= SWA_WINDOW) & (i > 0)
    rid = lax.broadcasted_iota(I32, (rows, 1), 0)
    outs = []
    for g in range(SWA_KV_HEADS):
        ksl = slice(g * HEAD_DIM, (g + 1) * HEAD_DIM)
        qg = jnp.concatenate([q[:, (g * group + j) * HEAD_DIM:(g * group + j + 1) * HEAD_DIM]
                              for j in range(group)], axis=0)
        sink = jnp.zeros((rows, 1), F32)
        for j in range(group):
            sink = jnp.where(rid // TILE == j, sink_ref[g * group + j], sink)
        s_c = _nt_dot(qg, kc[:, ksl])
        s_l = jnp.where(ok, _nt_dot(qg, kl[:, ksl]), NEG)
        mx = jnp.maximum(jnp.maximum(jnp.max(s_c, axis=1, keepdims=True), jnp.max(s_l, axis=1, keepdims=True)), sink)
        p_c = jnp.exp(s_c - mx)
        p_l = jnp.exp(s_l - mx)
        den = jnp.sum(p_c, axis=1, keepdims=True) + jnp.sum(p_l, axis=1, keepdims=True) + jnp.exp(sink - mx)
        o = (_dot(p_c.astype(BF16), vc[:, ksl]) + _dot(p_l.astype(BF16), vl[:, ksl])) / den
        outs.extend(o[j * TILE:(j + 1) * TILE] for j in range(group))
    o_ref[0] = jnp.concatenate(outs, axis=1).astype(BF16)


def _swa_attention(q, k, v, sink):
    b, m, wq = q.shape
    wk = k.shape[2]
    nt = m // TILE
    nb = nt - 1

    def lat(off):
        return lambda bb, i: (bb, jnp.clip(i - 1 + off, 0, nb - 1) + 1, 0)

    blk = lambda w, f: pl.BlockSpec((1, TILE, w), f)
    kv_specs = [blk(wk, lambda bb, i: (bb, 0, 0)), blk(wk, lat(-1)), blk(wk, lat(0)), blk(wk, lat(1))]
    return pl.pallas_call(
        functools.partial(_swa_kernel, n_lat=m - CTX),
        out_shape=jax.ShapeDtypeStruct((b, m, wq), BF16),
        grid=(b, nt),
        in_specs=[pl.BlockSpec(memory_space=pltpu.SMEM), blk(wq, lambda bb, i: (bb, i, 0))] + kv_specs + kv_specs,
        out_specs=blk(wq, lambda bb, i: (bb, i, 0)),
        compiler_params=_params(2),
        name="windowed_attention",
    )(sink, q, k, k, k, k, v, v, v, v)


def _diff_kernel(lam_ref, q_ref, k_ref, v_ref, g_ref, o_ref, acc_ref, m_ref, s_ref, cmax_ref, rise_ref,
                 *, n_chunks, kblk, lam_init):
    i = pl.program_id(2)
    q = q_ref[0]
    lane = lax.broadcasted_iota(I32, q.shape, 1)
    zero = jnp.zeros_like(q)
    qq = jnp.concatenate([jnp.where(lane < HEAD_DIM, q, zero), jnp.where(lane >= HEAD_DIM, q, zero)], axis=0)

    def chunk(c):
        return pl.ds(pl.multiple_of(c * kblk, LANES), kblk)

    def scores(c, slot):
        s = _nt_dot(qq, k_ref[0, chunk(c), :])
        s_ref[slot] = s
        cmax_ref[slot] = jnp.max(s, axis=1, keepdims=True)

    def accumulate(s, cmax, v):
        m_prev = m_ref[...]
        m_new = jnp.maximum(m_prev, cmax)
        p = jnp.exp2(s - m_new).astype(BF16)
        acc_ref[...] = jnp.exp2(m_prev - m_new) * acc_ref[...] + _dot(p, v)
        m_ref[...] = m_new

    def reset():
        m_ref[...] = jnp.full(m_ref.shape, NEG, F32)
        acc_ref[...] = jnp.zeros(acc_ref.shape, F32)

    def safe_sweep():
        reset()
        scores(0, 0)

        def pair(c, last):
            for cur in (0, 1):
                if not (last and cur == 1):
                    scores(c + cur + 1, 1 - cur)
                accumulate(s_ref[cur], cmax_ref[cur], v_ref[0, chunk(c + cur), :])

        def body(c2, carry):
            pair(2 * c2, False)
            return carry

        lax.fori_loop(0, n_chunks // 2 - 1, body, 0)
        pair(n_chunks - 2, True)

    def fast_chunk(c):
        s = _nt_dot(qq, k_ref[0, chunk(c), :])
        m_prev = m_ref[...]
        p = jnp.exp2(s - m_prev).astype(BF16)
        cmax = jnp.max(s, axis=1, keepdims=True)
        m_new = jnp.maximum(m_prev, cmax)
        acc_ref[...] = (acc_ref[...] + _dot(p, v_ref[0, chunk(c), :])) * jnp.exp2(m_prev - m_new)
        m_ref[...] = m_new
        rise_ref[...] = jnp.maximum(rise_ref[...], cmax - m_prev)

    def fast_sweep():
        reset()
        s = _nt_dot(qq, k_ref[0, chunk(0), :])
        accumulate(s, jnp.max(s, axis=1, keepdims=True), v_ref[0, chunk(0), :])
        rise_ref[...] = jnp.zeros(rise_ref.shape, F32)
        for c in range(1, n_chunks):
            fast_chunk(c)

    @pl.when(i == 0)
    def _():
        reset()
        s = _nt_dot(qq, k_ref[0, 0:CTX, :])
        accumulate(s, jnp.max(s, axis=1, keepdims=True), v_ref[0, 0:CTX, :])

    @pl.when(i > 0)
    def _():
        fast_sweep()

        @pl.when(jnp.max(rise_ref[...]) > DIFF_MAX_RISE)
        def _():
            safe_sweep()

    acc = acc_ref[...]
    o = acc[:, :DIFF_V] / acc[:, DIFF_V:]
    d = o[:TILE] - lam_ref[0] * o[TILE:]
    y = d * lax.rsqrt(jnp.mean(d * d, axis=-1, keepdims=True) + EPS) * g_ref[...]
    o_ref[0] = (y * (1.0 - lam_init)).astype(BF16)


def _diff_chunking(m, max_chunk=2048):
    for n_chunks in range(2, m // LANES + 1, 2):
        if m % (n_chunks * LANES) == 0 and m // n_chunks <= max_chunk:
            return n_chunks, m // n_chunks
    raise ValueError(f"no chunking for {m} keys")


def _diff_attention(q, k, v, lam, g, lam_init):
    b, m, _ = q.shape
    nt = m // TILE
    n_chunks, kblk = _diff_chunking(m)
    return pl.pallas_call(
        functools.partial(_diff_kernel, n_chunks=n_chunks, kblk=kblk, lam_init=lam_init),
        out_shape=jax.ShapeDtypeStruct((b, m, DIFF_HEADS * DIFF_V), BF16),
        grid=(b, DIFF_HEADS, nt),
        in_specs=[pl.BlockSpec(memory_space=pltpu.SMEM),
                  pl.BlockSpec((1, TILE, 2 * HEAD_DIM), lambda bb, hh, i: (bb, i, hh)),
                  pl.BlockSpec((1, m, 2 * HEAD_DIM), lambda bb, hh, i: (bb, 0, hh)),
                  pl.BlockSpec((1, m, 2 * DIFF_V), lambda bb, hh, i: (bb, 0, hh)),
                  pl.BlockSpec((1, DIFF_V), lambda bb, hh, i: (0, 0))],
        out_specs=pl.BlockSpec((1, TILE, DIFF_V), lambda bb, hh, i: (bb, i, hh)),
        scratch_shapes=[pltpu.VMEM((2 * TILE, 2 * DIFF_V), F32), pltpu.VMEM((2 * TILE, 1), F32),
                        pltpu.VMEM((2, 2 * TILE, kblk), F32), pltpu.VMEM((2, 2 * TILE, 1), F32),
                        pltpu.VMEM((2 * TILE, 1), F32)],
        compiler_params=_params(3),
        name="differential_attention",
    )(lam, q, k, v, g)


def _merge_kernel(ona_ref, od_ref, osw_ref, gs_ref, x_ref, mod_ref, wna_ref, wd_ref, wsw_ref, wo_ref,
                  g2_ref, wr_ref, wrt_ref, xo_ref, h2_ref, aff_ref, afft_ref):
    d = x_ref.shape[2]
    gs = gs_ref[0]
    y = (gs[:, :d].astype(F32) * _dot(ona_ref[0], wna_ref[...])
         + gs[:, d:2 * d].astype(F32) * _dot(od_ref[0], wd_ref[...])
         + gs[:, 2 * d:].astype(F32) * _dot(osw_ref[0], wsw_ref[...]))
    mod = mod_ref[0, 0]
    xn = x_ref[0] + mod[2:3] * _dot(y.astype(BF16), wo_ref[...])
    xo_ref[0] = xn
    r = xn * lax.rsqrt(jnp.mean(xn * xn, axis=-1, keepdims=True) + EPS) * g2_ref[...]
    h2 = r * (1.0 + mod[4:5]) + mod[3:4]
    h2_ref[0] = h2.astype(BF16)
    logits = jnp.dot(h2, wr_ref[...], preferred_element_type=F32, precision=HIGHEST)
    e = jnp.exp(logits - jnp.max(logits, axis=1, keepdims=True))
    aff_ref[0] = e / jnp.sum(e, axis=1, keepdims=True)
    lt = lax.dot_general(wrt_ref[...], h2, (((1,), (1,)), ((), ())),
                         preferred_element_type=F32, precision=HIGHEST)
    et = jnp.exp(lt - jnp.max(lt, axis=0, keepdims=True))
    afft_ref[0] = et / jnp.sum(et, axis=0, keepdims=True)


def _merge(o_na, o_d, o_sw, gs, xu, modtab, w_na, w_d, w_sw, w_o, g2, w_r):
    b, m, d = xu.shape
    nt = m // TILE
    ne = w_r.shape[1]
    row = lambda w: pl.BlockSpec((1, TILE, w), lambda bb, i: (bb, i, 0))
    full = lambda a: pl.BlockSpec(a.shape, lambda bb, i: (0,) * a.ndim)
    w_rt = w_r.T
    return pl.pallas_call(
        _merge_kernel,
        out_shape=[jax.ShapeDtypeStruct((b, m, d), F32), jax.ShapeDtypeStruct((b, m, d), BF16),
                   jax.ShapeDtypeStruct((b, m, ne), F32), jax.ShapeDtypeStruct((b, ne, m), F32)],
        grid=(b, nt),
        in_specs=[row(o_na.shape[2]), row(o_d.shape[2]), row(o_sw.shape[2]), row(gs.shape[2]), row(d),
                  pl.BlockSpec((1, 1, N_MOD, d), lambda bb, i: (bb, jnp.minimum(i, 1), 0, 0)),
                  full(w_na), full(w_d), full(w_sw), full(w_o), full(g2), full(w_r), full(w_rt)],
        out_specs=[row(d), row(d), row(ne), pl.BlockSpec((1, ne, TILE), lambda bb, i: (bb, 0, i))],
        compiler_params=_params(2),
        name="merge_residual_router",
    )(o_na, o_d, o_sw, gs, xu, modtab, w_na, w_d, w_sw, w_o, g2, w_r, w_rt)


def _route_kernel(a_ref, gpos_ref, gfull_ref, *, n_latent_sets, cap_lat, cap_ctx):
    ne, c, w = a_ref.shape[1:]
    a = a_ref[0]
    bits = lax.bitcast_convert_type(a, I32)
    cap = jnp.where(pl.program_id(0) < n_latent_sets, cap_lat, cap_ctx).astype(F32)

    def count(mask):
        return jnp.sum(jnp.sum(mask.astype(F32), axis=2, keepdims=True), axis=1, keepdims=True)

    def search(it, thr):
        cand = thr | jnp.left_shift(jnp.int32(1), 30 - it)
        return jnp.where(count(bits >= cand) >= cap, cand, thr)

    thr = lax.fori_loop(0, 31, search, jnp.zeros((ne, 1, 1), I32))
    gt = bits > thr
    eq = bits == thr
    need = cap - count(gt)

    upper = (lax.broadcasted_iota(I32, (w, w), 0) <= lax.broadcasted_iota(I32, (w, w), 1)).astype(BF16)
    lower = (lax.broadcasted_iota(I32, (c, c), 1) < lax.broadcasted_iota(I32, (c, c), 0)).astype(BF16)

    def exclusive_cumsum(mask):
        x = mask.astype(F32).reshape(ne * c, w)
        within = _dot(x.astype(BF16), upper)
        tot = jnp.broadcast_to(within[:, w - 1:w], (ne * c, w)).astype(BF16)
        before = jnp.concatenate([_dot(lower, tot[e * c:(e + 1) * c]) for e in range(ne)], axis=0)
        return (before + within - x).reshape(ne, c, w)

    sel = gt | (eq & (exclusive_cumsum(eq) < need))
    g = exclusive_cumsum(sel).astype(I32)
    gfull_ref[0] = g
    gpos_ref[0] = jnp.where(sel, g, -1)


def _route(aff_sets, n_latent_sets, cap_lat, cap_ctx):
    s, ne, n = aff_sets.shape
    c = n // LANES
    a4 = aff_sets.reshape(s, ne, c, LANES)
    blk = pl.BlockSpec((1, ne, c, LANES), lambda i: (i, 0, 0, 0))
    gpos, gfull = pl.pallas_call(
        functools.partial(_route_kernel, n_latent_sets=n_latent_sets, cap_lat=cap_lat, cap_ctx=cap_ctx),
        out_shape=[jax.ShapeDtypeStruct(a4.shape, I32)] * 2,
        grid=(s,),
        in_specs=[blk],
        out_specs=[blk, blk],
        compiler_params=_params(1),
        name="expert_choice_select",
    )(a4)
    return gpos.reshape(s, ne, n), gfull.reshape(s, ne, n)


def _window(cap):
    w = min(TILE + BF16_ROWS, cap)
    main = min(WINDOW_MAIN_ROWS, w)
    return main, w - main, cap - w


def _windows(gfull, cap):
    b, ne, _ = gfull.shape
    main, _, max_start = _window(cap)
    seg_start = gfull[:, :, ::TILE]
    seg_end = jnp.concatenate([gfull[:, :, TILE::TILE], jnp.full((b, ne, 1), cap, I32)], axis=2)
    astart = jnp.minimum(seg_start // BF16_ROWS * BF16_ROWS, max_start)
    return astart.reshape(-1), (seg_end > astart + main).astype(I32).reshape(-1)


def _gather_kernel(a_ref, t_ref, g_ref, h_ref, o_ref, *, ne, nblk, eg, main, tail):
    b, egi, j = pl.program_id(0), pl.program_id(1), pl.program_id(2)

    @pl.when(j == 0)
    def _():
        o_ref[...] = jnp.zeros(o_ref.shape, o_ref.dtype)

    hb = h_ref[0]
    t = hb.shape[0]

    def place(k, pos, first, rows):
        onehot = jnp.where(lax.broadcasted_iota(I32, (rows, t), 0) == pos - first, 1.0, 0.0).astype(BF16)
        picked = _dot(onehot, hb).astype(BF16)
        sl = (0, k, pl.ds(pl.multiple_of(a_ref[idx(k)] + first, BF16_ROWS), rows), slice(None))
        o_ref[sl] = o_ref[sl] + picked

    def idx(k):
        return (b * ne + egi * eg + k) * nblk + j

    def window_row(k):
        return g_ref[0, k] - a_ref[idx(k)]

    for k in range(eg):
        place(k, window_row(k), 0, main)
    for k in range(eg if tail else 0):
        @pl.when(t_ref[idx(k)] != 0)
        def _(k=k):
            place(k, window_row(k), main, tail)


def _gather(h2u, gpos, astart, need_tail, cap, tile_off, eg):
    b, ne, n = gpos.shape
    d = h2u.shape[2]
    nblk = n // TILE
    main, tail, _ = _window(cap)
    return pl.pallas_call(
        functools.partial(_gather_kernel, ne=ne, nblk=nblk, eg=eg, main=main, tail=tail),
        out_shape=jax.ShapeDtypeStruct((b, ne, cap, d), BF16),
        grid_spec=pltpu.PrefetchScalarGridSpec(
            num_scalar_prefetch=2,
            grid=(b, ne // eg, nblk),
            in_specs=[pl.BlockSpec((1, eg, 1, TILE), lambda bb, e, j, a, nt: (bb, e, 0, j)),
                      pl.BlockSpec((1, TILE, d), lambda bb, e, j, a, nt: (bb, tile_off + j, 0))],
            out_specs=pl.BlockSpec((1, eg, cap, d), lambda bb, e, j, a, nt: (bb, e, 0, 0))),
        compiler_params=_params(3),
        name="expert_gather",
    )(astart, need_tail, gpos.reshape(b, ne, 1, n), h2u)


def _ffn_kernel(x_ref, wg_ref, wu_ref, wd_ref, o_ref, wg_s, wu_s, wd_s):
    @pl.when((pl.program_id(1) == 0) & (pl.program_id(2) == 0))
    def _():
        wg_s[...] = wg_ref[0, 0].astype(BF16)
        wu_s[...] = wu_ref[0, 0].astype(BF16)
        wd_s[...] = wd_ref[0, 0].astype(BF16)

    x = x_ref[0, 0]
    gate = _dot(x, wg_s[...])
    up = _dot(x, wu_s[...])
    hid = (gate / (1.0 + jnp.exp(-gate)) * up).astype(BF16)
    o_ref[0, 0] = _dot(hid, wd_s[...]).astype(BF16)


def _expert_ffn(xs, w_gate, w_up, w_down, layer):
    b, ne, cap, d = xs.shape
    tr = min(cap, 512)
    wspec = lambda w: pl.BlockSpec((1, 1) + w.shape[2:], lambda e, bb, r: (layer, e, 0, 0))
    xspec = pl.BlockSpec((1, 1, tr, d), lambda e, bb, r: (bb, e, r, 0))
    return pl.pallas_call(
        _ffn_kernel,
        out_shape=jax.ShapeDtypeStruct(xs.shape, BF16),
        grid=(ne, b, cap // tr),
        in_specs=[xspec, wspec(w_gate), wspec(w_up), wspec(w_down)],
        out_specs=xspec,
        scratch_shapes=[pltpu.VMEM(w.shape[2:], BF16) for w in (w_gate, w_up, w_down)],
        compiler_params=_params(3),
        name="expert_swiglu",
    )(xs, w_gate, w_up, w_down)


def _combine_kernel(a_ref, t_ref, x_ref, aff_ref, gt_ref, mod_ref, fg_ref, ye_ref, o_ref,
                    win_ref, tail_ref, tacc_ref, sem_ref, tsem_ref, *, ne, nblk, main, tail, final):
    b, j = pl.program_id(0), pl.program_id(1)

    def start_of(e):
        return pl.multiple_of(a_ref[(b * ne + e) * nblk + j], BF16_ROWS)

    def window_copy(e):
        return pltpu.make_async_copy(ye_ref.at[b, e, pl.ds(start_of(e), main)], win_ref.at[e], sem_ref.at[e])

    for e in range(ne):
        window_copy(e).start()
    aff = aff_ref[0]
    gt = gt_ref[0]
    t = aff.shape[0]

    def window_row(e):
        return gt[:, e:e + 1] - start_of(e)

    if tail:
        tacc_ref[...] = jnp.zeros(tacc_ref.shape, F32)
        for e in range(ne):
            @pl.when(t_ref[(b * ne + e) * nblk + j] != 0)
            def _(e=e):
                cp = pltpu.make_async_copy(ye_ref.at[b, e, pl.ds(start_of(e) + main, tail)], tail_ref, tsem_ref.at[0])
                cp.start()
                cp.wait()
                tcols = lax.broadcasted_iota(I32, (t, tail), 1)
                oh = jnp.where(tcols == window_row(e) - main, 1.0, 0.0).astype(BF16)
                tacc_ref[...] += aff[:, e:e + 1] * _dot(oh, tail_ref[...])

    cols = lax.broadcasted_iota(I32, (t, main), 1)
    acc = tacc_ref[...] if tail else jnp.zeros(o_ref.shape[1:], F32)
    for e in range(ne):
        window_copy(e).wait()
        onehot = jnp.where(cols == window_row(e), 1.0, 0.0).astype(BF16)
        acc = acc + aff[:, e:e + 1] * _dot(onehot, win_ref[e])
    x = x_ref[0] + mod_ref[0, 0][5:6] * acc
    if final:
        x = x * lax.rsqrt(jnp.mean(x * x, axis=-1, keepdims=True) + EPS) * fg_ref[...]
    o_ref[0] = x


def _combine(xu, affu, gpos_t, astart, need_tail, ye, modtab, final_g, tile_off, kind, final):
    b, m, d = xu.shape
    _, n, ne = gpos_t.shape
    cap = ye.shape[2]
    nblk = n // TILE
    main, tail, _ = _window(cap)
    row =lambda w: pl.BlockSpec((1, TILE, w), lambda bb, j, a, nt: (bb, tile_off + j, 0))
    out_spec = pl.BlockSpec((1, TILE, d), (lambda bb, j, a, nt: (bb, j, 0)) if final
                            else (lambda bb, j, a, nt: (bb, tile_off + j, 0)))
    return pl.pallas_call(
        functools.partial(_combine_kernel, ne=ne, nblk=nblk, main=main, tail=tail, final=final),
        out_shape=jax.ShapeDtypeStruct((b, n, d) if final else xu.shape, F32),
        grid_spec=pltpu.PrefetchScalarGridSpec(
            num_scalar_prefetch=2,
            grid=(b, nblk),
            in_specs=[row(d), row(ne),
                      pl.BlockSpec((1, TILE, ne), lambda bb, j, a, nt: (bb, j, 0)),
                      pl.BlockSpec((1, 1, N_MOD, d), lambda bb, j, a, nt: (bb, kind, 0, 0)),
                      pl.BlockSpec((1, d), lambda bb, j, a, nt: (0, 0)),
                      pl.BlockSpec(memory_space=pl.ANY)],
            out_specs=out_spec,
            scratch_shapes=[pltpu.VMEM((ne, main, d), BF16), pltpu.VMEM((max(tail, BF16_ROWS), d), BF16),
                            pltpu.VMEM((TILE, d), F32), pltpu.SemaphoreType.DMA((ne,)),
                            pltpu.SemaphoreType.DMA((1,))]),
        input_output_aliases={} if final else {2: 0},
        compiler_params=_params(2),
        name="expert_combine",
    )(astart, need_tail, xu, affu, gpos_t, modtab, final_g, ye)


def _rope_tables(n):
    t = np.arange(n)
    row = (t // GRID_W).astype(np.float32)[:, None]
    col = (t % GRID_W).astype(np.float32)[:, None]
    inv = (ROPE_BASE ** (-np.arange(ROPE_PAIRS, dtype=np.float32) / ROPE_PAIRS)).astype(np.float32)
    ang = np.concatenate([row * inv, row * inv, col * inv, col * inv], axis=-1)
    cos, sin = np.cos(ang), np.sin(ang)
    half = (np.arange(HEAD_DIM) % (2 * ROPE_PAIRS)) < ROPE_PAIRS
    sin = np.where(half[None], -sin, sin)
    cos = np.concatenate([np.ones((CTX, HEAD_DIM)), cos], axis=0)
    sin = np.concatenate([np.zeros((CTX, HEAD_DIM)), sin], axis=0)
    tile2 = lambda a: jnp.asarray(np.concatenate([a, a], axis=1), F32)
    return tile2(cos), tile2(sin)


def _moe(xu, h2u, affu, afft, modtab, weights, layer, final_g, final):
    b, m, d = xu.shape
    n = m - CTX
    w_gate, w_up, w_down = weights
    cap_lat = max(1, EC_CAPACITY * n // N_EXPERTS)
    cap_ctx = max(1, EC_CAPACITY * CTX // N_EXPERTS)
    sets = [afft[:, :, CTX:]]
    if not final:
        sets.append(jnp.concatenate([afft[:, :, :CTX], jnp.full((b, N_EXPERTS, n - CTX), -1.0, F32)], axis=2))
    gpos, gfull = _route(jnp.concatenate(sets, axis=0), b, cap_lat, cap_ctx)

    def run(idx, n_set, cap, tile_off, kind, x_in, fin):
        gp = gpos[idx * b:(idx + 1) * b, :, :n_set]
        gf = gfull[idx * b:(idx + 1) * b, :, :n_set]
        astart, need_tail = _windows(gf, cap)
        xs = _gather(h2u, gp, astart, need_tail, cap, tile_off, eg=4)
        ye = _expert_ffn(xs, w_gate, w_up, w_down, layer)
        return _combine(x_in, affu, jnp.swapaxes(gp, 1, 2), astart, need_tail, ye, modtab, final_g, tile_off, kind, fin)

    if final:
        return run(0, n, cap_lat, 1, 1, xu, True)
    xu = run(0, n, cap_lat, 1, 1, xu, False)
    return run(1, CTX, cap_ctx, 0, 0, xu, False)


def kernel(x, c, ctx, c_ctx, w_mod, b_mod, norm1_g, w_in, na_rpb, diff_lambda, diff_subln_g, swa_sink,
           w_branch_na, w_branch_diff, w_branch_swa, w_out, norm2_g, w_router, w_expert_gate, w_expert_up,
           w_expert_down, final_g):
    b, n, d = x.shape
    depth = w_mod.shape[0]
    assert ctx.shape[1] == CTX and d == D_MODEL and n % (2 * TILE) == 0 and n // TILE >= 3

    cin = jnp.concatenate([c, c_ctx[None], jnp.zeros((8 - b - 1, d), F32)], axis=0)
    mod_all = _modulation(cin, w_mod, b_mod)
    cos_u, sin_u = _rope_tables(n)
    xu = jnp.concatenate([ctx, x], axis=1)
    final_g2 = final_g.reshape(1, d)

    out = None
    for l in range(depth):
        final = l == depth - 1
        lam_init = 0.8 - 0.6 * math.exp(-0.3 * l)
        mod_l = mod_all[l].reshape(8, N_MOD, d)
        modtab = jnp.stack([jnp.broadcast_to(mod_l[b], (b, N_MOD, d)), mod_l[:b]], axis=1)
        lam_p = diff_lambda[l].astype(F32)
        lam = (jnp.exp(jnp.sum(lam_p[0] * lam_p[1])) - jnp.exp(jnp.sum(lam_p[2] * lam_p[3])) + lam_init).reshape(1)

        (q_na, k_na, v_na, q_d, k_d, v_d, q_s, k_s, v_s, gs) = _project(
            xu, modtab, norm1_g[l].reshape(1, d), w_in[l].astype(BF16), cos_u, sin_u)
        o_na = _na_attention(q_na, k_na, v_na, _na_bias_table(na_rpb[l]))
        o_d = _diff_attention(q_d, k_d, v_d, lam, diff_subln_g[l].reshape(1, DIFF_V), lam_init)
        o_s = _swa_attention(q_s, k_s, v_s, swa_sink[l].astype(F32))
        xu, h2u, affu, afft = _merge(
            o_na, o_d, o_s, gs, xu, modtab, w_branch_na[l].astype(BF16), w_branch_diff[l].astype(BF16),
            w_branch_swa[l].astype(BF16), w_out[l].astype(BF16), norm2_g[l].reshape(1, d), w_router[l])
        weights = (w_expert_gate, w_expert_up, w_expert_down)
        res = _moe(xu, h2u, affu, afft, modtab, weights, l, final_g2, final)
        if final:
            out = res
        else:
            xu = res
    return out
```

```python
import functools
import math

import numpy as np
import jax
import jax.numpy as jnp
from jax import lax
from jax.experimental import pallas as pl
from jax.experimental.pallas import tpu as pltpu

F32 = jnp.float32
BF16 = jnp.bfloat16
I32 = jnp.int32
HIGHEST = lax.Precision.HIGHEST

D_MODEL = 1024
CTX = 256
TILE = 256
GRID_W = 64
HEAD_DIM = 64
ROPE_PAIRS = HEAD_DIM // 4
ROPE_BASE = 10000.0
EPS = 1e-6
N_MOD = 6
NA_HEADS = 4
NA_WIN_ROWS = 8
NA_WIN_COLS = 16
NA_TILE_ROWS = TILE // GRID_W
DIFF_HEADS = 4
DIFF_V = 2 * HEAD_DIM
SWA_HEADS = 4
SWA_KV_HEADS = 2
SWA_WINDOW = 128
N_EXPERTS = 16
EC_CAPACITY = 2
NEG = -1e30
LANES = 128
MXU_TILE = 256
LOG2E = math.log2(math.e)
DIFF_MAX_DENOMINATOR = 2.0 ** 64
BF16_ROWS = 16
WINDOW_MAIN_ROWS = 64

C_QNA, C_QD, C_QS = 0, 256, 768
C_KNA, C_KD, C_KS = 1024, 1280, 1792
C_VNA, C_VD, C_VS = 1920, 2176, 2688
C_GATE, C_END = 2816, 5888

VMEM_LIMIT = 56 * 1024 * 1024


def _params(n_axes, vmem=VMEM_LIMIT):
    return pltpu.CompilerParams(dimension_semantics=("arbitrary",) * n_axes, vmem_limit_bytes=vmem)


def _nt_dot(a, b):
    return lax.dot_general(a, b, (((1,), (1,)), ((), ())), preferred_element_type=F32)


def _dot(a, b):
    return jnp.dot(a, b, preferred_element_type=F32)


def _mod_kernel(c_ref, w_ref, b_ref, o_ref):
    c = c_ref[...]
    s = c / (1.0 + jnp.exp(-c))
    o_ref[0] = jnp.dot(s, w_ref[0], preferred_element_type=F32, precision=HIGHEST) + b_ref[0]


def _modulation(cin, w_mod, b_mod):
    depth, d, w = w_mod.shape
    tn = 1024
    return pl.pallas_call(
        _mod_kernel,
        out_shape=jax.ShapeDtypeStruct((depth, 8, w), F32),
        grid=(depth, w // tn),
        in_specs=[pl.BlockSpec((8, d), lambda l, j: (0, 0)),
                  pl.BlockSpec((1, d, tn), lambda l, j: (l, 0, j)),
                  pl.BlockSpec((1, 1, tn), lambda l, j: (l, 0, j))],
        out_specs=pl.BlockSpec((1, 8, tn), lambda l, j: (l, 0, j)),
        compiler_params=_params(2),
        name="modulation",
    )(cin, w_mod, b_mod.reshape(depth, 1, w))


def _proj_kernel(x_ref, mod_ref, g_ref, w_ref, cos_ref, sin_ref,
                 qna_ref, kna_ref, vna_ref, qd_ref, kd_ref, vd_ref, qs_ref, ks_ref, vs_ref, gs_ref):
    x = x_ref[0]
    mod = mod_ref[0, 0]
    y = x * lax.rsqrt(jnp.mean(x * x, axis=-1, keepdims=True) + EPS) * g_ref[...]
    h = (y * (1.0 + mod[1:2]) + mod[0:1]).astype(BF16)

    def proj(a, b):
        return _dot(h, w_ref[:, a:b])

    cos = cos_ref[...]
    sin = sin_ref[...]
    lane = lax.broadcasted_iota(I32, cos.shape, 1)
    first_half = (lane % (2 * ROPE_PAIRS)) < ROPE_PAIRS

    def rope(t):
        outs = []
        for j in range(t.shape[1] // LANES):
            c = t[:, j * LANES:(j + 1) * LANES]
            r = jnp.where(first_half, pltpu.roll(c, LANES - ROPE_PAIRS, 1), pltpu.roll(c, ROPE_PAIRS, 1))
            outs.append(c * cos + r * sin)
        return outs[0] if len(outs) == 1 else jnp.concatenate(outs, axis=1)

    scale = HEAD_DIM ** -0.5
    qna_ref[0] = (proj(C_QNA, C_QD) * scale).astype(BF16)
    qd_ref[0] = (rope(proj(C_QD, C_QS)) * (scale * LOG2E)).astype(BF16)
    qs_ref[0] = (rope(proj(C_QS, C_KNA)) * scale).astype(BF16)
    kna_ref[0] = proj(C_KNA, C_KD).astype(BF16)
    kd_ref[0] = rope(proj(C_KD, C_KS)).astype(BF16)
    ks_ref[0] = rope(proj(C_KS, C_VNA)).astype(BF16)
    vna_ref[0] = proj(C_VNA, C_VD).astype(BF16)
    vd = proj(C_VD, C_VS).astype(BF16)
    ones = jnp.ones((vd.shape[0], DIFF_V), BF16)
    vd_ref[0] = jnp.concatenate(
        [blk for hd in range(DIFF_HEADS) for blk in (vd[:, hd * DIFF_V:(hd + 1) * DIFF_V], ones)], axis=1)
    vs_ref[0] = proj(C_VS, C_GATE).astype(BF16)
    gates = proj(C_GATE, C_END)
    gs_ref[0] = (1.0 / (1.0 + jnp.exp(-gates))).astype(BF16)


def _project(xu, modtab, g, w_in, cos_u, sin_u):
    b, m, d = xu.shape
    nt = m // TILE
    widths = (256, 256, 256, 512, 512, 2 * DIFF_HEADS * DIFF_V, 256, 128, 128, C_END - C_GATE)
    row = lambda w: pl.BlockSpec((1, TILE, w), lambda bb, i: (bb, i, 0))
    return pl.pallas_call(
        _proj_kernel,
        out_shape=[jax.ShapeDtypeStruct((b, m, w), BF16) for w in widths],
        grid=(b, nt),
        in_specs=[row(d),
                  pl.BlockSpec((1, 1, N_MOD, d), lambda bb, i: (bb, jnp.minimum(i, 1), 0, 0)),
                  pl.BlockSpec((1, d), lambda bb, i: (0, 0)),
                  pl.BlockSpec((d, C_END), lambda bb, i: (0, 0)),
                  pl.BlockSpec((TILE, LANES), lambda bb, i: (i, 0)),
                  pl.BlockSpec((TILE, LANES), lambda bb, i: (i, 0))],
        out_specs=[row(w) for w in widths],
        compiler_params=_params(2),
        name="norm_project_rope",
    )(xu, modtab, g, w_in, cos_u, sin_u)


def _na_kernel(q_ref, kc_ref, kp_ref, kcur_ref, kn_ref, vc_ref, vp_ref, vcur_ref, vn_ref, bias_ref, o_ref):
    q = q_ref[0]
    kc, vc = kc_ref[0], vc_ref[0]
    kl = jnp.concatenate([kp_ref[0], kcur_ref[0], kn_ref[0]], axis=0)
    vl = jnp.concatenate([vp_ref[0], vcur_ref[0], vn_ref[0]], axis=0)
    outs = []
    for hd in range(NA_HEADS):
        sl = slice(hd * HEAD_DIM, (hd + 1) * HEAD_DIM)
        qh = q[:, sl]
        s_c = _nt_dot(qh, kc[:, sl])
        s_l = _nt_dot(qh, kl[:, sl]) + bias_ref[0, hd]
        mx = jnp.maximum(jnp.max(s_c, axis=1, keepdims=True), jnp.max(s_l, axis=1, keepdims=True))
        p_c = jnp.exp(s_c - mx)
        p_l = jnp.exp(s_l - mx)
        den = jnp.sum(p_c, axis=1, keepdims=True) + jnp.sum(p_l, axis=1, keepdims=True)
        o = _dot(p_c.astype(BF16), vc[:, sl]) + _dot(p_l.astype(BF16), vl[:, sl])
        outs.append(o / den)
    o_ref[0] = jnp.concatenate(outs, axis=1).astype(BF16)


def _na_bias_table(rpb):
    tr, nk = NA_TILE_ROWS, 3 * NA_TILE_ROWS
    qc = np.arange(GRID_W)[:, None]
    kc = np.arange(GRID_W)[None, :]
    cstart = np.clip(qc - NA_WIN_COLS // 2, 0, GRID_W - NA_WIN_COLS)
    col_ok = (kc >= cstart) & (kc < cstart + NA_WIN_COLS)
    dc = np.clip(kc - qc, -(NA_WIN_COLS - 1), NA_WIN_COLS - 1) + NA_WIN_COLS - 1
    onehot = jnp.asarray(np.arange(2 * NA_WIN_COLS - 1)[:, None, None] == dc[None], F32)
    cols = jnp.einsum("hrc,cqk->hrqk", rpb.astype(F32), onehot, precision=HIGHEST)
    qr = np.arange(tr)[:, None]
    krow = np.arange(nk)[None, :] - tr
    dr = np.clip(krow - qr, -(NA_WIN_ROWS - 1), NA_WIN_ROWS - 1) + NA_WIN_ROWS - 1
    starts = (0 * qr, qr - NA_WIN_ROWS // 2, 0 * qr + tr - NA_WIN_ROWS)
    row_ok = np.stack([(krow >= st) & (krow < st + NA_WIN_ROWS) for st in starts])
    t = jnp.take(cols, jnp.asarray(dr.reshape(-1)), axis=1)
    t = t.reshape(NA_HEADS, tr, nk, GRID_W, GRID_W).transpose(0, 1, 3, 2, 4)
    ok = row_ok[:, None, :, None, :, None] & col_ok[None, None, None, :, None, :]
    t = jnp.where(ok, t[None], NEG).reshape(3, NA_HEADS, TILE, 3 * TILE)
    return jnp.concatenate([jnp.full_like(t[:1], NEG), t], axis=0)


def _na_attention(q, k, v, bias):
    b, m, w = q.shape
    nt = m // TILE
    nb = nt - 1

    def lat(off):
        return lambda bb, i: (bb, jnp.clip(i - 1 + off, 0, nb - 1) + 1, 0)

    def kind(bb, i):
        return (jnp.where(i == 0, 0, jnp.where(i == 1, 1, jnp.where(i == nb, 3, 2))), 0, 0, 0)

    blk = lambda f: pl.BlockSpec((1, TILE, w), f)
    kv_specs = [blk(lambda bb, i: (bb, 0, 0)), blk(lat(-1)), blk(lat(0)), blk(lat(1))]
    return pl.pallas_call(
        _na_kernel,
        out_shape=jax.ShapeDtypeStruct((b, m, w), BF16),
        grid=(b, nt),
        in_specs=[blk(lambda bb, i: (bb, i, 0))] + kv_specs + kv_specs
                 + [pl.BlockSpec((1, NA_HEADS, TILE, 3 * TILE), kind)],
        out_specs=blk(lambda bb, i: (bb, i, 0)),
        compiler_params=_params(2),
        name="neighbourhood_attention",
    )(q, k, k, k, k, v, v, v, v, bias)


def _swa_kernel(sink_ref, q_ref, kc_ref, kp_ref, kcur_ref, kn_ref, vc_ref, vp_ref, vcur_ref, vn_ref, o_ref,
                *, n_lat):
    i = pl.program_id(1)
    q = q_ref[0]
    kc, vc = kc_ref[0], vc_ref[0]
    kl = jnp.concatenate([kp_ref[0], kcur_ref[0], kn_ref[0]], axis=0)
    vl = jnp.concatenate([vp_ref[0], vcur_ref[0], vn_ref[0]], axis=0)
    group = SWA_HEADS // SWA_KV_HEADS
    rows = group * TILE
    qpos = (i - 1) * TILE + lax.broadcasted_iota(I32, (rows, 3 * TILE), 0) % TILE
    kpos = (i - 2) * TILE + lax.broadcasted_iota(I32, (rows, 3 * TILE), 1)
    ok = (kpos >= 0) & (kpos < n_lat) & (jnp.abs(qpos - kpos) <= SWA_WINDOW) & (i > 0)
    rid = lax.broadcasted_iota(I32, (rows, 1), 0)
    outs = []
    for g in range(SWA_KV_HEADS):
        ksl = slice(g * HEAD_DIM, (g + 1) * HEAD_DIM)
        qg = jnp.concatenate([q[:, (g * group + j) * HEAD_DIM:(g * group + j + 1) * HEAD_DIM]
                              for j in range(group)], axis=0)
        sink = jnp.zeros((rows, 1), F32)
        for j in range(group):
            sink = jnp.where(rid // TILE == j, sink_ref[g * group + j], sink)
        s_c = _nt_dot(qg, kc[:, ksl])
        s_l = jnp.where(ok, _nt_dot(qg, kl[:, ksl]), NEG)
        mx = jnp.maximum(jnp.maximum(jnp.max(s_c, axis=1, keepdims=True), jnp.max(s_l, axis=1, keepdims=True)), sink)
        p_c = jnp.exp(s_c - mx)
        p_l = jnp.exp(s_l - mx)
        den = jnp.sum(p_c, axis=1, keepdims=True) + jnp.sum(p_l, axis=1, keepdims=True) + jnp.exp(sink - mx)
        o = (_dot(p_c.astype(BF16), vc[:, ksl]) + _dot(p_l.astype(BF16), vl[:, ksl])) / den
        outs.extend(o[j * TILE:(j + 1) * TILE] for j in range(group))
    o_ref[0] = jnp.concatenate(outs, axis=1).astype(BF16)


def _swa_attention(q, k, v, sink):
    b, m, wq = q.shape
    wk = k.shape[2]
    nt = m // TILE
    nb = nt - 1

    def lat(off):
        return lambda bb, i: (bb, jnp.clip(i - 1 + off, 0, nb - 1) + 1, 0)

    blk = lambda w, f: pl.BlockSpec((1, TILE, w), f)
    kv_specs = [blk(wk, lambda bb, i: (bb, 0, 0)), blk(wk, lat(-1)), blk(wk, lat(0)), blk(wk, lat(1))]
    return pl.pallas_call(
        functools.partial(_swa_kernel, n_lat=m - CTX),
        out_shape=jax.ShapeDtypeStruct((b, m, wq), BF16),
        grid=(b, nt),
        in_specs=[pl.BlockSpec(memory_space=pltpu.SMEM), blk(wq, lambda bb, i: (bb, i, 0))] + kv_specs + kv_specs,
        out_specs=blk(wq, lambda bb, i: (bb, i, 0)),
        compiler_params=_params(2),
        name="windowed_attention",
    )(sink, q, k, k, k, k, v, v, v, v)


def _diff_kernel(lam_ref, q_ref, k_ref, v_ref, g_ref, o_ref, acc_ref, m_ref, s_ref, cmax_ref,
                 *, n_chunks, kblk, fast_kblk, lam_init):
    i = pl.program_id(2)
    q = q_ref[0]
    lane = lax.broadcasted_iota(I32, q.shape, 1)
    zero = jnp.zeros_like(q)
    qq = jnp.concatenate([jnp.where(lane < HEAD_DIM, q, zero), jnp.where(lane >= HEAD_DIM, q, zero)], axis=0)

    def chunk(c):
        return pl.ds(pl.multiple_of(c * kblk, LANES), kblk)

    def scores(c, slot):
        s = _nt_dot(qq, k_ref[0, chunk(c), :])
        s_ref[slot] = s
        cmax_ref[slot] = jnp.max(s, axis=1, keepdims=True)

    def accumulate(s, cmax, v):
        m_prev = m_ref[...]
        m_new = jnp.maximum(m_prev, cmax)
        p = jnp.exp2(s - m_new).astype(BF16)
        acc_ref[...] = jnp.exp2(m_prev - m_new) * acc_ref[...] + _dot(p, v)
        m_ref[...] = m_new

    def reset():
        m_ref[...] = jnp.full(m_ref.shape, NEG, F32)
        acc_ref[...] = jnp.zeros(acc_ref.shape, F32)

    def safe_sweep():
        reset()
        scores(0, 0)

        def pair(c, last):
            for cur in (0, 1):
                if not (last and cur == 1):
                    scores(c + cur + 1, 1 - cur)
                accumulate(s_ref[cur], cmax_ref[cur], v_ref[0, chunk(c + cur), :])

        def body(c2, carry):
            pair(2 * c2, False)
            return carry

        lax.fori_loop(0, n_chunks // 2 - 1, body, 0)
        pair(n_chunks - 2, True)

    def fast_sweep():
        keys = lambda c: slice(c * fast_kblk, (c + 1) * fast_kblk)
        s = _nt_dot(qq, k_ref[0, keys(0), :])
        m0 = jnp.max(s, axis=1, keepdims=True)
        acc = _dot(jnp.exp2(s - m0).astype(BF16), v_ref[0, keys(0), :])
        for c in range(1, n_chunks * kblk // fast_kblk):
            s = _nt_dot(qq, k_ref[0, keys(c), :])
            acc = acc + _dot(jnp.exp2(s - m0).astype(BF16), v_ref[0, keys(c), :])
        acc_ref[...] = acc

    @pl.when(i == 0)
    def _():
        reset()
        s = _nt_dot(qq, k_ref[0, 0:CTX, :])
        accumulate(s, jnp.max(s, axis=1, keepdims=True), v_ref[0, 0:CTX, :])

    @pl.when(i > 0)
    def _():
        fast_sweep()

        @pl.when(jnp.logical_not(jnp.max(acc_ref[:, DIFF_V:]) < DIFF_MAX_DENOMINATOR))
        def _():
            safe_sweep()

    acc = acc_ref[...]
    o = acc[:, :DIFF_V] / acc[:, DIFF_V:]
    d = o[:TILE] - lam_ref[0] * o[TILE:]
    y = d * lax.rsqrt(jnp.mean(d * d, axis=-1, keepdims=True) + EPS) * g_ref[...]
    o_ref[0] = (y * (1.0 - lam_init)).astype(BF16)


def _diff_chunking(m, max_chunk=2048):
    for n_chunks in range(2, m // LANES + 1, 2):
        if m % (n_chunks * LANES) == 0 and m // n_chunks <= max_chunk:
            return n_chunks, m // n_chunks
    raise ValueError(f"no chunking for {m} keys")


def _diff_attention(q, k, v, lam, g, lam_init):
    b, m, _ = q.shape
    nt = m // TILE
    n_chunks, kblk = _diff_chunking(m)
    fast_kblk = max(k for k in range(MXU_TILE, 2048 + 1, MXU_TILE) if m % k == 0)
    return pl.pallas_call(
        functools.partial(_diff_kernel, n_chunks=n_chunks, kblk=kblk, fast_kblk=fast_kblk, lam_init=lam_init),
        out_shape=jax.ShapeDtypeStruct((b, m, DIFF_HEADS * DIFF_V), BF16),
        grid=(b, DIFF_HEADS, nt),
        in_specs=[pl.BlockSpec(memory_space=pltpu.SMEM),
                  pl.BlockSpec((1, TILE, 2 * HEAD_DIM), lambda bb, hh, i: (bb, i, hh)),
                  pl.BlockSpec((1, m, 2 * HEAD_DIM), lambda bb, hh, i: (bb, 0, hh)),
                  pl.BlockSpec((1, m, 2 * DIFF_V), lambda bb, hh, i: (bb, 0, hh)),
                  pl.BlockSpec((1, DIFF_V), lambda bb, hh, i: (0, 0))],
        out_specs=pl.BlockSpec((1, TILE, DIFF_V), lambda bb, hh, i: (bb, i, hh)),
        scratch_shapes=[pltpu.VMEM((2 * TILE, 2 * DIFF_V), F32), pltpu.VMEM((2 * TILE, 1), F32),
                        pltpu.VMEM((2, 2 * TILE, kblk), F32), pltpu.VMEM((2, 2 * TILE, 1), F32)],
        compiler_params=_params(3),
        name="differential_attention",
    )(lam, q, k, v, g)


def _merge_kernel(ona_ref, od_ref, osw_ref, gs_ref, x_ref, mod_ref, wna_ref, wd_ref, wsw_ref, wo_ref,
                  g2_ref, wrt_ref, xo_ref, h2_ref, afft_ref):
    d = x_ref.shape[2]
    gs = gs_ref[0]
    y = (gs[:, :d].astype(F32) * _dot(ona_ref[0], wna_ref[...])
         + gs[:, d:2 * d].astype(F32) * _dot(od_ref[0], wd_ref[...])
         + gs[:, 2 * d:].astype(F32) * _dot(osw_ref[0], wsw_ref[...]))
    mod = mod_ref[0, 0]
    xn = x_ref[0] + mod[2:3] * _dot(y.astype(BF16), wo_ref[...])
    xo_ref[0] = xn
    r = xn * lax.rsqrt(jnp.mean(xn * xn, axis=-1, keepdims=True) + EPS) * g2_ref[...]
    h2 = r * (1.0 + mod[4:5]) + mod[3:4]
    h2_ref[0] = h2.astype(BF16)
    lt = lax.dot_general(wrt_ref[...], h2, (((1,), (1,)), ((), ())),
                         preferred_element_type=F32, precision=HIGHEST)
    et = jnp.exp(lt - jnp.max(lt, axis=0, keepdims=True))
    afft_ref[0] = et / jnp.sum(et, axis=0, keepdims=True)


def _merge(o_na, o_d, o_sw, gs, xu, modtab, w_na, w_d, w_sw, w_o, g2, w_r):
    b, m, d = xu.shape
    nt = m // TILE
    ne = w_r.shape[1]
    row = lambda w: pl.BlockSpec((1, TILE, w), lambda bb, i: (bb, i, 0))
    full = lambda a: pl.BlockSpec(a.shape, lambda bb, i: (0,) * a.ndim)
    w_rt = w_r.T
    return pl.pallas_call(
        _merge_kernel,
        out_shape=[jax.ShapeDtypeStruct((b, m, d), F32), jax.ShapeDtypeStruct((b, m, d), BF16),
                   jax.ShapeDtypeStruct((b, ne, m), F32)],
        grid=(b, nt),
        in_specs=[row(o_na.shape[2]), row(o_d.shape[2]), row(o_sw.shape[2]), row(gs.shape[2]), row(d),
                  pl.BlockSpec((1, 1, N_MOD, d), lambda bb, i: (bb, jnp.minimum(i, 1), 0, 0)),
                  full(w_na), full(w_d), full(w_sw), full(w_o), full(g2), full(w_rt)],
        out_specs=[row(d), row(d), pl.BlockSpec((1, ne, TILE), lambda bb, i: (bb, 0, i))],
        compiler_params=_params(2),
        name="merge_residual_router",
    )(o_na, o_d, o_sw, gs, xu, modtab, w_na, w_d, w_sw, w_o, g2, w_rt)


def _route_kernel(a_ref, gpos_ref, gfull_ref, *, n_latent_sets, cap_lat, cap_ctx):
    ne, c, w = a_ref.shape[1:]
    a = a_ref[0]
    bits = lax.bitcast_convert_type(a, I32)
    cap = jnp.where(pl.program_id(0) < n_latent_sets, cap_lat, cap_ctx).astype(F32)

    def count(mask):
        return jnp.sum(jnp.sum(mask.astype(F32), axis=2, keepdims=True), axis=1, keepdims=True)

    def search(it, thr):
        cand = thr | jnp.left_shift(jnp.int32(1), 30 - it)
        return jnp.where(count(bits >= cand) >= cap, cand, thr)

    thr = lax.fori_loop(0, 31, search, jnp.zeros((ne, 1, 1), I32))
    gt = bits > thr
    eq = bits == thr
    need = cap - count(gt)

    upper = (lax.broadcasted_iota(I32, (w, w), 0) <= lax.broadcasted_iota(I32, (w, w), 1)).astype(BF16)
    lower = (lax.broadcasted_iota(I32, (c, c), 1) < lax.broadcasted_iota(I32, (c, c), 0)).astype(BF16)

    def exclusive_cumsum(mask):
        x = mask.astype(F32).reshape(ne * c, w)
        within = _dot(x.astype(BF16), upper)
        tot = jnp.broadcast_to(within[:, w - 1:w], (ne * c, w)).astype(BF16)
        before = jnp.concatenate([_dot(lower, tot[e * c:(e + 1) * c]) for e in range(ne)], axis=0)
        return (before + within - x).reshape(ne, c, w)

    sel = gt | (eq & (exclusive_cumsum(eq) < need))
    g = exclusive_cumsum(sel).astype(I32)
    gfull_ref[0] = g
    gpos_ref[0] = jnp.where(sel, g, -1)


def _route(aff_sets, n_latent_sets, cap_lat, cap_ctx):
    s, ne, n = aff_sets.shape
    c = n // LANES
    a4 = aff_sets.reshape(s, ne, c, LANES)
    blk = pl.BlockSpec((1, ne, c, LANES), lambda i: (i, 0, 0, 0))
    gpos, gfull = pl.pallas_call(
        functools.partial(_route_kernel, n_latent_sets=n_latent_sets, cap_lat=cap_lat, cap_ctx=cap_ctx),
        out_shape=[jax.ShapeDtypeStruct(a4.shape, I32)] * 2,
        grid=(s,),
        in_specs=[blk],
        out_specs=[blk, blk],
        compiler_params=_params(1),
        name="expert_choice_select",
    )(a4)
    return gpos.reshape(s, ne, n), gfull.reshape(s, ne, n)


def _window(cap):
    w = min(TILE + BF16_ROWS, cap)
    main = min(WINDOW_MAIN_ROWS, w)
    return main, w - main, cap - w


def _windows(gfull, cap):
    b, ne, _ = gfull.shape
    main, _, max_start = _window(cap)
    seg_start = gfull[:, :, ::TILE]
    seg_end = jnp.concatenate([gfull[:, :, TILE::TILE], jnp.full((b, ne, 1), cap, I32)], axis=2)
    astart = jnp.minimum(seg_start // BF16_ROWS * BF16_ROWS, max_start)
    return astart.reshape(-1), (seg_end > astart + main).astype(I32).reshape(-1)


def _gather_kernel(a_ref, t_ref, g_ref, aff_ref, h_ref, o_ref, w_ref, *, ne, nblk, eg, main, tail):
    b, egi, j = pl.program_id(0), pl.program_id(1), pl.program_id(2)

    @pl.when(j == 0)
    def _():
        o_ref[...] = jnp.zeros(o_ref.shape, o_ref.dtype)
        w_ref[...] = jnp.zeros(w_ref.shape, w_ref.dtype)

    hb = h_ref[0]
    t = hb.shape[0]

    def place(k, pos, first, rows):
        hit = lax.broadcasted_iota(I32, (rows, t), 0) == pos - first
        picked = _dot(jnp.where(hit, 1.0, 0.0).astype(BF16), hb).astype(BF16)
        weight = jnp.sum(jnp.where(hit, aff_ref[0, k], 0.0), axis=1, keepdims=True)
        sl = (0, k, pl.ds(pl.multiple_of(a_ref[idx(k)] + first, BF16_ROWS), rows), slice(None))
        o_ref[sl] = o_ref[sl] + picked
        w_ref[sl] = w_ref[sl] + weight

    def idx(k):
        return (b * ne + egi * eg + k) * nblk + j

    def window_row(k):
        return g_ref[0, k] - a_ref[idx(k)]

    for k in range(eg):
        place(k, window_row(k), 0, main)
    for k in range(eg if tail else 0):
        @pl.when(t_ref[idx(k)] != 0)
        def _(k=k):
            place(k, window_row(k), main, tail)


def _gather(h2u, afft, gpos, astart, need_tail, cap, tile_off, eg):
    b, ne, n = gpos.shape
    m, d = h2u.shape[1:]
    nblk = n // TILE
    main, tail, _ = _window(cap)
    lists = lambda w: pl.BlockSpec((1, eg, cap, w), lambda bb, e, j, a, nt: (bb, e, 0, 0))
    return pl.pallas_call(
        functools.partial(_gather_kernel, ne=ne, nblk=nblk, eg=eg, main=main, tail=tail),
        out_shape=[jax.ShapeDtypeStruct((b, ne, cap, d), BF16), jax.ShapeDtypeStruct((b, ne, cap, 1), F32)],
        grid_spec=pltpu.PrefetchScalarGridSpec(
            num_scalar_prefetch=2,
            grid=(b, ne // eg, nblk),
            in_specs=[pl.BlockSpec((1, eg, 1, TILE), lambda bb, e, j, a, nt: (bb, e, 0, j)),
                      pl.BlockSpec((1, eg, 1, TILE), lambda bb, e, j, a, nt: (bb, e, 0, tile_off + j)),
                      pl.BlockSpec((1, TILE, d), lambda bb, e, j, a, nt: (bb, tile_off + j, 0))],
            out_specs=[lists(d), lists(1)]),
        compiler_params=_params(3),
        name="expert_gather",
    )(astart, need_tail, gpos.reshape(b, ne, 1, n), afft.reshape(b, ne, 1, m), h2u)


def _ffn_kernel(x_ref, rw_ref, wg_ref, wu_ref, wd_ref, o_ref, wg_s, wu_s, wd_s):
    @pl.when((pl.program_id(1) == 0) & (pl.program_id(2) == 0))
    def _():
        wg_s[...] = wg_ref[0, 0].astype(BF16)
        wu_s[...] = wu_ref[0, 0].astype(BF16)
        wd_s[...] = wd_ref[0, 0].astype(BF16)

    x = x_ref[0, 0]
    gate = _dot(x, wg_s[...])
    up = _dot(x, wu_s[...])
    hid = (gate / (1.0 + jnp.exp(-gate)) * up).astype(BF16)
    o_ref[0, 0] = (_dot(hid, wd_s[...]) * rw_ref[0, 0]).astype(BF16)


def _expert_ffn(xs, ws, w_gate, w_up, w_down, layer):
    b, ne, cap, d = xs.shape
    tr = min(cap, 512)
    wspec = lambda w: pl.BlockSpec((1, 1) + w.shape[2:], lambda e, bb, r: (layer, e, 0, 0))
    rows = lambda w: pl.BlockSpec((1, 1, tr, w), lambda e, bb, r: (bb, e, r, 0))
    return pl.pallas_call(
        _ffn_kernel,
        out_shape=jax.ShapeDtypeStruct(xs.shape, BF16),
        grid=(ne, b, cap // tr),
        in_specs=[rows(d), rows(1), wspec(w_gate), wspec(w_up), wspec(w_down)],
        out_specs=rows(d),
        scratch_shapes=[pltpu.VMEM(w.shape[2:], BF16) for w in (w_gate, w_up, w_down)],
        compiler_params=_params(3),
        name="expert_swiglu",
    )(xs, ws, w_gate, w_up, w_down)


def _combine_kernel(a_ref, t_ref, x_ref, gt_ref, mod_ref, fg_ref, ye_ref, o_ref,
                    win_ref, tail_ref, tacc_ref, sem_ref, tsem_ref, *, ne, nblk, main, tail, final):
    b, j = pl.program_id(0), pl.program_id(1)
    step = b * nblk + j
    slot = step % 2

    def start_at(bb, jj, e):
        return pl.multiple_of(a_ref[(bb * ne + e) * nblk + jj], BF16_ROWS)

    def start_of(e):
        return start_at(b, j, e)

    def window_copy(bb, jj, e, sl):
        return pltpu.make_async_copy(ye_ref.at[bb, e, pl.ds(start_at(bb, jj, e), main)],
                                     win_ref.at[sl, e], sem_ref.at[sl, e])

    @pl.when(step == 0)
    def _():
        for e in range(ne):
            window_copy(b, j, e, slot).start()

    @pl.when(step + 1 < pl.num_programs(0) * nblk)
    def _():
        nxt = step + 1
        for e in range(ne):
            window_copy(nxt // nblk, nxt % nblk, e, 1 - slot).start()

    gt = gt_ref[0]
    t = gt.shape[0]

    def window_row(e):
        return gt[:, e:e + 1] - start_of(e)

    if tail:
        tacc_ref[...] = jnp.zeros(tacc_ref.shape, F32)
        for e in range(ne):
            @pl.when(t_ref[(b * ne + e) * nblk + j] != 0)
            def _(e=e):
                cp = pltpu.make_async_copy(ye_ref.at[b, e, pl.ds(start_of(e) + main, tail)], tail_ref, tsem_ref.at[0])
                cp.start()
                cp.wait()
                tcols = lax.broadcasted_iota(I32, (t, tail), 1)
                oh = jnp.where(tcols == window_row(e) - main, 1.0, 0.0).astype(BF16)
                tacc_ref[...] += _dot(oh, tail_ref[...])

    cols = lax.broadcasted_iota(I32, (t, main), 1)
    acc = tacc_ref[...] if tail else jnp.zeros(o_ref.shape[1:], F32)
    for e in range(ne):
        window_copy(b, j, e, slot).wait()
        onehot = jnp.where(cols == window_row(e), 1.0, 0.0).astype(BF16)
        acc = acc + _dot(onehot, win_ref[slot, e])
    x = x_ref[0] + mod_ref[0, 0][5:6] * acc
    if final:
        x = x * lax.rsqrt(jnp.mean(x * x, axis=-1, keepdims=True) + EPS) * fg_ref[...]
    o_ref[0] = x


def _combine(xu, gpos_t, astart, need_tail, ye, modtab, final_g, tile_off, kind, final):
    b, m, d = xu.shape
    _, n, ne = gpos_t.shape
    cap = ye.shape[2]
    nblk = n // TILE
    main, tail, _ = _window(cap)
    out_spec = pl.BlockSpec((1, TILE, d), (lambda bb, j, a, nt: (bb, j, 0)) if final
                            else (lambda bb, j, a, nt: (bb, tile_off + j, 0)))
    return pl.pallas_call(
        functools.partial(_combine_kernel, ne=ne, nblk=nblk, main=main, tail=tail, final=final),
        out_shape=jax.ShapeDtypeStruct((b, n, d) if final else xu.shape, F32),
        grid_spec=pltpu.PrefetchScalarGridSpec(
            num_scalar_prefetch=2,
            grid=(b, nblk),
            in_specs=[pl.BlockSpec((1, TILE, d), lambda bb, j, a, nt: (bb, tile_off + j, 0)),
                      pl.BlockSpec((1, TILE, ne), lambda bb, j, a, nt: (bb, j, 0)),
                      pl.BlockSpec((1, 1, N_MOD, d), lambda bb, j, a, nt: (bb, kind, 0, 0)),
                      pl.BlockSpec((1, d), lambda bb, j, a, nt: (0, 0)),
                      pl.BlockSpec(memory_space=pl.ANY)],
            out_specs=out_spec,
            scratch_shapes=[pltpu.VMEM((2, ne, main, d), BF16), pltpu.VMEM((max(tail, BF16_ROWS), d), BF16),
                            pltpu.VMEM((TILE, d), F32), pltpu.SemaphoreType.DMA((2, ne)),
                            pltpu.SemaphoreType.DMA((1,))]),
        input_output_aliases={} if final else {2: 0},
        compiler_params=_params(2),
        name="expert_combine",
    )(astart, need_tail, xu, gpos_t, modtab, final_g, ye)


def _rope_tables(n):
    t = np.arange(n)
    row = (t // GRID_W).astype(np.float32)[:, None]
    col = (t % GRID_W).astype(np.float32)[:, None]
    inv = (ROPE_BASE ** (-np.arange(ROPE_PAIRS, dtype=np.float32) / ROPE_PAIRS)).astype(np.float32)
    ang = np.concatenate([row * inv, row * inv, col * inv, col * inv], axis=-1)
    cos, sin = np.cos(ang), np.sin(ang)
    half = (np.arange(HEAD_DIM) % (2 * ROPE_PAIRS)) < ROPE_PAIRS
    sin = np.where(half[None], -sin, sin)
    cos = np.concatenate([np.ones((CTX, HEAD_DIM)), cos], axis=0)
    sin = np.concatenate([np.zeros((CTX, HEAD_DIM)), sin], axis=0)
    tile2 = lambda a: jnp.asarray(np.concatenate([a, a], axis=1), F32)
    return tile2(cos), tile2(sin)


def _moe(xu, h2u, afft, modtab, weights, layer, final_g, final):
    b, m, d = xu.shape
    n = m - CTX
    w_gate, w_up, w_down = weights
    cap_lat = max(1, EC_CAPACITY * n // N_EXPERTS)
    cap_ctx = max(1, EC_CAPACITY * CTX // N_EXPERTS)
    sets = [afft[:, :, CTX:]]
    if not final:
        sets.append(jnp.concatenate([afft[:, :, :CTX], jnp.full((b, N_EXPERTS, n - CTX), -1.0, F32)], axis=2))
    gpos, gfull = _route(jnp.concatenate(sets, axis=0), b, cap_lat, cap_ctx)

    def run(idx, n_set, cap, tile_off, kind, x_in, fin):
        gp = gpos[idx * b:(idx + 1) * b, :, :n_set]
        gf = gfull[idx * b:(idx + 1) * b, :, :n_set]
        astart, need_tail = _windows(gf, cap)
        xs, ws = _gather(h2u, afft, gp, astart, need_tail, cap, tile_off, eg=4)
        ye = _expert_ffn(xs, ws, w_gate, w_up, w_down, layer)
        return _combine(x_in, jnp.swapaxes(gp, 1, 2), astart, need_tail, ye, modtab, final_g, tile_off, kind, fin)

    if final:
        return run(0, n, cap_lat, 1, 1, xu, True)
    xu = run(0, n, cap_lat, 1, 1, xu, False)
    return run(1, CTX, cap_ctx, 0, 0, xu, False)


def kernel(x, c, ctx, c_ctx, w_mod, b_mod, norm1_g, w_in, na_rpb, diff_lambda, diff_subln_g, swa_sink,
           w_branch_na, w_branch_diff, w_branch_swa, w_out, norm2_g, w_router, w_expert_gate, w_expert_up,
           w_expert_down, final_g):
    b, n, d = x.shape
    depth = w_mod.shape[0]
    assert ctx.shape[1] == CTX and d == D_MODEL and n % (2 * TILE) == 0 and n // TILE >= 3

    cin = jnp.concatenate([c, c_ctx[None], jnp.zeros((8 - b - 1, d), F32)], axis=0)
    mod_all = _modulation(cin, w_mod, b_mod)
    cos_u, sin_u = _rope_tables(n)
    xu = jnp.concatenate([ctx, x], axis=1)
    final_g2 = final_g.reshape(1, d)

    out = None
    for l in range(depth):
        final = l == depth - 1
        lam_init = 0.8 - 0.6 * math.exp(-0.3 * l)
        mod_l = mod_all[l].reshape(8, N_MOD, d)
        modtab = jnp.stack([jnp.broadcast_to(mod_l[b], (b, N_MOD, d)), mod_l[:b]], axis=1)
        lam_p = diff_lambda[l].astype(F32)
        lam = (jnp.exp(jnp.sum(lam_p[0] * lam_p[1])) - jnp.exp(jnp.sum(lam_p[2] * lam_p[3])) + lam_init).reshape(1)

        (q_na, k_na, v_na, q_d, k_d, v_d, q_s, k_s, v_s, gs) = _project(
            xu, modtab, norm1_g[l].reshape(1, d), w_in[l].astype(BF16), cos_u, sin_u)
        o_na = _na_attention(q_na, k_na, v_na, _na_bias_table(na_rpb[l]))
        o_d = _diff_attention(q_d, k_d, v_d, lam, diff_subln_g[l].reshape(1, DIFF_V), lam_init)
        o_s = _swa_attention(q_s, k_s, v_s, swa_sink[l].astype(F32))
        xu, h2u, afft = _merge(
            o_na, o_d, o_s, gs, xu, modtab, w_branch_na[l].astype(BF16), w_branch_diff[l].astype(BF16),
            w_branch_swa[l].astype(BF16), w_out[l].astype(BF16), norm2_g[l].reshape(1, d), w_router[l])
        weights = (w_expert_gate, w_expert_up, w_expert_down)
        res = _moe(xu, h2u, afft, modtab, weights, l, final_g2, final)
        if final:
            out = res
        else:
            xu = res
    return out
```

```python
import functools
import math

import numpy as np
import jax
import jax.numpy as jnp
from jax import lax
from jax.experimental import pallas as pl
from jax.experimental.pallas import tpu as pltpu

F32 = jnp.float32
BF16 = jnp.bfloat16
I32 = jnp.int32
HIGHEST = lax.Precision.HIGHEST

D_MODEL = 1024
CTX = 256
TILE = 256
GRID_W = 64
HEAD_DIM = 64
ROPE_PAIRS = HEAD_DIM // 4
ROPE_BASE = 10000.0
EPS = 1e-6
N_MOD = 6
NA_HEADS = 4
NA_WIN_ROWS = 8
NA_WIN_COLS = 16
NA_TILE_ROWS = TILE // GRID_W
DIFF_HEADS = 4
DIFF_V = 2 * HEAD_DIM
SWA_HEADS = 4
SWA_KV_HEADS = 2
SWA_WINDOW = 128
N_EXPERTS = 16
EC_CAPACITY = 2
NEG = -1e30
LANES = 128
MXU_TILE = 256
LOG2E = math.log2(math.e)
DIFF_MAX_DENOMINATOR = 2.0 ** 64
BF16_ROWS = 16
WINDOW_MAIN_ROWS = 64

C_QNA, C_QD, C_QS = 0, 256, 768
C_KNA, C_KD, C_KS = 1024, 1280, 1792
C_VNA, C_VD, C_VS = 1920, 2176, 2688
C_GATE, C_END = 2816, 5888

VMEM_LIMIT = 56 * 1024 * 1024


def _params(n_axes, vmem=VMEM_LIMIT):
    return pltpu.CompilerParams(dimension_semantics=("arbitrary",) * n_axes, vmem_limit_bytes=vmem)


def _nt_dot(a, b):
    return lax.dot_general(a, b, (((1,), (1,)), ((), ())), preferred_element_type=F32)


def _dot(a, b):
    return jnp.dot(a, b, preferred_element_type=F32)


def _mod_kernel(c_ref, w_ref, b_ref, o_ref):
    c = c_ref[...]
    s = c / (1.0 + jnp.exp(-c))
    o_ref[0] = jnp.dot(s, w_ref[0], preferred_element_type=F32, precision=HIGHEST) + b_ref[0]


def _modulation(cin, w_mod, b_mod):
    depth, d, w = w_mod.shape
    tn = 1024
    return pl.pallas_call(
        _mod_kernel,
        out_shape=jax.ShapeDtypeStruct((depth, 8, w), F32),
        grid=(depth, w // tn),
        in_specs=[pl.BlockSpec((8, d), lambda l, j: (0, 0)),
                  pl.BlockSpec((1, d, tn), lambda l, j: (l, 0, j)),
                  pl.BlockSpec((1, 1, tn), lambda l, j: (l, 0, j))],
        out_specs=pl.BlockSpec((1, 8, tn), lambda l, j: (l, 0, j)),
        compiler_params=_params(2),
        name="modulation",
    )(cin, w_mod, b_mod.reshape(depth, 1, w))


def _row_modulation(mod_ref, rows):
    first = pl.program_id(1) * rows
    is_ctx = first + lax.broadcasted_iota(I32, (rows, 1), 0) < CTX
    return lambda k: jnp.where(is_ctx, mod_ref[0, 0, k:k + 1], mod_ref[0, 1, k:k + 1])


def _proj_kernel(x_ref, mod_ref, g_ref, w_ref, cos_ref, sin_ref,
                 qna_ref, kna_ref, vna_ref, qd_ref, kd_ref, vd_ref, qs_ref, ks_ref, vs_ref, gs_ref):
    x = x_ref[0]
    mod = _row_modulation(mod_ref, x.shape[0])
    y = x * lax.rsqrt(jnp.mean(x * x, axis=-1, keepdims=True) + EPS) * g_ref[...]
    h = (y * (1.0 + mod(1)) + mod(0)).astype(BF16)

    def proj(a, b):
        return _dot(h, w_ref[:, a:b])

    cos = cos_ref[...]
    sin = sin_ref[...]
    lane = lax.broadcasted_iota(I32, cos.shape, 1)
    first_half = (lane % (2 * ROPE_PAIRS)) < ROPE_PAIRS

    def rope(t):
        outs = []
        for j in range(t.shape[1] // LANES):
            c = t[:, j * LANES:(j + 1) * LANES]
            r = jnp.where(first_half, pltpu.roll(c, LANES - ROPE_PAIRS, 1), pltpu.roll(c, ROPE_PAIRS, 1))
            outs.append(c * cos + r * sin)
        return outs[0] if len(outs) == 1 else jnp.concatenate(outs, axis=1)

    scale = HEAD_DIM ** -0.5
    qna_ref[0] = (proj(C_QNA, C_QD) * scale).astype(BF16)
    qd_ref[0] = (rope(proj(C_QD, C_QS)) * (scale * LOG2E)).astype(BF16)
    qs_ref[0] = (rope(proj(C_QS, C_KNA)) * scale).astype(BF16)
    kna_ref[0] = proj(C_KNA, C_KD).astype(BF16)
    kd_ref[0] = rope(proj(C_KD, C_KS)).astype(BF16)
    ks_ref[0] = rope(proj(C_KS, C_VNA)).astype(BF16)
    vna_ref[0] = proj(C_VNA, C_VD).astype(BF16)
    vd = proj(C_VD, C_VS).astype(BF16)
    ones = jnp.ones((vd.shape[0], DIFF_V), BF16)
    vd_ref[0] = jnp.concatenate(
        [blk for hd in range(DIFF_HEADS) for blk in (vd[:, hd * DIFF_V:(hd + 1) * DIFF_V], ones)], axis=1)
    vs_ref[0] = proj(C_VS, C_GATE).astype(BF16)
    gates = proj(C_GATE, C_END)
    gs_ref[0] = (1.0 / (1.0 + jnp.exp(-gates))).astype(BF16)


def _matmul_rows(m):
    return max(t for t in range(LANES, 640 + 1, LANES) if m % t == 0)


def _project(xu, modtab, g, w_in, cos_u, sin_u):
    b, m, d = xu.shape
    tm = _matmul_rows(m)
    widths = (256, 256, 256, 512, 512, 2 * DIFF_HEADS * DIFF_V, 256, 128, 128, C_END - C_GATE)
    row = lambda w: pl.BlockSpec((1, tm, w), lambda bb, i: (bb, i, 0))
    return pl.pallas_call(
        _proj_kernel,
        out_shape=[jax.ShapeDtypeStruct((b, m, w), BF16) for w in widths],
        grid=(b, m // tm),
        in_specs=[row(d),
                  pl.BlockSpec((1, 2, N_MOD, d), lambda bb, i: (bb, 0, 0, 0)),
                  pl.BlockSpec((1, d), lambda bb, i: (0, 0)),
                  pl.BlockSpec((d, C_END), lambda bb, i: (0, 0)),
                  pl.BlockSpec((tm, LANES), lambda bb, i: (i, 0)),
                  pl.BlockSpec((tm, LANES), lambda bb, i: (i, 0))],
        out_specs=[row(w) for w in widths],
        compiler_params=_params(2),
        name="norm_project_rope",
    )(xu, modtab, g, w_in, cos_u, sin_u)


def _na_kernel(q_ref, kc_ref, kp_ref, kcur_ref, kn_ref, vc_ref, vp_ref, vcur_ref, vn_ref, bias_ref, o_ref):
    q = q_ref[0]
    kc, vc = kc_ref[0], vc_ref[0]
    kl = jnp.concatenate([kp_ref[0], kcur_ref[0], kn_ref[0]], axis=0)
    vl = jnp.concatenate([vp_ref[0], vcur_ref[0], vn_ref[0]], axis=0)
    outs = []
    for hd in range(NA_HEADS):
        sl = slice(hd * HEAD_DIM, (hd + 1) * HEAD_DIM)
        qh = q[:, sl]
        s_c = _nt_dot(qh, kc[:, sl])
        s_l = _nt_dot(qh, kl[:, sl]) + bias_ref[0, hd]
        mx = jnp.maximum(jnp.max(s_c, axis=1, keepdims=True), jnp.max(s_l, axis=1, keepdims=True))
        p_c = jnp.exp(s_c - mx)
        p_l = jnp.exp(s_l - mx)
        den = jnp.sum(p_c, axis=1, keepdims=True) + jnp.sum(p_l, axis=1, keepdims=True)
        o = _dot(p_c.astype(BF16), vc[:, sl]) + _dot(p_l.astype(BF16), vl[:, sl])
        outs.append(o / den)
    o_ref[0] = jnp.concatenate(outs, axis=1).astype(BF16)


def _na_bias_table(rpb):
    tr, nk = NA_TILE_ROWS, 3 * NA_TILE_ROWS
    qc = np.arange(GRID_W)[:, None]
    kc = np.arange(GRID_W)[None, :]
    cstart = np.clip(qc - NA_WIN_COLS // 2, 0, GRID_W - NA_WIN_COLS)
    col_ok = (kc >= cstart) & (kc < cstart + NA_WIN_COLS)
    dc = np.clip(kc - qc, -(NA_WIN_COLS - 1), NA_WIN_COLS - 1) + NA_WIN_COLS - 1
    onehot = jnp.asarray(np.arange(2 * NA_WIN_COLS - 1)[:, None, None] == dc[None], F32)
    cols = jnp.einsum("hrc,cqk->hrqk", rpb.astype(F32), onehot, precision=HIGHEST)
    qr = np.arange(tr)[:, None]
    krow = np.arange(nk)[None, :] - tr
    dr = np.clip(krow - qr, -(NA_WIN_ROWS - 1), NA_WIN_ROWS - 1) + NA_WIN_ROWS - 1
    starts = (0 * qr, qr - NA_WIN_ROWS // 2, 0 * qr + tr - NA_WIN_ROWS)
    row_ok = np.stack([(krow >= st) & (krow < st + NA_WIN_ROWS) for st in starts])
    t = jnp.take(cols, jnp.asarray(dr.reshape(-1)), axis=1)
    t = t.reshape(NA_HEADS, tr, nk, GRID_W, GRID_W).transpose(0, 1, 3, 2, 4)
    ok = row_ok[:, None, :, None, :, None] & col_ok[None, None, None, :, None, :]
    t = jnp.where(ok, t[None], NEG).reshape(3, NA_HEADS, TILE, 3 * TILE)
    return jnp.concatenate([jnp.full_like(t[:1], NEG), t], axis=0)


def _na_attention(q, k, v, bias):
    b, m, w = q.shape
    nt = m // TILE
    nb = nt - 1

    def lat(off):
        return lambda bb, i: (bb, jnp.clip(i - 1 + off, 0, nb - 1) + 1, 0)

    def kind(bb, i):
        return (jnp.where(i == 0, 0, jnp.where(i == 1, 1, jnp.where(i == nb, 3, 2))), 0, 0, 0)

    blk = lambda f: pl.BlockSpec((1, TILE, w), f)
    kv_specs = [blk(lambda bb, i: (bb, 0, 0)), blk(lat(-1)), blk(lat(0)), blk(lat(1))]
    return pl.pallas_call(
        _na_kernel,
        out_shape=jax.ShapeDtypeStruct((b, m, w), BF16),
        grid=(b, nt),
        in_specs=[blk(lambda bb, i: (bb, i, 0))] + kv_specs + kv_specs
                 + [pl.BlockSpec((1, NA_HEADS, TILE, 3 * TILE), kind)],
        out_specs=blk(lambda bb, i: (bb, i, 0)),
        compiler_params=_params(2),
        name="neighbourhood_attention",
    )(q, k, k, k, k, v, v, v, v, bias)


def _swa_kernel(sink_ref, q_ref, kc_ref, kp_ref, kcur_ref, kn_ref, vc_ref, vp_ref, vcur_ref, vn_ref, mask_ref,
                o_ref):
    q = q_ref[0]
    kc, vc = kc_ref[0], vc_ref[0]
    kl = jnp.concatenate([kp_ref[0], kcur_ref[0], kn_ref[0]], axis=0)
    vl = jnp.concatenate([vp_ref[0], vcur_ref[0], vn_ref[0]], axis=0)
    group = SWA_HEADS // SWA_KV_HEADS
    rows = group * TILE
    mask = jnp.concatenate([mask_ref[0]] * group, axis=0)
    rid = lax.broadcasted_iota(I32, (rows, 1), 0)
    outs = []
    for g in range(SWA_KV_HEADS):
        ksl = slice(g * HEAD_DIM, (g + 1) * HEAD_DIM)
        qg = jnp.concatenate([q[:, (g * group + j) * HEAD_DIM:(g * group + j + 1) * HEAD_DIM]
                              for j in range(group)], axis=0)
        sink = jnp.zeros((rows, 1), F32)
        for j in range(group):
            sink = jnp.where(rid // TILE == j, sink_ref[g * group + j], sink)
        s_c = _nt_dot(qg, kc[:, ksl])
        s_l = _nt_dot(qg, kl[:, ksl]) + mask
        mx = jnp.maximum(jnp.maximum(jnp.max(s_c, axis=1, keepdims=True), jnp.max(s_l, axis=1, keepdims=True)), sink)
        p_c = jnp.exp(s_c - mx)
        p_l = jnp.exp(s_l - mx)
        den = jnp.sum(p_c, axis=1, keepdims=True) + jnp.sum(p_l, axis=1, keepdims=True) + jnp.exp(sink - mx)
        o = (_dot(p_c.astype(BF16), vc[:, ksl]) + _dot(p_l.astype(BF16), vl[:, ksl])) / den
        outs.extend(o[j * TILE:(j + 1) * TILE] for j in range(group))
    o_ref[0] = jnp.concatenate(outs, axis=1).astype(BF16)


def _swa_mask_table():
    qpos = np.arange(TILE)[:, None]
    kpos = np.arange(3 * TILE)[None, :] - TILE
    near = np.abs(qpos - kpos) <= SWA_WINDOW
    kinds = [np.zeros_like(near), near & (kpos >= 0), near, near & (kpos < TILE)]
    return jnp.asarray(np.where(np.stack(kinds), 0.0, NEG), F32)


def _tile_kind(i, nb):
    return jnp.where(i == 0, 0, jnp.where(i == 1, 1, jnp.where(i == nb, 3, 2)))


def _swa_attention(q, k, v, sink):
    b, m, wq = q.shape
    wk = k.shape[2]
    nt = m // TILE
    nb = nt - 1

    def lat(off):
        return lambda bb, i: (bb, jnp.clip(i - 1 + off, 0, nb - 1) + 1, 0)

    blk = lambda w, f: pl.BlockSpec((1, TILE, w), f)
    kv_specs = [blk(wk, lambda bb, i: (bb, 0, 0)), blk(wk, lat(-1)), blk(wk, lat(0)), blk(wk, lat(1))]
    return pl.pallas_call(
        _swa_kernel,
        out_shape=jax.ShapeDtypeStruct((b, m, wq), BF16),
        grid=(b, nt),
        in_specs=[pl.BlockSpec(memory_space=pltpu.SMEM), blk(wq, lambda bb, i: (bb, i, 0))] + kv_specs + kv_specs
                 + [pl.BlockSpec((1, TILE, 3 * TILE), lambda bb, i: (_tile_kind(i, nb), 0, 0))],
        out_specs=blk(wq, lambda bb, i: (bb, i, 0)),
        compiler_params=_params(2),
        name="windowed_attention",
    )(sink, q, k, k, k, k, v, v, v, v, _swa_mask_table())


def _diff_kernel(lam_ref, q_ref, k_ref, v_ref, g_ref, o_ref, acc_ref, m_ref, s_ref, cmax_ref,
                 *, n_chunks, kblk, fast_kblk, lam_init):
    i = pl.program_id(2)
    q = q_ref[0]
    lane = lax.broadcasted_iota(I32, q.shape, 1)
    zero = jnp.zeros_like(q)
    qq = jnp.concatenate([jnp.where(lane < HEAD_DIM, q, zero), jnp.where(lane >= HEAD_DIM, q, zero)], axis=0)

    def chunk(c):
        return pl.ds(pl.multiple_of(c * kblk, LANES), kblk)

    def scores(c, slot):
        s = _nt_dot(qq, k_ref[0, chunk(c), :])
        s_ref[slot] = s
        cmax_ref[slot] = jnp.max(s, axis=1, keepdims=True)

    def accumulate(s, cmax, v):
        m_prev = m_ref[...]
        m_new = jnp.maximum(m_prev, cmax)
        p = jnp.exp2(s - m_new).astype(BF16)
        acc_ref[...] = jnp.exp2(m_prev - m_new) * acc_ref[...] + _dot(p, v)
        m_ref[...] = m_new

    def reset():
        m_ref[...] = jnp.full(m_ref.shape, NEG, F32)
        acc_ref[...] = jnp.zeros(acc_ref.shape, F32)

    def safe_sweep():
        reset()
        scores(0, 0)

        def pair(c, last):
            for cur in (0, 1):
                if not (last and cur == 1):
                    scores(c + cur + 1, 1 - cur)
                accumulate(s_ref[cur], cmax_ref[cur], v_ref[0, chunk(c + cur), :])

        def body(c2, carry):
            pair(2 * c2, False)
            return carry

        lax.fori_loop(0, n_chunks // 2 - 1, body, 0)
        pair(n_chunks - 2, True)

    def fast_sweep():
        keys = lambda c: slice(c * fast_kblk, (c + 1) * fast_kblk)
        s = _nt_dot(qq, k_ref[0, keys(0), :])
        m0 = jnp.max(s, axis=1, keepdims=True)
        acc = _dot(jnp.exp2(s - m0).astype(BF16), v_ref[0, keys(0), :])
        for c in range(1, n_chunks * kblk // fast_kblk):
            s = _nt_dot(qq, k_ref[0, keys(c), :])
            acc = acc + _dot(jnp.exp2(s - m0).astype(BF16), v_ref[0, keys(c), :])
        acc_ref[...] = acc

    @pl.when(i == 0)
    def _():
        reset()
        s = _nt_dot(qq, k_ref[0, 0:CTX, :])
        accumulate(s, jnp.max(s, axis=1, keepdims=True), v_ref[0, 0:CTX, :])

    @pl.when(i > 0)
    def _():
        fast_sweep()

        @pl.when(jnp.logical_not(jnp.max(acc_ref[:, DIFF_V:]) < DIFF_MAX_DENOMINATOR))
        def _():
            safe_sweep()

    acc = acc_ref[...]
    o = acc[:, :DIFF_V] / acc[:, DIFF_V:]
    d = o[:TILE] - lam_ref[0] * o[TILE:]
    y = d * lax.rsqrt(jnp.mean(d * d, axis=-1, keepdims=True) + EPS) * g_ref[...]
    o_ref[0] = (y * (1.0 - lam_init)).astype(BF16)


def _diff_chunking(m, max_chunk=2048):
    for n_chunks in range(2, m // LANES + 1, 2):
        if m % (n_chunks * LANES) == 0 and m // n_chunks <= max_chunk:
            return n_chunks, m // n_chunks
    raise ValueError(f"no chunking for {m} keys")


def _diff_attention(q, k, v, lam, g, lam_init):
    b, m, _ = q.shape
    nt = m // TILE
    n_chunks, kblk = _diff_chunking(m)
    fast_kblk = max(k for k in range(MXU_TILE, 2048 + 1, MXU_TILE) if m % k == 0)
    return pl.pallas_call(
        functools.partial(_diff_kernel, n_chunks=n_chunks, kblk=kblk, fast_kblk=fast_kblk, lam_init=lam_init),
        out_shape=jax.ShapeDtypeStruct((b, m, DIFF_HEADS * DIFF_V), BF16),
        grid=(b, DIFF_HEADS, nt),
        in_specs=[pl.BlockSpec(memory_space=pltpu.SMEM),
                  pl.BlockSpec((1, TILE, 2 * HEAD_DIM), lambda bb, hh, i: (bb, i, hh)),
                  pl.BlockSpec((1, m, 2 * HEAD_DIM), lambda bb, hh, i: (bb, 0, hh)),
                  pl.BlockSpec((1, m, 2 * DIFF_V), lambda bb, hh, i: (bb, 0, hh)),
                  pl.BlockSpec((1, DIFF_V), lambda bb, hh, i: (0, 0))],
        out_specs=pl.BlockSpec((1, TILE, DIFF_V), lambda bb, hh, i: (bb, i, hh)),
        scratch_shapes=[pltpu.VMEM((2 * TILE, 2 * DIFF_V), F32), pltpu.VMEM((2 * TILE, 1), F32),
                        pltpu.VMEM((2, 2 * TILE, kblk), F32), pltpu.VMEM((2, 2 * TILE, 1), F32)],
        compiler_params=_params(3),
        name="differential_attention",
    )(lam, q, k, v, g)


def _merge_kernel(ona_ref, od_ref, osw_ref, gs_ref, x_ref, mod_ref, wna_ref, wd_ref, wsw_ref, wo_ref,
                  g2_ref, wrt_ref, xo_ref, h2_ref, afft_ref):
    d = x_ref.shape[2]
    gs = gs_ref[0]
    y = (gs[:, :d].astype(F32) * _dot(ona_ref[0], wna_ref[...])
         + gs[:, d:2 * d].astype(F32) * _dot(od_ref[0], wd_ref[...])
         + gs[:, 2 * d:].astype(F32) * _dot(osw_ref[0], wsw_ref[...]))
    mod = _row_modulation(mod_ref, x_ref.shape[1])
    xn = x_ref[0] + mod(2) * _dot(y.astype(BF16), wo_ref[...])
    xo_ref[0] = xn
    r = xn * lax.rsqrt(jnp.mean(xn * xn, axis=-1, keepdims=True) + EPS) * g2_ref[...]
    h2 = r * (1.0 + mod(4)) + mod(3)
    h2_ref[0] = h2.astype(BF16)
    lt = lax.dot_general(wrt_ref[...], h2, (((1,), (1,)), ((), ())),
                         preferred_element_type=F32, precision=HIGHEST)
    et = jnp.exp(lt - jnp.max(lt, axis=0, keepdims=True))
    afft_ref[0] = et / jnp.sum(et, axis=0, keepdims=True)


def _merge(o_na, o_d, o_sw, gs, xu, modtab, w_na, w_d, w_sw, w_o, g2, w_r):
    b, m, d = xu.shape
    tm = _matmul_rows(m)
    ne = w_r.shape[1]
    row = lambda w: pl.BlockSpec((1, tm, w), lambda bb, i: (bb, i, 0))
    full = lambda a: pl.BlockSpec(a.shape, lambda bb, i: (0,) * a.ndim)
    w_rt = w_r.T
    return pl.pallas_call(
        _merge_kernel,
        out_shape=[jax.ShapeDtypeStruct((b, m, d), F32), jax.ShapeDtypeStruct((b, m, d), BF16),
                   jax.ShapeDtypeStruct((b, ne, m), F32)],
        grid=(b, m // tm),
        in_specs=[row(o_na.shape[2]), row(o_d.shape[2]), row(o_sw.shape[2]), row(gs.shape[2]), row(d),
                  pl.BlockSpec((1, 2, N_MOD, d), lambda bb, i: (bb, 0, 0, 0)),
                  full(w_na), full(w_d), full(w_sw), full(w_o), full(g2), full(w_rt)],
        out_specs=[row(d), row(d), pl.BlockSpec((1, ne, tm), lambda bb, i: (bb, 0, i))],
        compiler_params=_params(2),
        name="merge_residual_router",
    )(o_na, o_d, o_sw, gs, xu, modtab, w_na, w_d, w_sw, w_o, g2, w_rt)


def _route_kernel(a_ref, gpos_ref, gfull_ref, *, n_latent_sets, cap_lat, cap_ctx):
    ne, c, w = a_ref.shape[1:]
    a = a_ref[0]
    bits = lax.bitcast_convert_type(a, I32)
    cap = jnp.where(pl.program_id(0) < n_latent_sets, cap_lat, cap_ctx).astype(F32)

    def count(mask):
        return jnp.sum(jnp.sum(mask.astype(F32), axis=2, keepdims=True), axis=1, keepdims=True)

    def search(it, thr):
        cand = thr | jnp.left_shift(jnp.int32(1), 30 - it)
        return jnp.where(count(bits >= cand) >= cap, cand, thr)

    thr = lax.fori_loop(0, 31, search, jnp.zeros((ne, 1, 1), I32))
    gt = bits > thr
    eq = bits == thr
    need = cap - count(gt)

    upper = (lax.broadcasted_iota(I32, (w, w), 0) <= lax.broadcasted_iota(I32, (w, w), 1)).astype(BF16)
    lower = (lax.broadcasted_iota(I32, (c, c), 1) < lax.broadcasted_iota(I32, (c, c), 0)).astype(BF16)

    def exclusive_cumsum(mask):
        x = mask.astype(F32).reshape(ne * c, w)
        within = _dot(x.astype(BF16), upper)
        tot = jnp.broadcast_to(within[:, w - 1:w], (ne * c, w)).astype(BF16)
        before = jnp.concatenate([_dot(lower, tot[e * c:(e + 1) * c]) for e in range(ne)], axis=0)
        return (before + within - x).reshape(ne, c, w)

    sel = gt | (eq & (exclusive_cumsum(eq) < need))
    g = exclusive_cumsum(sel).astype(I32)
    gfull_ref[0] = g
    gpos_ref[0] = jnp.where(sel, g, -1)


def _route(aff_sets, n_latent_sets, cap_lat, cap_ctx):
    s, ne, n = aff_sets.shape
    c = n // LANES
    a4 = aff_sets.reshape(s, ne, c, LANES)
    blk = pl.BlockSpec((1, ne, c, LANES), lambda i: (i, 0, 0, 0))
    gpos, gfull = pl.pallas_call(
        functools.partial(_route_kernel, n_latent_sets=n_latent_sets, cap_lat=cap_lat, cap_ctx=cap_ctx),
        out_shape=[jax.ShapeDtypeStruct(a4.shape, I32)] * 2,
        grid=(s,),
        in_specs=[blk],
        out_specs=[blk, blk],
        compiler_params=_params(1),
        name="expert_choice_select",
    )(a4)
    return gpos.reshape(s, ne, n), gfull.reshape(s, ne, n)


def _window(cap):
    w = min(TILE + BF16_ROWS, cap)
    main = min(WINDOW_MAIN_ROWS, w)
    return main, w - main, cap - w


def _windows(gfull, cap):
    b, ne, _ = gfull.shape
    main, _, max_start = _window(cap)
    seg_start = gfull[:, :, ::TILE]
    seg_end = jnp.concatenate([gfull[:, :, TILE::TILE], jnp.full((b, ne, 1), cap, I32)], axis=2)
    astart = jnp.minimum(seg_start // BF16_ROWS * BF16_ROWS, max_start)
    return astart.reshape(-1), (seg_end > astart + main).astype(I32).reshape(-1)


def _gather_kernel(a_ref, t_ref, g_ref, aff_ref, h_ref, o_ref, w_ref, *, ne, nblk, eg, main, tail):
    b, egi, j = pl.program_id(0), pl.program_id(1), pl.program_id(2)

    @pl.when(j == 0)
    def _():
        o_ref[...] = jnp.zeros(o_ref.shape, o_ref.dtype)
        w_ref[...] = jnp.zeros(w_ref.shape, w_ref.dtype)

    hb = h_ref[0]
    t = hb.shape[0]

    def place(k, pos, first, rows):
        hit = lax.broadcasted_iota(I32, (rows, t), 0) == pos - first
        picked = _dot(jnp.where(hit, 1.0, 0.0).astype(BF16), hb).astype(BF16)
        weight = jnp.sum(jnp.where(hit, aff_ref[0, k], 0.0), axis=1, keepdims=True)
        sl = (0, k, pl.ds(pl.multiple_of(a_ref[idx(k)] + first, BF16_ROWS), rows), slice(None))
        o_ref[sl] = o_ref[sl] + picked
        w_ref[sl] = w_ref[sl] + weight

    def idx(k):
        return (b * ne + egi * eg + k) * nblk + j

    def window_row(k):
        return g_ref[0, k] - a_ref[idx(k)]

    for k in range(eg):
        place(k, window_row(k), 0, main)
    for k in range(eg if tail else 0):
        @pl.when(t_ref[idx(k)] != 0)
        def _(k=k):
            place(k, window_row(k), main, tail)


def _gather(h2u, afft, gpos, astart, need_tail, cap, tile_off, eg):
    b, ne, n = gpos.shape
    m, d = h2u.shape[1:]
    nblk = n // TILE
    main, tail, _ = _window(cap)
    lists = lambda w: pl.BlockSpec((1, eg, cap, w), lambda bb, e, j, a, nt: (bb, e, 0, 0))
    return pl.pallas_call(
        functools.partial(_gather_kernel, ne=ne, nblk=nblk, eg=eg, main=main, tail=tail),
        out_shape=[jax.ShapeDtypeStruct((b, ne, cap, d), BF16), jax.ShapeDtypeStruct((b, ne, cap, 1), F32)],
        grid_spec=pltpu.PrefetchScalarGridSpec(
            num_scalar_prefetch=2,
            grid=(b, ne // eg, nblk),
            in_specs=[pl.BlockSpec((1, eg, 1, TILE), lambda bb, e, j, a, nt: (bb, e, 0, j)),
                      pl.BlockSpec((1, eg, 1, TILE), lambda bb, e, j, a, nt: (bb, e, 0, tile_off + j)),
                      pl.BlockSpec((1, TILE, d), lambda bb, e, j, a, nt: (bb, tile_off + j, 0))],
            out_specs=[lists(d), lists(1)]),
        compiler_params=_params(3),
        name="expert_gather",
    )(astart, need_tail, gpos.reshape(b, ne, 1, n), afft.reshape(b, ne, 1, m), h2u)


def _ffn_kernel(x_ref, rw_ref, wg_ref, wu_ref, wd_ref, o_ref, wg_s, wu_s, wd_s):
    @pl.when((pl.program_id(1) == 0) & (pl.program_id(2) == 0))
    def _():
        wg_s[...] = wg_ref[0, 0].astype(BF16)
        wu_s[...] = wu_ref[0, 0].astype(BF16)
        wd_s[...] = wd_ref[0, 0].astype(BF16)

    x = x_ref[0, 0]
    gate = _dot(x, wg_s[...])
    up = _dot(x, wu_s[...])
    hid = (gate / (1.0 + jnp.exp(-gate)) * up).astype(BF16)
    o_ref[0, 0] = (_dot(hid, wd_s[...]) * rw_ref[0, 0]).astype(BF16)


def _expert_ffn(xs, ws, w_gate, w_up, w_down, layer):
    b, ne, cap, d = xs.shape
    tr = min(cap, 512)
    wspec = lambda w: pl.BlockSpec((1, 1) + w.shape[2:], lambda e, bb, r: (layer, e, 0, 0))
    rows = lambda w: pl.BlockSpec((1, 1, tr, w), lambda e, bb, r: (bb, e, r, 0))
    return pl.pallas_call(
        _ffn_kernel,
        out_shape=jax.ShapeDtypeStruct(xs.shape, BF16),
        grid=(ne, b, cap // tr),
        in_specs=[rows(d), rows(1), wspec(w_gate), wspec(w_up), wspec(w_down)],
        out_specs=rows(d),
        scratch_shapes=[pltpu.VMEM(w.shape[2:], BF16) for w in (w_gate, w_up, w_down)],
        compiler_params=_params(3),
        name="expert_swiglu",
    )(xs, ws, w_gate, w_up, w_down)


def _combine_kernel(a_ref, t_ref, x_ref, g_ref, mod_ref, fg_ref, ye_ref, o_ref,
                    win_ref, tail_ref, tacc_ref, sem_ref, tsem_ref, *, ne, nblk, main, tail, final):
    b, j = pl.program_id(0), pl.program_id(1)
    step = b * nblk + j
    slot = step % 2

    def start_at(bb, jj, e):
        return pl.multiple_of(a_ref[(bb * ne + e) * nblk + jj], BF16_ROWS)

    def start_of(e):
        return start_at(b, j, e)

    def window_copy(bb, jj, e, sl):
        return pltpu.make_async_copy(ye_ref.at[bb, e, pl.ds(start_at(bb, jj, e), main)],
                                     win_ref.at[sl, e], sem_ref.at[sl, e])

    @pl.when(step == 0)
    def _():
        for e in range(ne):
            window_copy(b, j, e, slot).start()

    @pl.when(step + 1 < pl.num_programs(0) * nblk)
    def _():
        nxt = step + 1
        for e in range(ne):
            window_copy(nxt // nblk, nxt % nblk, e, 1 - slot).start()

    t = x_ref.shape[1]

    def pick(e, first, rows, window):
        hit = lax.broadcasted_iota(I32, (rows, t), 0) == g_ref[0, e] - (start_of(e) + first)
        return lax.dot_general(jnp.where(hit, 1.0, 0.0).astype(BF16), window, (((0,), (0,)), ((), ())),
                               preferred_element_type=F32)

    if tail:
        tacc_ref[...] = jnp.zeros(tacc_ref.shape, F32)
        for e in range(ne):
            @pl.when(t_ref[(b * ne + e) * nblk + j] != 0)
            def _(e=e):
                cp = pltpu.make_async_copy(ye_ref.at[b, e, pl.ds(start_of(e) + main, tail)], tail_ref, tsem_ref.at[0])
                cp.start()
                cp.wait()
                tacc_ref[...] += pick(e, main, tail, tail_ref[...])

    acc = tacc_ref[...] if tail else jnp.zeros(o_ref.shape[1:], F32)
    for e in range(ne):
        window_copy(b, j, e, slot).wait()
        acc = acc + pick(e, 0, main, win_ref[slot, e])
    x = x_ref[0] + mod_ref[0, 0][5:6] * acc
    if final:
        x = x * lax.rsqrt(jnp.mean(x * x, axis=-1, keepdims=True) + EPS) * fg_ref[...]
    o_ref[0] = x


def _combine(xu, gpos, astart, need_tail, ye, modtab, final_g, tile_off, kind, final):
    b, m, d = xu.shape
    _, ne, n = gpos.shape
    cap = ye.shape[2]
    nblk = n // TILE
    main, tail, _ = _window(cap)
    out_spec = pl.BlockSpec((1, TILE, d), (lambda bb, j, a, nt: (bb, j, 0)) if final
                            else (lambda bb, j, a, nt: (bb, tile_off + j, 0)))
    return pl.pallas_call(
        functools.partial(_combine_kernel, ne=ne, nblk=nblk, main=main, tail=tail, final=final),
        out_shape=jax.ShapeDtypeStruct((b, n, d) if final else xu.shape, F32),
        grid_spec=pltpu.PrefetchScalarGridSpec(
            num_scalar_prefetch=2,
            grid=(b, nblk),
            in_specs=[pl.BlockSpec((1, TILE, d), lambda bb, j, a, nt: (bb, tile_off + j, 0)),
                      pl.BlockSpec((1, ne, 1, TILE), lambda bb, j, a, nt: (bb, 0, 0, j)),
                      pl.BlockSpec((1, 1, N_MOD, d), lambda bb, j, a, nt: (bb, kind, 0, 0)),
                      pl.BlockSpec((1, d), lambda bb, j, a, nt: (0, 0)),
                      pl.BlockSpec(memory_space=pl.ANY)],
            out_specs=out_spec,
            scratch_shapes=[pltpu.VMEM((2, ne, main, d), BF16), pltpu.VMEM((max(tail, BF16_ROWS), d), BF16),
                            pltpu.VMEM((TILE, d), F32), pltpu.SemaphoreType.DMA((2, ne)),
                            pltpu.SemaphoreType.DMA((1,))]),
        input_output_aliases={} if final else {2: 0},
        compiler_params=_params(2),
        name="expert_combine",
    )(astart, need_tail, xu, gpos.reshape(b, ne, 1, n), modtab, final_g, ye)


def _rope_tables(n):
    t = np.arange(n)
    row = (t // GRID_W).astype(np.float32)[:, None]
    col = (t % GRID_W).astype(np.float32)[:, None]
    inv = (ROPE_BASE ** (-np.arange(ROPE_PAIRS, dtype=np.float32) / ROPE_PAIRS)).astype(np.float32)
    ang = np.concatenate([row * inv, row * inv, col * inv, col * inv], axis=-1)
    cos, sin = np.cos(ang), np.sin(ang)
    half = (np.arange(HEAD_DIM) % (2 * ROPE_PAIRS)) < ROPE_PAIRS
    sin = np.where(half[None], -sin, sin)
    cos = np.concatenate([np.ones((CTX, HEAD_DIM)), cos], axis=0)
    sin = np.concatenate([np.zeros((CTX, HEAD_DIM)), sin], axis=0)
    tile2 = lambda a: jnp.asarray(np.concatenate([a, a], axis=1), F32)
    return tile2(cos), tile2(sin)


def _moe(xu, h2u, afft, modtab, weights, layer, final_g, final):
    b, m, d = xu.shape
    n = m - CTX
    w_gate, w_up, w_down = weights
    cap_lat = max(1, EC_CAPACITY * n // N_EXPERTS)
    cap_ctx = max(1, EC_CAPACITY * CTX // N_EXPERTS)
    sets = [afft[:, :, CTX:]]
    if not final:
        sets.append(jnp.concatenate([afft[:, :, :CTX], jnp.full((b, N_EXPERTS, n - CTX), -1.0, F32)], axis=2))
    gpos, gfull = _route(jnp.concatenate(sets, axis=0), b, cap_lat, cap_ctx)

    def run(idx, n_set, cap, tile_off, kind, x_in, fin):
        gp = gpos[idx * b:(idx + 1) * b, :, :n_set]
        gf = gfull[idx * b:(idx + 1) * b, :, :n_set]
        astart, need_tail = _windows(gf, cap)
        xs, ws = _gather(h2u, afft, gp, astart, need_tail, cap, tile_off, eg=4)
        ye = _expert_ffn(xs, ws, w_gate, w_up, w_down, layer)
        return _combine(x_in, gp, astart, need_tail, ye, modtab, final_g, tile_off, kind, fin)

    if final:
        return run(0, n, cap_lat, 1, 1, xu, True)
    xu = run(0, n, cap_lat, 1, 1, xu, False)
    return run(1, CTX, cap_ctx, 0, 0, xu, False)


def kernel(x, c, ctx, c_ctx, w_mod, b_mod, norm1_g, w_in, na_rpb, diff_lambda, diff_subln_g, swa_sink,
           w_branch_na, w_branch_diff, w_branch_swa, w_out, norm2_g, w_router, w_expert_gate, w_expert_up,
           w_expert_down, final_g):
    b, n, d = x.shape
    depth = w_mod.shape[0]
    assert ctx.shape[1] == CTX and d == D_MODEL and n % (2 * TILE) == 0 and n // TILE >= 3

    cin = jnp.concatenate([c, c_ctx[None], jnp.zeros((8 - b - 1, d), F32)], axis=0)
    mod_all = _modulation(cin, w_mod, b_mod)
    cos_u, sin_u = _rope_tables(n)
    xu = jnp.concatenate([ctx, x], axis=1)
    final_g2 = final_g.reshape(1, d)

    out = None
    for l in range(depth):
        final = l == depth - 1
        lam_init = 0.8 - 0.6 * math.exp(-0.3 * l)
        mod_l = mod_all[l].reshape(8, N_MOD, d)
        modtab = jnp.stack([jnp.broadcast_to(mod_l[b], (b, N_MOD, d)), mod_l[:b]], axis=1)
        lam_p = diff_lambda[l].astype(F32)
        lam = (jnp.exp(jnp.sum(lam_p[0] * lam_p[1])) - jnp.exp(jnp.sum(lam_p[2] * lam_p[3])) + lam_init).reshape(1)

        (q_na, k_na, v_na, q_d, k_d, v_d, q_s, k_s, v_s, gs) = _project(
            xu, modtab, norm1_g[l].reshape(1, d), w_in[l].astype(BF16), cos_u, sin_u)
        o_na = _na_attention(q_na, k_na, v_na, _na_bias_table(na_rpb[l]))
        o_d = _diff_attention(q_d, k_d, v_d, lam, diff_subln_g[l].reshape(1, DIFF_V), lam_init)
        o_s = _swa_attention(q_s, k_s, v_s, swa_sink[l].astype(F32))
        xu, h2u, afft = _merge(
            o_na, o_d, o_s, gs, xu, modtab, w_branch_na[l].astype(BF16), w_branch_diff[l].astype(BF16),
            w_branch_swa[l].astype(BF16), w_out[l].astype(BF16), norm2_g[l].reshape(1, d), w_router[l])
        weights = (w_expert_gate, w_expert_up, w_expert_down)
        res = _moe(xu, h2u, afft, modtab, weights, l, final_g2, final)
        if final:
            out = res
        else:
            xu = res
    return out
```

```python
import functools
import math

import numpy as np
import jax
import jax.numpy as jnp
from jax import lax
from jax.experimental import pallas as pl
from jax.experimental.pallas import tpu as pltpu

F32 = jnp.float32
BF16 = jnp.bfloat16
I32 = jnp.int32
HIGHEST = lax.Precision.HIGHEST

D_MODEL = 1024
CTX = 256
TILE = 256
GRID_W = 64
HEAD_DIM = 64
ROPE_PAIRS = HEAD_DIM // 4
ROPE_BASE = 10000.0
EPS = 1e-6
N_MOD = 6
NA_HEADS = 4
NA_WIN_ROWS = 8
NA_WIN_COLS = 16
NA_TILE_ROWS = TILE // GRID_W
DIFF_HEADS = 4
DIFF_V = 2 * HEAD_DIM
SWA_HEADS = 4
SWA_KV_HEADS = 2
SWA_WINDOW = 128
N_EXPERTS = 16
EC_CAPACITY = 2
NEG = -1e30
LANES = 128
MXU_TILE = 256
LOG2E = math.log2(math.e)
DIFF_MAX_DENOMINATOR = 2.0 ** 64
BF16_ROWS = 16
WINDOW_MAIN_ROWS = 64

C_QNA, C_QD, C_QS = 0, 256, 768
C_KNA, C_KD, C_KS = 1024, 1280, 1792
C_VNA, C_VD, C_VS = 1920, 2176, 2688
C_GATE, C_END = 2816, 5888

VMEM_LIMIT = 56 * 1024 * 1024


def _params(n_axes, vmem=VMEM_LIMIT):
    return pltpu.CompilerParams(dimension_semantics=("arbitrary",) * n_axes, vmem_limit_bytes=vmem)


def _nt_dot(a, b):
    return lax.dot_general(a, b, (((1,), (1,)), ((), ())), preferred_element_type=F32)


def _dot(a, b):
    return jnp.dot(a, b, preferred_element_type=F32)


def _mod_kernel(c_ref, w_ref, b_ref, o_ref):
    c = c_ref[...]
    s = c / (1.0 + jnp.exp(-c))
    o_ref[0] = jnp.dot(s, w_ref[0], preferred_element_type=F32, precision=HIGHEST) + b_ref[0]


def _modulation(cin, w_mod, b_mod):
    depth, d, w = w_mod.shape
    tn = 1024
    return pl.pallas_call(
        _mod_kernel,
        out_shape=jax.ShapeDtypeStruct((depth, 8, w), F32),
        grid=(depth, w // tn),
        in_specs=[pl.BlockSpec((8, d), lambda l, j: (0, 0)),
                  pl.BlockSpec((1, d, tn), lambda l, j: (l, 0, j)),
                  pl.BlockSpec((1, 1, tn), lambda l, j: (l, 0, j))],
        out_specs=pl.BlockSpec((1, 8, tn), lambda l, j: (l, 0, j)),
        compiler_params=_params(2),
        name="modulation",
    )(cin, w_mod, b_mod.reshape(depth, 1, w))


def _row_modulation(mod_ref, rows):
    first = pl.program_id(1) * rows
    is_ctx = first + lax.broadcasted_iota(I32, (rows, 1), 0) < CTX
    return lambda k: jnp.where(is_ctx, mod_ref[0, 0, k:k + 1], mod_ref[0, 1, k:k + 1])


def _proj_kernel(x_ref, mod_ref, g_ref, w_ref, cos_ref, sin_ref,
                 qna_ref, kna_ref, vna_ref, qd_ref, kd_ref, vd_ref, qs_ref, ks_ref, vs_ref, gs_ref):
    x = x_ref[0]
    mod = _row_modulation(mod_ref, x.shape[0])
    y = x * lax.rsqrt(jnp.mean(x * x, axis=-1, keepdims=True) + EPS) * g_ref[...]
    h = (y * (1.0 + mod(1)) + mod(0)).astype(BF16)

    def proj(a, b):
        return _dot(h, w_ref[:, a:b])

    cos = cos_ref[...]
    sin = sin_ref[...]
    lane = lax.broadcasted_iota(I32, cos.shape, 1)
    first_half = (lane % (2 * ROPE_PAIRS)) < ROPE_PAIRS

    def rope(t):
        outs = []
        for j in range(t.shape[1] // LANES):
            c = t[:, j * LANES:(j + 1) * LANES]
            r = jnp.where(first_half, pltpu.roll(c, LANES - ROPE_PAIRS, 1), pltpu.roll(c, ROPE_PAIRS, 1))
            outs.append(c * cos + r * sin)
        return outs[0] if len(outs) == 1 else jnp.concatenate(outs, axis=1)

    scale = HEAD_DIM ** -0.5
    qna_ref[0] = (proj(C_QNA, C_QD) * scale).astype(BF16)
    qd_ref[0] = (rope(proj(C_QD, C_QS)) * (scale * LOG2E)).astype(BF16)
    qs_ref[0] = (rope(proj(C_QS, C_KNA)) * scale).astype(BF16)
    kna_ref[0] = proj(C_KNA, C_KD).astype(BF16)
    kd_ref[0] = rope(proj(C_KD, C_KS)).astype(BF16)
    ks_ref[0] = rope(proj(C_KS, C_VNA)).astype(BF16)
    vna_ref[0] = proj(C_VNA, C_VD).astype(BF16)
    vd = proj(C_VD, C_VS).astype(BF16)
    ones = jnp.ones((vd.shape[0], DIFF_V), BF16)
    vd_ref[0] = jnp.concatenate(
        [blk for hd in range(DIFF_HEADS) for blk in (vd[:, hd * DIFF_V:(hd + 1) * DIFF_V], ones)], axis=1)
    vs_ref[0] = proj(C_VS, C_GATE).astype(BF16)
    gates = proj(C_GATE, C_END)
    gs_ref[0] = (1.0 / (1.0 + jnp.exp(-gates))).astype(BF16)


def _matmul_rows(m, limit):
    return max(t for t in range(LANES, limit + 1, LANES) if m % t == 0)


def _project(xu, modtab, g, w_in, cos_u, sin_u):
    b, m, d = xu.shape
    tm = _matmul_rows(m, 256)
    widths =(256, 256, 256, 512, 512, 2 * DIFF_HEADS * DIFF_V, 256, 128, 128, C_END - C_GATE)
    row = lambda w: pl.BlockSpec((1, tm, w), lambda bb, i: (bb, i, 0))
    return pl.pallas_call(
        _proj_kernel,
        out_shape=[jax.ShapeDtypeStruct((b, m, w), BF16) for w in widths],
        grid=(b, m // tm),
        in_specs=[row(d),
                  pl.BlockSpec((1, 2, N_MOD, d), lambda bb, i: (bb, 0, 0, 0)),
                  pl.BlockSpec((1, d), lambda bb, i: (0, 0)),
                  pl.BlockSpec((d, C_END), lambda bb, i: (0, 0)),
                  pl.BlockSpec((tm, LANES), lambda bb, i: (i, 0)),
                  pl.BlockSpec((tm, LANES), lambda bb, i: (i, 0))],
        out_specs=[row(w) for w in widths],
        compiler_params=_params(2),
        name="norm_project_rope",
    )(xu, modtab, g, w_in, cos_u, sin_u)


def _na_kernel(q_ref, kc_ref, kp_ref, kcur_ref, kn_ref, vc_ref, vp_ref, vcur_ref, vn_ref, bias_ref, o_ref):
    q = q_ref[0]
    kc, vc = kc_ref[0], vc_ref[0]
    kl = jnp.concatenate([kp_ref[0], kcur_ref[0], kn_ref[0]], axis=0)
    vl = jnp.concatenate([vp_ref[0], vcur_ref[0], vn_ref[0]], axis=0)
    outs = []
    for hd in range(NA_HEADS):
        sl = slice(hd * HEAD_DIM, (hd + 1) * HEAD_DIM)
        qh = q[:, sl]
        s_c = _nt_dot(qh, kc[:, sl])
        s_l = _nt_dot(qh, kl[:, sl]) + bias_ref[0, hd]
        mx = jnp.maximum(jnp.max(s_c, axis=1, keepdims=True), jnp.max(s_l, axis=1, keepdims=True))
        p_c = jnp.exp(s_c - mx)
        p_l = jnp.exp(s_l - mx)
        den = jnp.sum(p_c, axis=1, keepdims=True) + jnp.sum(p_l, axis=1, keepdims=True)
        o = _dot(p_c.astype(BF16), vc[:, sl]) + _dot(p_l.astype(BF16), vl[:, sl])
        outs.append(o / den)
    o_ref[0] = jnp.concatenate(outs, axis=1).astype(BF16)


def _na_bias_table(rpb):
    tr, nk = NA_TILE_ROWS, 3 * NA_TILE_ROWS
    qc = np.arange(GRID_W)[:, None]
    kc = np.arange(GRID_W)[None, :]
    cstart = np.clip(qc - NA_WIN_COLS // 2, 0, GRID_W - NA_WIN_COLS)
    col_ok = (kc >= cstart) & (kc < cstart + NA_WIN_COLS)
    dc = np.clip(kc - qc, -(NA_WIN_COLS - 1), NA_WIN_COLS - 1) + NA_WIN_COLS - 1
    onehot = jnp.asarray(np.arange(2 * NA_WIN_COLS - 1)[:, None, None] == dc[None], F32)
    cols = jnp.einsum("hrc,cqk->hrqk", rpb.astype(F32), onehot, precision=HIGHEST)
    qr = np.arange(tr)[:, None]
    krow = np.arange(nk)[None, :] - tr
    dr = np.clip(krow - qr, -(NA_WIN_ROWS - 1), NA_WIN_ROWS - 1) + NA_WIN_ROWS - 1
    starts = (0 * qr, qr - NA_WIN_ROWS // 2, 0 * qr + tr - NA_WIN_ROWS)
    row_ok = np.stack([(krow >= st) & (krow < st + NA_WIN_ROWS) for st in starts])
    t = jnp.take(cols, jnp.asarray(dr.reshape(-1)), axis=1)
    t = t.reshape(NA_HEADS, tr, nk, GRID_W, GRID_W).transpose(0, 1, 3, 2, 4)
    ok = row_ok[:, None, :, None, :, None] & col_ok[None, None, None, :, None, :]
    t = jnp.where(ok, t[None], NEG).reshape(3, NA_HEADS, TILE, 3 * TILE)
    return jnp.concatenate([jnp.full_like(t[:1], NEG), t], axis=0)


def _na_attention(q, k, v, bias):
    b, m, w = q.shape
    nt = m // TILE
    nb = nt - 1

    def lat(off):
        return lambda bb, i: (bb, jnp.clip(i - 1 + off, 0, nb - 1) + 1, 0)

    def kind(bb, i):
        return (jnp.where(i == 0, 0, jnp.where(i == 1, 1, jnp.where(i == nb, 3, 2))), 0, 0, 0)

    blk = lambda f: pl.BlockSpec((1, TILE, w), f)
    kv_specs = [blk(lambda bb, i: (bb, 0, 0)), blk(lat(-1)), blk(lat(0)), blk(lat(1))]
    return pl.pallas_call(
        _na_kernel,
        out_shape=jax.ShapeDtypeStruct((b, m, w), BF16),
        grid=(b, nt),
        in_specs=[blk(lambda bb, i: (bb, i, 0))] + kv_specs + kv_specs
                 + [pl.BlockSpec((1, NA_HEADS, TILE, 3 * TILE), kind)],
        out_specs=blk(lambda bb, i: (bb, i, 0)),
        compiler_params=_params(2),
        name="neighbourhood_attention",
    )(q, k, k, k, k, v, v, v, v, bias)


def _swa_kernel(sink_ref, q_ref, kc_ref, kp_ref, kcur_ref, kn_ref, vc_ref, vp_ref, vcur_ref, vn_ref, mask_ref,
                o_ref):
    q = q_ref[0]
    kc, vc = kc_ref[0], vc_ref[0]
    kl = jnp.concatenate([kp_ref[0], kcur_ref[0], kn_ref[0]], axis=0)
    vl = jnp.concatenate([vp_ref[0], vcur_ref[0], vn_ref[0]], axis=0)
    group = SWA_HEADS // SWA_KV_HEADS
    rows = group * TILE
    mask = jnp.concatenate([mask_ref[0]] * group, axis=0)
    rid = lax.broadcasted_iota(I32, (rows, 1), 0)
    outs = []
    for g in range(SWA_KV_HEADS):
        ksl = slice(g * HEAD_DIM, (g + 1) * HEAD_DIM)
        qg = jnp.concatenate([q[:, (g * group + j) * HEAD_DIM:(g * group + j + 1) * HEAD_DIM]
                              for j in range(group)], axis=0)
        sink = jnp.zeros((rows, 1), F32)
        for j in range(group):
            sink = jnp.where(rid // TILE == j, sink_ref[g * group + j], sink)
        s_c = _nt_dot(qg, kc[:, ksl])
        s_l = _nt_dot(qg, kl[:, ksl]) + mask
        mx = jnp.maximum(jnp.maximum(jnp.max(s_c, axis=1, keepdims=True), jnp.max(s_l, axis=1, keepdims=True)), sink)
        p_c = jnp.exp(s_c - mx)
        p_l = jnp.exp(s_l - mx)
        den = jnp.sum(p_c, axis=1, keepdims=True) + jnp.sum(p_l, axis=1, keepdims=True) + jnp.exp(sink - mx)
        o = (_dot(p_c.astype(BF16), vc[:, ksl]) + _dot(p_l.astype(BF16), vl[:, ksl])) / den
        outs.extend(o[j * TILE:(j + 1) * TILE] for j in range(group))
    o_ref[0] = jnp.concatenate(outs, axis=1).astype(BF16)


def _swa_mask_table():
    qpos = np.arange(TILE)[:, None]
    kpos = np.arange(3 * TILE)[None, :] - TILE
    near = np.abs(qpos - kpos) <= SWA_WINDOW
    kinds = [np.zeros_like(near), near & (kpos >= 0), near, near & (kpos < TILE)]
    return jnp.asarray(np.where(np.stack(kinds), 0.0, NEG), F32)


def _tile_kind(i, nb):
    return jnp.where(i == 0, 0, jnp.where(i == 1, 1, jnp.where(i == nb, 3, 2)))


def _swa_attention(q, k, v, sink):
    b, m, wq = q.shape
    wk = k.shape[2]
    nt = m // TILE
    nb = nt - 1

    def lat(off):
        return lambda bb, i: (bb, jnp.clip(i - 1 + off, 0, nb - 1) + 1, 0)

    blk = lambda w, f: pl.BlockSpec((1, TILE, w), f)
    kv_specs = [blk(wk, lambda bb, i: (bb, 0, 0)), blk(wk, lat(-1)), blk(wk, lat(0)), blk(wk, lat(1))]
    return pl.pallas_call(
        _swa_kernel,
        out_shape=jax.ShapeDtypeStruct((b, m, wq), BF16),
        grid=(b, nt),
        in_specs=[pl.BlockSpec(memory_space=pltpu.SMEM), blk(wq, lambda bb, i: (bb, i, 0))] + kv_specs + kv_specs
                 + [pl.BlockSpec((1, TILE, 3 * TILE), lambda bb, i: (_tile_kind(i, nb), 0, 0))],
        out_specs=blk(wq, lambda bb, i: (bb, i, 0)),
        compiler_params=_params(2),
        name="windowed_attention",
    )(sink, q, k, k, k, k, v, v, v, v, _swa_mask_table())


def _diff_kernel(lam_ref, q_ref, k_ref, v_ref, g_ref, o_ref, acc_ref, m_ref, s_ref, cmax_ref,
                 *, n_chunks, kblk, fast_kblk, lam_init):
    i = pl.program_id(2)
    q = q_ref[0]
    lane = lax.broadcasted_iota(I32, q.shape, 1)
    zero = jnp.zeros_like(q)
    qq = jnp.concatenate([jnp.where(lane < HEAD_DIM, q, zero), jnp.where(lane >= HEAD_DIM, q, zero)], axis=0)

    def chunk(c):
        return pl.ds(pl.multiple_of(c * kblk, LANES), kblk)

    def scores(c, slot):
        s = _nt_dot(qq, k_ref[0, chunk(c), :])
        s_ref[slot] = s
        cmax_ref[slot] = jnp.max(s, axis=1, keepdims=True)

    def accumulate(s, cmax, v):
        m_prev = m_ref[...]
        m_new = jnp.maximum(m_prev, cmax)
        p = jnp.exp2(s - m_new).astype(BF16)
        acc_ref[...] = jnp.exp2(m_prev - m_new) * acc_ref[...] + _dot(p, v)
        m_ref[...] = m_new

    def reset():
        m_ref[...] = jnp.full(m_ref.shape, NEG, F32)
        acc_ref[...] = jnp.zeros(acc_ref.shape, F32)

    def safe_sweep():
        reset()
        scores(0, 0)

        def pair(c, last):
            for cur in (0, 1):
                if not (last and cur == 1):
                    scores(c + cur + 1, 1 - cur)
                accumulate(s_ref[cur], cmax_ref[cur], v_ref[0, chunk(c + cur), :])

        def body(c2, carry):
            pair(2 * c2, False)
            return carry

        lax.fori_loop(0, n_chunks // 2 - 1, body, 0)
        pair(n_chunks - 2, True)

    def fast_sweep():
        keys = lambda c: slice(c * fast_kblk, (c + 1) * fast_kblk)
        s = _nt_dot(qq, k_ref[0, keys(0), :])
        m0 = jnp.max(s, axis=1, keepdims=True)
        acc = _dot(jnp.exp2(s - m0).astype(BF16), v_ref[0, keys(0), :])
        for c in range(1, n_chunks * kblk // fast_kblk):
            s = _nt_dot(qq, k_ref[0, keys(c), :])
            acc = acc + _dot(jnp.exp2(s - m0).astype(BF16), v_ref[0, keys(c), :])
        acc_ref[...] = acc

    @pl.when(i == 0)
    def _():
        reset()
        s = _nt_dot(qq, k_ref[0, 0:CTX, :])
        accumulate(s, jnp.max(s, axis=1, keepdims=True), v_ref[0, 0:CTX, :])

    @pl.when(i > 0)
    def _():
        fast_sweep()

        @pl.when(jnp.logical_not(jnp.max(acc_ref[:, DIFF_V:]) < DIFF_MAX_DENOMINATOR))
        def _():
            safe_sweep()

    acc = acc_ref[...]
    o = acc[:, :DIFF_V] / acc[:, DIFF_V:]
    d = o[:TILE] - lam_ref[0] * o[TILE:]
    y = d * lax.rsqrt(jnp.mean(d * d, axis=-1, keepdims=True) + EPS) * g_ref[...]
    o_ref[0] = (y * (1.0 - lam_init)).astype(BF16)


def _diff_chunking(m, max_chunk=2048):
    for n_chunks in range(2, m // LANES + 1, 2):
        if m % (n_chunks * LANES) == 0 and m // n_chunks <= max_chunk:
            return n_chunks, m // n_chunks
    raise ValueError(f"no chunking for {m} keys")


def _diff_attention(q, k, v, lam, g, lam_init):
    b, m, _ = q.shape
    nt = m // TILE
    n_chunks, kblk = _diff_chunking(m)
    fast_kblk = max(k for k in range(MXU_TILE, 2048 + 1, MXU_TILE) if m % k == 0)
    return pl.pallas_call(
        functools.partial(_diff_kernel, n_chunks=n_chunks, kblk=kblk, fast_kblk=fast_kblk, lam_init=lam_init),
        out_shape=jax.ShapeDtypeStruct((b, m, DIFF_HEADS * DIFF_V), BF16),
        grid=(b, DIFF_HEADS, nt),
        in_specs=[pl.BlockSpec(memory_space=pltpu.SMEM),
                  pl.BlockSpec((1, TILE, 2 * HEAD_DIM), lambda bb, hh, i: (bb, i, hh)),
                  pl.BlockSpec((1, m, 2 * HEAD_DIM), lambda bb, hh, i: (bb, 0, hh)),
                  pl.BlockSpec((1, m, 2 * DIFF_V), lambda bb, hh, i: (bb, 0, hh)),
                  pl.BlockSpec((1, DIFF_V), lambda bb, hh, i: (0, 0))],
        out_specs=pl.BlockSpec((1, TILE, DIFF_V), lambda bb, hh, i: (bb, i, hh)),
        scratch_shapes=[pltpu.VMEM((2 * TILE, 2 * DIFF_V), F32), pltpu.VMEM((2 * TILE, 1), F32),
                        pltpu.VMEM((2, 2 * TILE, kblk), F32), pltpu.VMEM((2, 2 * TILE, 1), F32)],
        compiler_params=_params(3),
        name="differential_attention",
    )(lam, q, k, v, g)


def _merge_kernel(ona_ref, od_ref, osw_ref, gs_ref, x_ref, mod_ref, wna_ref, wd_ref, wsw_ref, wo_ref,
                  g2_ref, wrt_ref, xo_ref, h2_ref, afft_ref):
    d = x_ref.shape[2]
    gs = gs_ref[0]
    y = (gs[:, :d].astype(F32) * _dot(ona_ref[0], wna_ref[...])
         + gs[:, d:2 * d].astype(F32) * _dot(od_ref[0], wd_ref[...])
         + gs[:, 2 * d:].astype(F32) * _dot(osw_ref[0], wsw_ref[...]))
    mod = _row_modulation(mod_ref, x_ref.shape[1])
    xn = x_ref[0] + mod(2) * _dot(y.astype(BF16), wo_ref[...])
    xo_ref[0] = xn
    r = xn * lax.rsqrt(jnp.mean(xn * xn, axis=-1, keepdims=True) + EPS) * g2_ref[...]
    h2 = r * (1.0 + mod(4)) + mod(3)
    h2_ref[0] = h2.astype(BF16)
    lt = lax.dot_general(wrt_ref[...], h2, (((1,), (1,)), ((), ())),
                         preferred_element_type=F32, precision=HIGHEST)
    et = jnp.exp(lt - jnp.max(lt, axis=0, keepdims=True))
    afft_ref[0] = et / jnp.sum(et, axis=0, keepdims=True)


def _merge(o_na, o_d, o_sw, gs, xu, modtab, w_na, w_d, w_sw, w_o, g2, w_r):
    b, m, d = xu.shape
    tm = _matmul_rows(m, 640)
    ne = w_r.shape[1]
    row = lambda w: pl.BlockSpec((1, tm, w), lambda bb, i: (bb, i, 0))
    full = lambda a: pl.BlockSpec(a.shape, lambda bb, i: (0,) * a.ndim)
    w_rt = w_r.T
    return pl.pallas_call(
        _merge_kernel,
        out_shape=[jax.ShapeDtypeStruct((b, m, d), F32), jax.ShapeDtypeStruct((b, m, d), BF16),
                   jax.ShapeDtypeStruct((b, ne, m), F32)],
        grid=(b, m // tm),
        in_specs=[row(o_na.shape[2]), row(o_d.shape[2]), row(o_sw.shape[2]), row(gs.shape[2]), row(d),
                  pl.BlockSpec((1, 2, N_MOD, d), lambda bb, i: (bb, 0, 0, 0)),
                  full(w_na), full(w_d), full(w_sw), full(w_o), full(g2), full(w_rt)],
        out_specs=[row(d), row(d), pl.BlockSpec((1, ne, tm), lambda bb, i: (bb, 0, i))],
        compiler_params=_params(2),
        name="merge_residual_router",
    )(o_na, o_d, o_sw, gs, xu, modtab, w_na, w_d, w_sw, w_o, g2, w_rt)


def _route_kernel(a_ref, gpos_ref, gfull_ref, *, n_latent_sets, cap_lat, cap_ctx):
    ne, c, w = a_ref.shape[1:]
    a = a_ref[0]
    bits = lax.bitcast_convert_type(a, I32)
    cap = jnp.where(pl.program_id(0) < n_latent_sets, cap_lat, cap_ctx).astype(F32)

    def count(mask):
        return jnp.sum(jnp.sum(mask.astype(F32), axis=2, keepdims=True), axis=1, keepdims=True)

    def search(it, thr):
        cand = thr | jnp.left_shift(jnp.int32(1), 30 - it)
        return jnp.where(count(bits >= cand) >= cap, cand, thr)

    thr = lax.fori_loop(0, 31, search, jnp.zeros((ne, 1, 1), I32))
    gt = bits > thr
    eq = bits == thr
    need = cap - count(gt)

    upper = (lax.broadcasted_iota(I32, (w, w), 0) <= lax.broadcasted_iota(I32, (w, w), 1)).astype(BF16)
    lower = (lax.broadcasted_iota(I32, (c, c), 1) < lax.broadcasted_iota(I32, (c, c), 0)).astype(BF16)

    def exclusive_cumsum(mask):
        x = mask.astype(F32).reshape(ne * c, w)
        within = _dot(x.astype(BF16), upper)
        tot = jnp.broadcast_to(within[:, w - 1:w], (ne * c, w)).astype(BF16)
        before = jnp.concatenate([_dot(lower, tot[e * c:(e + 1) * c]) for e in range(ne)], axis=0)
        return (before + within - x).reshape(ne, c, w)

    sel = gt | (eq & (exclusive_cumsum(eq) < need))
    g = exclusive_cumsum(sel).astype(I32)
    gfull_ref[0] = g
    gpos_ref[0] = jnp.where(sel, g, -1)


def _route(aff_sets, n_latent_sets, cap_lat, cap_ctx):
    s, ne, n = aff_sets.shape
    c = n // LANES
    a4 = aff_sets.reshape(s, ne, c, LANES)
    blk = pl.BlockSpec((1, ne, c, LANES), lambda i: (i, 0, 0, 0))
    gpos, gfull = pl.pallas_call(
        functools.partial(_route_kernel, n_latent_sets=n_latent_sets, cap_lat=cap_lat, cap_ctx=cap_ctx),
        out_shape=[jax.ShapeDtypeStruct(a4.shape, I32)] * 2,
        grid=(s,),
        in_specs=[blk],
        out_specs=[blk, blk],
        compiler_params=_params(1),
        name="expert_choice_select",
    )(a4)
    return gpos.reshape(s, ne, n), gfull.reshape(s, ne, n)


def _window(cap):
    w = min(TILE + BF16_ROWS, cap)
    main = min(WINDOW_MAIN_ROWS, w)
    return main, w - main, cap - w


def _windows(gfull, cap):
    b, ne, _ = gfull.shape
    main, _, max_start = _window(cap)
    seg_start = gfull[:, :, ::TILE]
    seg_end = jnp.concatenate([gfull[:, :, TILE::TILE], jnp.full((b, ne, 1), cap, I32)], axis=2)
    astart = jnp.minimum(seg_start // BF16_ROWS * BF16_ROWS, max_start)
    return astart.reshape(-1), (seg_end > astart + main).astype(I32).reshape(-1)


def _gather_kernel(a_ref, t_ref, g_ref, aff_ref, h_ref, o_ref, w_ref, *, ne, nblk, eg, main, tail):
    b, egi, j = pl.program_id(0), pl.program_id(1), pl.program_id(2)

    @pl.when(j == 0)
    def _():
        o_ref[...] = jnp.zeros(o_ref.shape, o_ref.dtype)
        w_ref[...] = jnp.zeros(w_ref.shape, w_ref.dtype)

    hb = h_ref[0]
    t = hb.shape[0]

    def idx(k):
        return (b * ne + egi * eg + k) * nblk + j

    def place(experts, first, rows):
        hits = [lax.broadcasted_iota(I32, (rows, t), 0) == g_ref[0, k] - (a_ref[idx(k)] + first) for k in experts]
        onehot = jnp.concatenate([jnp.where(h, 1.0, 0.0).astype(BF16) for h in hits], axis=0)
        picked = _dot(onehot, hb).astype(BF16)
        for n, (k, hit) in enumerate(zip(experts, hits)):
            weight = jnp.sum(jnp.where(hit, aff_ref[0, k], 0.0), axis=1, keepdims=True)
            sl = (0, k, pl.ds(pl.multiple_of(a_ref[idx(k)] + first, BF16_ROWS), rows), slice(None))
            o_ref[sl] = o_ref[sl] + picked[n * rows:(n + 1) * rows]
            w_ref[sl] = w_ref[sl] + weight

    place(list(range(eg)), 0, main)
    for k in range(eg if tail else 0):
        @pl.when(t_ref[idx(k)] != 0)
        def _(k=k):
            place([k], main, tail)


def _gather(h2u, afft, gpos, astart, need_tail, cap, tile_off, eg):
    b, ne, n = gpos.shape
    m, d = h2u.shape[1:]
    nblk = n // TILE
    main, tail, _ = _window(cap)
    lists = lambda w: pl.BlockSpec((1, eg, cap, w), lambda bb, e, j, a, nt: (bb, e, 0, 0))
    return pl.pallas_call(
        functools.partial(_gather_kernel, ne=ne, nblk=nblk, eg=eg, main=main, tail=tail),
        out_shape=[jax.ShapeDtypeStruct((b, ne, cap, d), BF16), jax.ShapeDtypeStruct((b, ne, cap, 1), F32)],
        grid_spec=pltpu.PrefetchScalarGridSpec(
            num_scalar_prefetch=2,
            grid=(b, ne // eg, nblk),
            in_specs=[pl.BlockSpec((1, eg, 1, TILE), lambda bb, e, j, a, nt: (bb, e, 0, j)),
                      pl.BlockSpec((1, eg, 1, TILE), lambda bb, e, j, a, nt: (bb, e, 0, tile_off + j)),
                      pl.BlockSpec((1, TILE, d), lambda bb, e, j, a, nt: (bb, tile_off + j, 0))],
            out_specs=[lists(d), lists(1)]),
        compiler_params=_params(3),
        name="expert_gather",
    )(astart, need_tail, gpos.reshape(b, ne, 1, n), afft.reshape(b, ne, 1, m), h2u)


def _ffn_kernel(x_ref, rw_ref, wg_ref, wu_ref, wd_ref, o_ref, wg_s, wu_s, wd_s):
    @pl.when((pl.program_id(1) == 0) & (pl.program_id(2) == 0))
    def _():
        wg_s[...] = wg_ref[0, 0].astype(BF16)
        wu_s[...] = wu_ref[0, 0].astype(BF16)
        wd_s[...] = wd_ref[0, 0].astype(BF16)

    x = x_ref[0, 0]
    gate = _dot(x, wg_s[...])
    up = _dot(x, wu_s[...])
    hid = (gate / (1.0 + jnp.exp(-gate)) * up).astype(BF16)
    o_ref[0, 0] = (_dot(hid, wd_s[...]) * rw_ref[0, 0]).astype(BF16)


def _expert_ffn(xs, ws, w_gate, w_up, w_down, layer):
    b, ne, cap, d = xs.shape
    tr = min(cap, 512)
    wspec = lambda w: pl.BlockSpec((1, 1) + w.shape[2:], lambda e, bb, r: (layer, e, 0, 0))
    rows = lambda w: pl.BlockSpec((1, 1, tr, w), lambda e, bb, r: (bb, e, r, 0))
    return pl.pallas_call(
        _ffn_kernel,
        out_shape=jax.ShapeDtypeStruct(xs.shape, BF16),
        grid=(ne, b, cap // tr),
        in_specs=[rows(d), rows(1), wspec(w_gate), wspec(w_up), wspec(w_down)],
        out_specs=rows(d),
        scratch_shapes=[pltpu.VMEM(w.shape[2:], BF16) for w in (w_gate, w_up, w_down)],
        compiler_params=_params(3),
        name="expert_swiglu",
    )(xs, ws, w_gate, w_up, w_down)


def _combine_kernel(a_ref, t_ref, x_ref, g_ref, mod_ref, fg_ref, ye_ref, o_ref,
                    win_ref, tail_ref, tacc_ref, sem_ref, tsem_ref, *, ne, nblk, main, tail, final):
    b, j = pl.program_id(0), pl.program_id(1)
    step = b * nblk + j
    slot = step % 2

    def start_at(bb, jj, e):
        return pl.multiple_of(a_ref[(bb * ne + e) * nblk + jj], BF16_ROWS)

    def start_of(e):
        return start_at(b, j, e)

    def window_copy(bb, jj, e, sl):
        return pltpu.make_async_copy(ye_ref.at[bb, e, pl.ds(start_at(bb, jj, e), main)],
                                     win_ref.at[sl, e * main:(e + 1) * main], sem_ref.at[sl, e])

    @pl.when(step == 0)
    def _():
        for e in range(ne):
            window_copy(b, j, e, slot).start()

    @pl.when(step + 1 < pl.num_programs(0) * nblk)
    def _():
        nxt = step + 1
        for e in range(ne):
            window_copy(nxt // nblk, nxt % nblk, e, 1 - slot).start()

    t = x_ref.shape[1]

    def pick(experts, first, rows, windows):
        onehot = jnp.concatenate(
            [jnp.where(lax.broadcasted_iota(I32, (rows, t), 0) == g_ref[0, e] - (start_of(e) + first), 1.0, 0.0)
             .astype(BF16) for e in experts], axis=0)
        return lax.dot_general(onehot, windows, (((0,), (0,)), ((), ())), preferred_element_type=F32)

    if tail:
        tacc_ref[...] = jnp.zeros(tacc_ref.shape, F32)
        for e in range(ne):
            @pl.when(t_ref[(b * ne + e) * nblk + j] != 0)
            def _(e=e):
                cp = pltpu.make_async_copy(ye_ref.at[b, e, pl.ds(start_of(e) + main, tail)], tail_ref, tsem_ref.at[0])
                cp.start()
                cp.wait()
                tacc_ref[...] += pick([e], main, tail, tail_ref[...])

    for e in range(ne):
        window_copy(b, j, e, slot).wait()
    acc = pick(list(range(ne)), 0, main, win_ref[slot])
    if tail:
        acc = acc + tacc_ref[...]
    x = x_ref[0] + mod_ref[0, 0][5:6] * acc
    if final:
        x = x * lax.rsqrt(jnp.mean(x * x, axis=-1, keepdims=True) + EPS) * fg_ref[...]
    o_ref[0] = x


def _combine(xu, gpos, astart, need_tail, ye, modtab, final_g, tile_off, kind, final):
    b, m, d = xu.shape
    _, ne, n = gpos.shape
    cap = ye.shape[2]
    nblk = n // TILE
    main, tail, _ = _window(cap)
    out_spec = pl.BlockSpec((1, TILE, d), (lambda bb, j, a, nt: (bb, j, 0)) if final
                            else (lambda bb, j, a, nt: (bb, tile_off + j, 0)))
    return pl.pallas_call(
        functools.partial(_combine_kernel, ne=ne, nblk=nblk, main=main, tail=tail, final=final),
        out_shape=jax.ShapeDtypeStruct((b, n, d) if final else xu.shape, F32),
        grid_spec=pltpu.PrefetchScalarGridSpec(
            num_scalar_prefetch=2,
            grid=(b, nblk),
            in_specs=[pl.BlockSpec((1, TILE, d), lambda bb, j, a, nt: (bb, tile_off + j, 0)),
                      pl.BlockSpec((1, ne, 1, TILE), lambda bb, j, a, nt: (bb, 0, 0, j)),
                      pl.BlockSpec((1, 1, N_MOD, d), lambda bb, j, a, nt: (bb, kind, 0, 0)),
                      pl.BlockSpec((1, d), lambda bb, j, a, nt: (0, 0)),
                      pl.BlockSpec(memory_space=pl.ANY)],
            out_specs=out_spec,
            scratch_shapes=[pltpu.VMEM((2, ne * main, d), BF16), pltpu.VMEM((max(tail, BF16_ROWS), d), BF16),
                            pltpu.VMEM((TILE, d), F32), pltpu.SemaphoreType.DMA((2, ne)),
                            pltpu.SemaphoreType.DMA((1,))]),
        input_output_aliases={} if final else {2: 0},
        compiler_params=_params(2),
        name="expert_combine",
    )(astart, need_tail, xu, gpos.reshape(b, ne, 1, n), modtab, final_g, ye)


def _rope_tables(n):
    t = np.arange(n)
    row = (t // GRID_W).astype(np.float32)[:, None]
    col = (t % GRID_W).astype(np.float32)[:, None]
    inv = (ROPE_BASE ** (-np.arange(ROPE_PAIRS, dtype=np.float32) / ROPE_PAIRS)).astype(np.float32)
    ang = np.concatenate([row * inv, row * inv, col * inv, col * inv], axis=-1)
    cos, sin = np.cos(ang), np.sin(ang)
    half = (np.arange(HEAD_DIM) % (2 * ROPE_PAIRS)) < ROPE_PAIRS
    sin = np.where(half[None], -sin, sin)
    cos = np.concatenate([np.ones((CTX, HEAD_DIM)), cos], axis=0)
    sin = np.concatenate([np.zeros((CTX, HEAD_DIM)), sin], axis=0)
    tile2 = lambda a: jnp.asarray(np.concatenate([a, a], axis=1), F32)
    return tile2(cos), tile2(sin)


def _moe(xu, h2u, afft, modtab, weights, layer, final_g, final):
    b, m, d = xu.shape
    n = m - CTX
    w_gate, w_up, w_down = weights
    cap_lat = max(1, EC_CAPACITY * n // N_EXPERTS)
    cap_ctx = max(1, EC_CAPACITY * CTX // N_EXPERTS)
    sets = [afft[:, :, CTX:]]
    if not final:
        sets.append(jnp.concatenate([afft[:, :, :CTX], jnp.full((b, N_EXPERTS, n - CTX), -1.0, F32)], axis=2))
    gpos, gfull = _route(jnp.concatenate(sets, axis=0), b, cap_lat, cap_ctx)

    def run(idx, n_set, cap, tile_off, kind, x_in, fin):
        gp = gpos[idx * b:(idx + 1) * b, :, :n_set]
        gf = gfull[idx * b:(idx + 1) * b, :, :n_set]
        astart, need_tail = _windows(gf, cap)
        xs, ws = _gather(h2u, afft, gp, astart, need_tail, cap, tile_off, eg=4)
        ye = _expert_ffn(xs, ws, w_gate, w_up, w_down, layer)
        return _combine(x_in, gp, astart, need_tail, ye, modtab, final_g, tile_off, kind, fin)

    if final:
        return run(0, n, cap_lat, 1, 1, xu, True)
    xu = run(0, n, cap_lat, 1, 1, xu, False)
    return run(1, CTX, cap_ctx, 0, 0, xu, False)


def kernel(x, c, ctx, c_ctx, w_mod, b_mod, norm1_g, w_in, na_rpb, diff_lambda, diff_subln_g, swa_sink,
           w_branch_na, w_branch_diff, w_branch_swa, w_out, norm2_g, w_router, w_expert_gate, w_expert_up,
           w_expert_down, final_g):
    b, n, d = x.shape
    depth = w_mod.shape[0]
    assert ctx.shape[1] == CTX and d == D_MODEL and n % (2 * TILE) == 0 and n // TILE >= 3

    cin = jnp.concatenate([c, c_ctx[None], jnp.zeros((8 - b - 1, d), F32)], axis=0)
    mod_all = _modulation(cin, w_mod, b_mod)
    cos_u, sin_u = _rope_tables(n)
    xu = jnp.concatenate([ctx, x], axis=1)
    final_g2 = final_g.reshape(1, d)

    out = None
    for l in range(depth):
        final = l == depth - 1
        lam_init = 0.8 - 0.6 * math.exp(-0.3 * l)
        mod_l = mod_all[l].reshape(8, N_MOD, d)
        modtab = jnp.stack([jnp.broadcast_to(mod_l[b], (b, N_MOD, d)), mod_l[:b]], axis=1)
        lam_p = diff_lambda[l].astype(F32)
        lam = (jnp.exp(jnp.sum(lam_p[0] * lam_p[1])) - jnp.exp(jnp.sum(lam_p[2] * lam_p[3])) + lam_init).reshape(1)

        (q_na, k_na, v_na, q_d, k_d, v_d, q_s, k_s, v_s, gs) = _project(
            xu, modtab, norm1_g[l].reshape(1, d), w_in[l].astype(BF16), cos_u, sin_u)
        o_na = _na_attention(q_na, k_na, v_na, _na_bias_table(na_rpb[l]))
        o_d = _diff_attention(q_d, k_d, v_d, lam, diff_subln_g[l].reshape(1, DIFF_V), lam_init)
        o_s = _swa_attention(q_s, k_s, v_s, swa_sink[l].astype(F32))
        xu, h2u, afft = _merge(
            o_na, o_d, o_s, gs, xu, modtab, w_branch_na[l].astype(BF16), w_branch_diff[l].astype(BF16),
            w_branch_swa[l].astype(BF16), w_out[l].astype(BF16), norm2_g[l].reshape(1, d), w_router[l])
        weights = (w_expert_gate, w_expert_up, w_expert_down)
        res = _moe(xu, h2u, afft, modtab, weights, l, final_g2, final)
        if final:
            out = res
        else:
            xu = res
    return out
```

```python
import functools
import math

import numpy as np
import jax
import jax.numpy as jnp
from jax import lax
from jax.experimental import pallas as pl
from jax.experimental.pallas import tpu as pltpu

F32 = jnp.float32
BF16 = jnp.bfloat16
I32 = jnp.int32
HIGHEST = lax.Precision.HIGHEST

D_MODEL = 1024
CTX = 256
TILE = 256
GRID_W = 64
HEAD_DIM = 64
ROPE_PAIRS = HEAD_DIM // 4
ROPE_BASE = 10000.0
EPS = 1e-6
N_MOD = 6
NA_HEADS = 4
NA_WIN_ROWS = 8
NA_WIN_COLS = 16
NA_TILE_ROWS = TILE // GRID_W
DIFF_HEADS = 4
DIFF_V = 2 * HEAD_DIM
SWA_HEADS = 4
SWA_KV_HEADS = 2
SWA_WINDOW = 128
N_EXPERTS = 16
EC_CAPACITY = 2
NEG = -1e30
LANES = 128
MXU_TILE = 256
LOG2E = math.log2(math.e)
DIFF_MAX_DENOMINATOR = 2.0 ** 64
BF16_ROWS = 16
WINDOW_MAIN_ROWS = 64

C_QNA, C_QD, C_QS = 0, 256, 768
C_KNA, C_KD, C_KS = 1024, 1280, 1792
C_VNA, C_VD, C_VS = 1920, 2176, 2688
C_GATE, C_END = 2816, 5888

VMEM_LIMIT = 56 * 1024 * 1024


def _params(n_axes, vmem=VMEM_LIMIT):
    return pltpu.CompilerParams(dimension_semantics=("arbitrary",) * n_axes, vmem_limit_bytes=vmem)


def _nt_dot(a, b):
    return lax.dot_general(a, b, (((1,), (1,)), ((), ())), preferred_element_type=F32)


def _dot(a, b):
    return jnp.dot(a, b, preferred_element_type=F32)


def _mod_kernel(c_ref, w_ref, b_ref, o_ref):
    c = c_ref[...]
    s = c / (1.0 + jnp.exp(-c))
    o_ref[0] = jnp.dot(s, w_ref[0], preferred_element_type=F32, precision=HIGHEST) + b_ref[0]


def _modulation(cin, w_mod, b_mod):
    depth, d, w = w_mod.shape
    tn = 1024
    return pl.pallas_call(
        _mod_kernel,
        out_shape=jax.ShapeDtypeStruct((depth, 8, w), F32),
        grid=(depth, w // tn),
        in_specs=[pl.BlockSpec((8, d), lambda l, j: (0, 0)),
                  pl.BlockSpec((1, d, tn), lambda l, j: (l, 0, j)),
                  pl.BlockSpec((1, 1, tn), lambda l, j: (l, 0, j))],
        out_specs=pl.BlockSpec((1, 8, tn), lambda l, j: (l, 0, j)),
        compiler_params=_params(2),
        name="modulation",
    )(cin, w_mod, b_mod.reshape(depth, 1, w))


def _row_modulation(mod_ref, rows):
    first = pl.program_id(1) * rows
    is_ctx = first + lax.broadcasted_iota(I32, (rows, 1), 0) < CTX
    return lambda k: jnp.where(is_ctx, mod_ref[0, 0, k:k + 1], mod_ref[0, 1, k:k + 1])


def _proj_kernel(x_ref, mod_ref, g_ref, w_ref, cos_ref, sin_ref,
                 qna_ref, kna_ref, vna_ref, qd_ref, kd_ref, vd_ref, qs_ref, ks_ref, vs_ref, gs_ref):
    x = x_ref[0]
    mod = _row_modulation(mod_ref, x.shape[0])
    y = x * lax.rsqrt(jnp.mean(x * x, axis=-1, keepdims=True) + EPS) * g_ref[...]
    h = (y * (1.0 + mod(1)) + mod(0)).astype(BF16)

    def proj(a, b):
        return _dot(h, w_ref[:, a:b])

    cos = cos_ref[...]
    sin = sin_ref[...]
    lane = lax.broadcasted_iota(I32, cos.shape, 1)
    first_half = (lane % (2 * ROPE_PAIRS)) < ROPE_PAIRS

    def rope(t):
        outs = []
        for j in range(t.shape[1] // LANES):
            c = t[:, j * LANES:(j + 1) * LANES]
            r = jnp.where(first_half, pltpu.roll(c, LANES - ROPE_PAIRS, 1), pltpu.roll(c, ROPE_PAIRS, 1))
            outs.append(c * cos + r * sin)
        return outs[0] if len(outs) == 1 else jnp.concatenate(outs, axis=1)

    scale = HEAD_DIM ** -0.5
    qna_ref[0] = (proj(C_QNA, C_QD) * scale).astype(BF16)
    qd_ref[0] = (rope(proj(C_QD, C_QS)) * (scale * LOG2E)).astype(BF16)
    qs_ref[0] = (rope(proj(C_QS, C_KNA)) * scale).astype(BF16)
    kna_ref[0] = proj(C_KNA, C_KD).astype(BF16)
    kd_ref[0] = rope(proj(C_KD, C_KS)).astype(BF16)
    ks_ref[0] = rope(proj(C_KS, C_VNA)).astype(BF16)
    vna_ref[0] = proj(C_VNA, C_VD).astype(BF16)
    vd = proj(C_VD, C_VS).astype(BF16)
    ones = jnp.ones((vd.shape[0], DIFF_V), BF16)
    vd_ref[0] = jnp.concatenate(
        [blk for hd in range(DIFF_HEADS) for blk in (vd[:, hd * DIFF_V:(hd + 1) * DIFF_V], ones)], axis=1)
    vs_ref[0] = proj(C_VS, C_GATE).astype(BF16)
    gates = proj(C_GATE, C_END)
    gs_ref[0] = (1.0 / (1.0 + jnp.exp(-gates))).astype(BF16)


def _matmul_rows(m, limit):
    return max(t for t in range(LANES, limit + 1, LANES) if m % t == 0)


def _project(xu, modtab, g, w_in, cos_u, sin_u):
    b, m, d = xu.shape
    tm = _matmul_rows(m, 256)
    widths =(256, 256, 256, 512, 512, 2 * DIFF_HEADS * DIFF_V, 256, 128, 128, C_END - C_GATE)
    row = lambda w: pl.BlockSpec((1, tm, w), lambda bb, i: (bb, i, 0))
    return pl.pallas_call(
        _proj_kernel,
        out_shape=[jax.ShapeDtypeStruct((b, m, w), BF16) for w in widths],
        grid=(b, m // tm),
        in_specs=[row(d),
                  pl.BlockSpec((1, 2, N_MOD, d), lambda bb, i: (bb, 0, 0, 0)),
                  pl.BlockSpec((1, d), lambda bb, i: (0, 0)),
                  pl.BlockSpec((d, C_END), lambda bb, i: (0, 0)),
                  pl.BlockSpec((tm, LANES), lambda bb, i: (i, 0)),
                  pl.BlockSpec((tm, LANES), lambda bb, i: (i, 0))],
        out_specs=[row(w) for w in widths],
        compiler_params=_params(2),
        name="norm_project_rope",
    )(xu, modtab, g, w_in, cos_u, sin_u)


def _na_kernel(q_ref, kc_ref, kp_ref, kcur_ref, kn_ref, vc_ref, vp_ref, vcur_ref, vn_ref, bias_ref, o_ref):
    q = q_ref[0]
    kc, vc = kc_ref[0], vc_ref[0]
    kl = jnp.concatenate([kp_ref[0], kcur_ref[0], kn_ref[0]], axis=0)
    vl = jnp.concatenate([vp_ref[0], vcur_ref[0], vn_ref[0]], axis=0)
    outs = []
    for hd in range(NA_HEADS):
        sl = slice(hd * HEAD_DIM, (hd + 1) * HEAD_DIM)
        qh = q[:, sl]
        s_c = _nt_dot(qh, kc[:, sl])
        s_l = _nt_dot(qh, kl[:, sl]) + bias_ref[0, hd]
        mx = jnp.maximum(jnp.max(s_c, axis=1, keepdims=True), jnp.max(s_l, axis=1, keepdims=True))
        p_c = jnp.exp(s_c - mx)
        p_l = jnp.exp(s_l - mx)
        den = jnp.sum(p_c, axis=1, keepdims=True) + jnp.sum(p_l, axis=1, keepdims=True)
        o = _dot(p_c.astype(BF16), vc[:, sl]) + _dot(p_l.astype(BF16), vl[:, sl])
        outs.append(o / den)
    o_ref[0] = jnp.concatenate(outs, axis=1).astype(BF16)


def _na_bias_table(rpb):
    tr, nk = NA_TILE_ROWS, 3 * NA_TILE_ROWS
    qc = np.arange(GRID_W)[:, None]
    kc = np.arange(GRID_W)[None, :]
    cstart = np.clip(qc - NA_WIN_COLS // 2, 0, GRID_W - NA_WIN_COLS)
    col_ok = (kc >= cstart) & (kc < cstart + NA_WIN_COLS)
    dc = np.clip(kc - qc, -(NA_WIN_COLS - 1), NA_WIN_COLS - 1) + NA_WIN_COLS - 1
    onehot = jnp.asarray(np.arange(2 * NA_WIN_COLS - 1)[:, None, None] == dc[None], F32)
    cols = jnp.einsum("hrc,cqk->hrqk", rpb.astype(F32), onehot, precision=HIGHEST)
    qr = np.arange(tr)[:, None]
    krow = np.arange(nk)[None, :] - tr
    dr = np.clip(krow - qr, -(NA_WIN_ROWS - 1), NA_WIN_ROWS - 1) + NA_WIN_ROWS - 1
    starts = (0 * qr, qr - NA_WIN_ROWS // 2, 0 * qr + tr - NA_WIN_ROWS)
    row_ok = np.stack([(krow >= st) & (krow < st + NA_WIN_ROWS) for st in starts])
    t = jnp.take(cols, jnp.asarray(dr.reshape(-1)), axis=1)
    t = t.reshape(NA_HEADS, tr, nk, GRID_W, GRID_W).transpose(0, 1, 3, 2, 4)
    ok = row_ok[:, None, :, None, :, None] & col_ok[None, None, None, :, None, :]
    t = jnp.where(ok, t[None], NEG).reshape(3, NA_HEADS, TILE, 3 * TILE)
    return jnp.concatenate([jnp.full_like(t[:1], NEG), t], axis=0)


def _na_attention(q, k, v, bias):
    b, m, w = q.shape
    nt = m // TILE
    nb = nt - 1

    def lat(off):
        return lambda bb, i: (bb, jnp.clip(i - 1 + off, 0, nb - 1) + 1, 0)

    def kind(bb, i):
        return (jnp.where(i == 0, 0, jnp.where(i == 1, 1, jnp.where(i == nb, 3, 2))), 0, 0, 0)

    blk = lambda f: pl.BlockSpec((1, TILE, w), f)
    kv_specs = [blk(lambda bb, i: (bb, 0, 0)), blk(lat(-1)), blk(lat(0)), blk(lat(1))]
    return pl.pallas_call(
        _na_kernel,
        out_shape=jax.ShapeDtypeStruct((b, m, w), BF16),
        grid=(b, nt),
        in_specs=[blk(lambda bb, i: (bb, i, 0))] + kv_specs + kv_specs
                 + [pl.BlockSpec((1, NA_HEADS, TILE, 3 * TILE), kind)],
        out_specs=blk(lambda bb, i: (bb, i, 0)),
        compiler_params=_params(2),
        name="neighbourhood_attention",
    )(q, k, k, k, k, v, v, v, v, bias)


def _swa_kernel(sink_ref, q_ref, kc_ref, kp_ref, kcur_ref, kn_ref, vc_ref, vp_ref, vcur_ref, vn_ref, mask_ref,
                o_ref):
    q = q_ref[0]
    kc, vc = kc_ref[0], vc_ref[0]
    kl = jnp.concatenate([kp_ref[0], kcur_ref[0], kn_ref[0]], axis=0)
    vl = jnp.concatenate([vp_ref[0], vcur_ref[0], vn_ref[0]], axis=0)
    group = SWA_HEADS // SWA_KV_HEADS
    rows = group * TILE
    mask = jnp.concatenate([mask_ref[0]] * group, axis=0)
    rid = lax.broadcasted_iota(I32, (rows, 1), 0)
    outs = []
    for g in range(SWA_KV_HEADS):
        ksl = slice(g * HEAD_DIM, (g + 1) * HEAD_DIM)
        qg = jnp.concatenate([q[:, (g * group + j) * HEAD_DIM:(g * group + j + 1) * HEAD_DIM]
                              for j in range(group)], axis=0)
        sink = jnp.zeros((rows, 1), F32)
        for j in range(group):
            sink = jnp.where(rid // TILE == j, sink_ref[g * group + j], sink)
        s_c = _nt_dot(qg, kc[:, ksl])
        s_l = _nt_dot(qg, kl[:, ksl]) + mask
        mx = jnp.maximum(jnp.maximum(jnp.max(s_c, axis=1, keepdims=True), jnp.max(s_l, axis=1, keepdims=True)), sink)
        p_c = jnp.exp(s_c - mx)
        p_l = jnp.exp(s_l - mx)
        den = jnp.sum(p_c, axis=1, keepdims=True) + jnp.sum(p_l, axis=1, keepdims=True) + jnp.exp(sink - mx)
        o = (_dot(p_c.astype(BF16), vc[:, ksl]) + _dot(p_l.astype(BF16), vl[:, ksl])) / den
        outs.extend(o[j * TILE:(j + 1) * TILE] for j in range(group))
    o_ref[0] = jnp.concatenate(outs, axis=1).astype(BF16)


def _swa_mask_table():
    qpos = np.arange(TILE)[:, None]
    kpos = np.arange(3 * TILE)[None, :] - TILE
    near = np.abs(qpos - kpos) <= SWA_WINDOW
    kinds = [np.zeros_like(near), near & (kpos >= 0), near, near & (kpos < TILE)]
    return jnp.asarray(np.where(np.stack(kinds), 0.0, NEG), F32)


def _tile_kind(i, nb):
    return jnp.where(i == 0, 0, jnp.where(i == 1, 1, jnp.where(i == nb, 3, 2)))


def _swa_attention(q, k, v, sink):
    b, m, wq = q.shape
    wk = k.shape[2]
    nt = m // TILE
    nb = nt - 1

    def lat(off):
        return lambda bb, i: (bb, jnp.clip(i - 1 + off, 0, nb - 1) + 1, 0)

    blk = lambda w, f: pl.BlockSpec((1, TILE, w), f)
    kv_specs = [blk(wk, lambda bb, i: (bb, 0, 0)), blk(wk, lat(-1)), blk(wk, lat(0)), blk(wk, lat(1))]
    return pl.pallas_call(
        _swa_kernel,
        out_shape=jax.ShapeDtypeStruct((b, m, wq), BF16),
        grid=(b, nt),
        in_specs=[pl.BlockSpec(memory_space=pltpu.SMEM), blk(wq, lambda bb, i: (bb, i, 0))] + kv_specs + kv_specs
                 + [pl.BlockSpec((1, TILE, 3 * TILE), lambda bb, i: (_tile_kind(i, nb), 0, 0))],
        out_specs=blk(wq, lambda bb, i: (bb, i, 0)),
        compiler_params=_params(2),
        name="windowed_attention",
    )(sink, q, k, k, k, k, v, v, v, v, _swa_mask_table())


def _diff_kernel(lam_ref, q_ref, k_ref, v_ref, g_ref, o_ref, acc_ref, m_ref, s_ref, cmax_ref,
                 *, n_chunks, kblk, fast_kblk, lam_init):
    i = pl.program_id(2)

    def stacked(half):
        q = q_ref[0, half * TILE:(half + 1) * TILE]
        lane = lax.broadcasted_iota(I32, q.shape, 1)
        zero = jnp.zeros_like(q)
        return jnp.concatenate([jnp.where(lane < HEAD_DIM, q, zero), jnp.where(lane >= HEAD_DIM, q, zero)], axis=0)

    def chunk(c):
        return pl.ds(pl.multiple_of(c * kblk, LANES), kblk)

    def accumulate(s, cmax, v):
        m_prev = m_ref[...]
        m_new = jnp.maximum(m_prev, cmax)
        p = jnp.exp2(s - m_new).astype(BF16)
        acc_ref[...] = jnp.exp2(m_prev - m_new) * acc_ref[...] + _dot(p, v)
        m_ref[...] = m_new

    def reset():
        m_ref[...] = jnp.full(m_ref.shape, NEG, F32)
        acc_ref[...] = jnp.zeros(acc_ref.shape, F32)

    def safe_sweep(qq):
        def scores(c, slot):
            s = _nt_dot(qq, k_ref[0, chunk(c), :])
            s_ref[slot] = s
            cmax_ref[slot] = jnp.max(s, axis=1, keepdims=True)

        reset()
        scores(0, 0)

        def pair(c, last):
            for cur in (0, 1):
                if not (last and cur == 1):
                    scores(c + cur + 1, 1 - cur)
                accumulate(s_ref[cur], cmax_ref[cur], v_ref[0, chunk(c + cur), :])

        def body(c2, carry):
            pair(2 * c2, False)
            return carry

        lax.fori_loop(0, n_chunks // 2 - 1, body, 0)
        pair(n_chunks - 2, True)
        return acc_ref[...]

    def fast_sweep(qq):
        keys = lambda c: slice(c * fast_kblk, (c + 1) * fast_kblk)
        s = _nt_dot(qq, k_ref[0, keys(0), :])
        m0 = jnp.max(s, axis=1, keepdims=True)
        acc = _dot(jnp.exp2(s - m0).astype(BF16), v_ref[0, keys(0), :])
        for c in range(1, n_chunks * kblk // fast_kblk):
            s = _nt_dot(qq, k_ref[0, keys(c), :])
            acc = acc + _dot(jnp.exp2(s - m0).astype(BF16), v_ref[0, keys(c), :])
        return acc

    def finish(acc):
        o = acc[:, :DIFF_V] / acc[:, DIFF_V:]
        d = o[:TILE] - lam_ref[0] * o[TILE:]
        y = d * lax.rsqrt(jnp.mean(d * d, axis=-1, keepdims=True) + EPS) * g_ref[...]
        return (y * (1.0 - lam_init)).astype(BF16)

    def rows(half):
        return pl.ds(half * TILE, TILE)

    @pl.when(i == 0)
    def _():
        o_ref[0, rows(0)] = jnp.zeros((TILE, DIFF_V), BF16)
        reset()
        s = _nt_dot(stacked(1), k_ref[0, 0:CTX, :])
        accumulate(s, jnp.max(s, axis=1, keepdims=True), v_ref[0, 0:CTX, :])
        o_ref[0, rows(1)] = finish(acc_ref[...])

    @pl.when(i > 0)
    def _():
        accs = [fast_sweep(stacked(half)) for half in (0, 1)]
        for half in (0, 1):
            o_ref[0, rows(half)] = finish(accs[half])
        bad = [jnp.logical_not(jnp.max(acc[:, DIFF_V:]) < DIFF_MAX_DENOMINATOR) for acc in accs]
        for half in (0, 1):
            @pl.when(bad[half])
            def _(half=half):
                o_ref[0, rows(half)] = finish(safe_sweep(stacked(half)))


def _diff_chunking(m, max_chunk=2048):
    for n_chunks in range(2, m // LANES + 1, 2):
        if m % (n_chunks * LANES) == 0 and m // n_chunks <= max_chunk:
            return n_chunks, m // n_chunks
    raise ValueError(f"no chunking for {m} keys")


def _diff_attention(q, k, v, lam, g, lam_init):
    b, m, _ = q.shape
    n_chunks, kblk = _diff_chunking(m)
    fast_kblk = max(k for k in range(MXU_TILE, 2048 + 1, MXU_TILE) if m % k == 0)
    q_pad = jnp.pad(q, ((0, 0), (TILE, 0), (0, 0)))
    out = pl.pallas_call(
        functools.partial(_diff_kernel, n_chunks=n_chunks, kblk=kblk, fast_kblk=fast_kblk, lam_init=lam_init),
        out_shape=jax.ShapeDtypeStruct((b, m + TILE, DIFF_HEADS * DIFF_V), BF16),
        grid=(b, DIFF_HEADS, (m + TILE) // (2 * TILE)),
        in_specs=[pl.BlockSpec(memory_space=pltpu.SMEM),
                  pl.BlockSpec((1, 2 * TILE, 2 * HEAD_DIM), lambda bb, hh, i: (bb, i, hh)),
                  pl.BlockSpec((1, m, 2 * HEAD_DIM), lambda bb, hh, i: (bb, 0, hh)),
                  pl.BlockSpec((1, m, 2 * DIFF_V), lambda bb, hh, i: (bb, 0, hh)),
                  pl.BlockSpec((1, DIFF_V), lambda bb, hh, i: (0, 0))],
        out_specs=pl.BlockSpec((1, 2 * TILE, DIFF_V), lambda bb, hh, i: (bb, i, hh)),
        scratch_shapes=[pltpu.VMEM((2 * TILE, 2 * DIFF_V), F32), pltpu.VMEM((2 * TILE, 1), F32),
                        pltpu.VMEM((2, 2 * TILE, kblk), F32), pltpu.VMEM((2, 2 * TILE, 1), F32)],
        compiler_params=_params(3),
        name="differential_attention",
    )(lam, q_pad, k, v, g)
    return out[:, TILE:]


def _merge_kernel(ona_ref, od_ref, osw_ref, gs_ref, x_ref, mod_ref, wna_ref, wd_ref, wsw_ref, wo_ref,
                  g2_ref, wrt_ref, xo_ref, h2_ref, afft_ref):
    d = x_ref.shape[2]
    gs = gs_ref[0]
    y = (gs[:, :d].astype(F32) * _dot(ona_ref[0], wna_ref[...])
         + gs[:, d:2 * d].astype(F32) * _dot(od_ref[0], wd_ref[...])
         + gs[:, 2 * d:].astype(F32) * _dot(osw_ref[0], wsw_ref[...]))
    mod = _row_modulation(mod_ref, x_ref.shape[1])
    xn = x_ref[0] + mod(2) * _dot(y.astype(BF16), wo_ref[...])
    xo_ref[0] = xn
    r = xn * lax.rsqrt(jnp.mean(xn * xn, axis=-1, keepdims=True) + EPS) * g2_ref[...]
    h2 = r * (1.0 + mod(4)) + mod(3)
    h2_ref[0] = h2.astype(BF16)
    lt = lax.dot_general(wrt_ref[...], h2, (((1,), (1,)), ((), ())),
                         preferred_element_type=F32, precision=HIGHEST)
    et = jnp.exp(lt - jnp.max(lt, axis=0, keepdims=True))
    afft_ref[0] = et / jnp.sum(et, axis=0, keepdims=True)


def _merge(o_na, o_d, o_sw, gs, xu, modtab, w_na, w_d, w_sw, w_o, g2, w_r):
    b, m, d = xu.shape
    tm = _matmul_rows(m, 640)
    ne = w_r.shape[1]
    row = lambda w: pl.BlockSpec((1, tm, w), lambda bb, i: (bb, i, 0))
    full = lambda a: pl.BlockSpec(a.shape, lambda bb, i: (0,) * a.ndim)
    w_rt = w_r.T
    return pl.pallas_call(
        _merge_kernel,
        out_shape=[jax.ShapeDtypeStruct((b, m, d), F32), jax.ShapeDtypeStruct((b, m, d), BF16),
                   jax.ShapeDtypeStruct((b, ne, m), F32)],
        grid=(b, m // tm),
        in_specs=[row(o_na.shape[2]), row(o_d.shape[2]), row(o_sw.shape[2]), row(gs.shape[2]), row(d),
                  pl.BlockSpec((1, 2, N_MOD, d), lambda bb, i: (bb, 0, 0, 0)),
                  full(w_na), full(w_d), full(w_sw), full(w_o), full(g2), full(w_rt)],
        out_specs=[row(d), row(d), pl.BlockSpec((1, ne, tm), lambda bb, i: (bb, 0, i))],
        compiler_params=_params(2),
        name="merge_residual_router",
    )(o_na, o_d, o_sw, gs, xu, modtab, w_na, w_d, w_sw, w_o, g2, w_rt)


def _route_kernel(a_ref, gpos_ref, gfull_ref, *, n_latent_sets, cap_lat, cap_ctx):
    ne, c, w = a_ref.shape[1:]
    a = a_ref[0]
    bits = lax.bitcast_convert_type(a, I32)
    cap = jnp.where(pl.program_id(0) < n_latent_sets, cap_lat, cap_ctx).astype(F32)

    def count(mask):
        return jnp.sum(jnp.sum(mask.astype(F32), axis=2, keepdims=True), axis=1, keepdims=True)

    def search(it, thr):
        cand = thr | jnp.left_shift(jnp.int32(1), 30 - it)
        return jnp.where(count(bits >= cand) >= cap, cand, thr)

    thr = lax.fori_loop(0, 31, search, jnp.zeros((ne, 1, 1), I32))
    gt = bits > thr
    eq = bits == thr
    need = cap - count(gt)

    upper = (lax.broadcasted_iota(I32, (w, w), 0) <= lax.broadcasted_iota(I32, (w, w), 1)).astype(BF16)
    lower = (lax.broadcasted_iota(I32, (c, c), 1) < lax.broadcasted_iota(I32, (c, c), 0)).astype(BF16)

    def exclusive_cumsum(mask):
        x = mask.astype(F32).reshape(ne * c, w)
        within = _dot(x.astype(BF16), upper)
        tot = jnp.broadcast_to(within[:, w - 1:w], (ne * c, w)).astype(BF16)
        before = jnp.concatenate([_dot(lower, tot[e * c:(e + 1) * c]) for e in range(ne)], axis=0)
        return (before + within - x).reshape(ne, c, w)

    sel = gt | (eq & (exclusive_cumsum(eq) < need))
    g = exclusive_cumsum(sel).astype(I32)
    gfull_ref[0] = g
    gpos_ref[0] = jnp.where(sel, g, -1)


def _route(aff_sets, n_latent_sets, cap_lat, cap_ctx):
    s, ne, n = aff_sets.shape
    c = n // LANES
    a4 = aff_sets.reshape(s, ne, c, LANES)
    blk = pl.BlockSpec((1, ne, c, LANES), lambda i: (i, 0, 0, 0))
    gpos, gfull = pl.pallas_call(
        functools.partial(_route_kernel, n_latent_sets=n_latent_sets, cap_lat=cap_lat, cap_ctx=cap_ctx),
        out_shape=[jax.ShapeDtypeStruct(a4.shape, I32)] * 2,
        grid=(s,),
        in_specs=[blk],
        out_specs=[blk, blk],
        compiler_params=_params(1),
        name="expert_choice_select",
    )(a4)
    return gpos.reshape(s, ne, n), gfull.reshape(s, ne, n)


def _window(cap):
    w = min(TILE + BF16_ROWS, cap)
    main = min(WINDOW_MAIN_ROWS, w)
    return main, w - main, cap - w


def _windows(gfull, cap):
    b, ne, _ = gfull.shape
    main, _, max_start = _window(cap)
    seg_start = gfull[:, :, ::TILE]
    seg_end = jnp.concatenate([gfull[:, :, TILE::TILE], jnp.full((b, ne, 1), cap, I32)], axis=2)
    astart = jnp.minimum(seg_start // BF16_ROWS * BF16_ROWS, max_start)
    return astart.reshape(-1), (seg_end > astart + main).astype(I32).reshape(-1)


def _gather_kernel(a_ref, t_ref, g_ref, aff_ref, h_ref, o_ref, w_ref, *, ne, nblk, eg, main, tail):
    b, egi, j = pl.program_id(0), pl.program_id(1), pl.program_id(2)

    @pl.when(j == 0)
    def _():
        o_ref[...] = jnp.zeros(o_ref.shape, o_ref.dtype)
        w_ref[...] = jnp.zeros(w_ref.shape, w_ref.dtype)

    hb = h_ref[0]
    t = hb.shape[0]

    def idx(k):
        return (b * ne + egi * eg + k) * nblk + j

    def place(experts, first, rows):
        hits = [lax.broadcasted_iota(I32, (rows, t), 0) == g_ref[0, k] - (a_ref[idx(k)] + first) for k in experts]
        onehot = jnp.concatenate([jnp.where(h, 1.0, 0.0).astype(BF16) for h in hits], axis=0)
        picked = _dot(onehot, hb).astype(BF16)
        for n, (k, hit) in enumerate(zip(experts, hits)):
            weight = jnp.sum(jnp.where(hit, aff_ref[0, k], 0.0), axis=1, keepdims=True)
            sl = (0, k, pl.ds(pl.multiple_of(a_ref[idx(k)] + first, BF16_ROWS), rows), slice(None))
            o_ref[sl] = o_ref[sl] + picked[n * rows:(n + 1) * rows]
            w_ref[sl] = w_ref[sl] + weight

    place(list(range(eg)), 0, main)
    for k in range(eg if tail else 0):
        @pl.when(t_ref[idx(k)] != 0)
        def _(k=k):
            place([k], main, tail)


def _gather(h2u, afft, gpos, astart, need_tail, cap, tile_off, eg):
    b, ne, n = gpos.shape
    m, d = h2u.shape[1:]
    nblk = n // TILE
    main, tail, _ = _window(cap)
    lists = lambda w: pl.BlockSpec((1, eg, cap, w), lambda bb, e, j, a, nt: (bb, e, 0, 0),
                                   pipeline_mode=pl.Buffered(1))
    return pl.pallas_call(
        functools.partial(_gather_kernel, ne=ne, nblk=nblk, eg=eg, main=main, tail=tail),
        out_shape=[jax.ShapeDtypeStruct((b, ne, cap, d), BF16), jax.ShapeDtypeStruct((b, ne, cap, 1), F32)],
        grid_spec=pltpu.PrefetchScalarGridSpec(
            num_scalar_prefetch=2,
            grid=(b, ne // eg, nblk),
            in_specs=[pl.BlockSpec((1, eg, 1, TILE), lambda bb, e, j, a, nt: (bb, e, 0, j)),
                      pl.BlockSpec((1, eg, 1, TILE), lambda bb, e, j, a, nt: (bb, e, 0, tile_off + j)),
                      pl.BlockSpec((1, TILE, d), lambda bb, e, j, a, nt: (bb, tile_off + j, 0))],
            out_specs=[lists(d), lists(1)]),
        compiler_params=_params(3),
        name="expert_gather",
    )(astart, need_tail, gpos.reshape(b, ne, 1, n), afft.reshape(b, ne, 1, m), h2u)


def _ffn_kernel(x_ref, rw_ref, wg_ref, wu_ref, wd_ref, o_ref, wg_s, wu_s, wd_s):
    @pl.when((pl.program_id(1) == 0) & (pl.program_id(2) == 0))
    def _():
        wg_s[...] = wg_ref[0, 0].astype(BF16)
        wu_s[...] = wu_ref[0, 0].astype(BF16)
        wd_s[...] = wd_ref[0, 0].astype(BF16)

    x = x_ref[0, 0]
    gate = _dot(x, wg_s[...])
    up = _dot(x, wu_s[...])
    hid = (gate / (1.0 + jnp.exp(-gate)) * up).astype(BF16)
    o_ref[0, 0] = (_dot(hid, wd_s[...]) * rw_ref[0, 0]).astype(BF16)


def _expert_ffn(xs, ws, w_gate, w_up, w_down, layer):
    b, ne, cap, d = xs.shape
    tr = min(cap, 512)
    wspec = lambda w: pl.BlockSpec((1, 1) + w.shape[2:], lambda e, bb, r: (layer, e, 0, 0))
    rows = lambda w: pl.BlockSpec((1, 1, tr, w), lambda e, bb, r: (bb, e, r, 0))
    return pl.pallas_call(
        _ffn_kernel,
        out_shape=jax.ShapeDtypeStruct(xs.shape, BF16),
        grid=(ne, b, cap // tr),
        in_specs=[rows(d), rows(1), wspec(w_gate), wspec(w_up), wspec(w_down)],
        out_specs=rows(d),
        scratch_shapes=[pltpu.VMEM(w.shape[2:], BF16) for w in (w_gate, w_up, w_down)],
        compiler_params=_params(3),
        name="expert_swiglu",
    )(xs, ws, w_gate, w_up, w_down)


def _combine_kernel(a_ref, t_ref, x_ref, g_ref, mod_ref, fg_ref, ye_ref, o_ref,
                    win_ref, tail_ref, tacc_ref, sem_ref, tsem_ref, *, ne, nblk, main, tail, final):
    b, j = pl.program_id(0), pl.program_id(1)
    step = b * nblk + j
    slot = step % 2

    def start_at(bb, jj, e):
        return pl.multiple_of(a_ref[(bb * ne + e) * nblk + jj], BF16_ROWS)

    def start_of(e):
        return start_at(b, j, e)

    def window_copy(bb, jj, e, sl):
        return pltpu.make_async_copy(ye_ref.at[bb, e, pl.ds(start_at(bb, jj, e), main)],
                                     win_ref.at[sl, e * main:(e + 1) * main], sem_ref.at[sl, e])

    @pl.when(step == 0)
    def _():
        for e in range(ne):
            window_copy(b, j, e, slot).start()

    @pl.when(step + 1 < pl.num_programs(0) * nblk)
    def _():
        nxt = step + 1
        for e in range(ne):
            window_copy(nxt // nblk, nxt % nblk, e, 1 - slot).start()

    t = x_ref.shape[1]

    def pick(experts, first, rows, windows):
        onehot = jnp.concatenate(
            [jnp.where(lax.broadcasted_iota(I32, (rows, t), 0) == g_ref[0, e] - (start_of(e) + first), 1.0, 0.0)
             .astype(BF16) for e in experts], axis=0)
        return lax.dot_general(onehot, windows, (((0,), (0,)), ((), ())), preferred_element_type=F32)

    if tail:
        tacc_ref[...] = jnp.zeros(tacc_ref.shape, F32)
        for e in range(ne):
            @pl.when(t_ref[(b * ne + e) * nblk + j] != 0)
            def _(e=e):
                cp = pltpu.make_async_copy(ye_ref.at[b, e, pl.ds(start_of(e) + main, tail)], tail_ref, tsem_ref.at[0])
                cp.start()
                cp.wait()
                tacc_ref[...] += pick([e], main, tail, tail_ref[...])

    for e in range(ne):
        window_copy(b, j, e, slot).wait()
    acc = pick(list(range(ne)), 0, main, win_ref[slot])
    if tail:
        acc = acc + tacc_ref[...]
    x = x_ref[0] + mod_ref[0, 0][5:6] * acc
    if final:
        x = x * lax.rsqrt(jnp.mean(x * x, axis=-1, keepdims=True) + EPS) * fg_ref[...]
    o_ref[0] = x


def _combine(xu, gpos, astart, need_tail, ye, modtab, final_g, tile_off, kind, final):
    b, m, d = xu.shape
    _, ne, n = gpos.shape
    cap = ye.shape[2]
    nblk = n // TILE
    main, tail, _ = _window(cap)
    out_spec = pl.BlockSpec((1, TILE, d), (lambda bb, j, a, nt: (bb, j, 0)) if final
                            else (lambda bb, j, a, nt: (bb, tile_off + j, 0)))
    return pl.pallas_call(
        functools.partial(_combine_kernel, ne=ne, nblk=nblk, main=main, tail=tail, final=final),
        out_shape=jax.ShapeDtypeStruct((b, n, d) if final else xu.shape, F32),
        grid_spec=pltpu.PrefetchScalarGridSpec(
            num_scalar_prefetch=2,
            grid=(b, nblk),
            in_specs=[pl.BlockSpec((1, TILE, d), lambda bb, j, a, nt: (bb, tile_off + j, 0)),
                      pl.BlockSpec((1, ne, 1, TILE), lambda bb, j, a, nt: (bb, 0, 0, j)),
                      pl.BlockSpec((1, 1, N_MOD, d), lambda bb, j, a, nt: (bb, kind, 0, 0)),
                      pl.BlockSpec((1, d), lambda bb, j, a, nt: (0, 0)),
                      pl.BlockSpec(memory_space=pl.ANY)],
            out_specs=out_spec,
            scratch_shapes=[pltpu.VMEM((2, ne * main, d), BF16), pltpu.VMEM((max(tail, BF16_ROWS), d), BF16),
                            pltpu.VMEM((TILE, d), F32), pltpu.SemaphoreType.DMA((2, ne)),
                            pltpu.SemaphoreType.DMA((1,))]),
        input_output_aliases={} if final else {2: 0},
        compiler_params=_params(2),
        name="expert_combine",
    )(astart, need_tail, xu, gpos.reshape(b, ne, 1, n), modtab, final_g, ye)


def _rope_tables(n):
    t = np.arange(n)
    row = (t // GRID_W).astype(np.float32)[:, None]
    col = (t % GRID_W).astype(np.float32)[:, None]
    inv = (ROPE_BASE ** (-np.arange(ROPE_PAIRS, dtype=np.float32) / ROPE_PAIRS)).astype(np.float32)
    ang = np.concatenate([row * inv, row * inv, col * inv, col * inv], axis=-1)
    cos, sin = np.cos(ang), np.sin(ang)
    half = (np.arange(HEAD_DIM) % (2 * ROPE_PAIRS)) < ROPE_PAIRS
    sin = np.where(half[None], -sin, sin)
    cos = np.concatenate([np.ones((CTX, HEAD_DIM)), cos], axis=0)
    sin = np.concatenate([np.zeros((CTX, HEAD_DIM)), sin], axis=0)
    tile2 = lambda a: jnp.asarray(np.concatenate([a, a], axis=1), F32)
    return tile2(cos), tile2(sin)


def _moe(xu, h2u, afft, modtab, weights, layer, final_g, final):
    b, m, d = xu.shape
    n = m - CTX
    w_gate, w_up, w_down = weights
    cap_lat = max(1, EC_CAPACITY * n // N_EXPERTS)
    cap_ctx = max(1, EC_CAPACITY * CTX // N_EXPERTS)
    sets = [afft[:, :, CTX:]]
    if not final:
        sets.append(jnp.concatenate([afft[:, :, :CTX], jnp.full((b, N_EXPERTS, n - CTX), -1.0, F32)], axis=2))
    gpos, gfull = _route(jnp.concatenate(sets, axis=0), b, cap_lat, cap_ctx)

    def run(idx, n_set, cap, tile_off, kind, x_in, fin):
        gp = gpos[idx * b:(idx + 1) * b, :, :n_set]
        gf = gfull[idx * b:(idx + 1) * b, :, :n_set]
        astart, need_tail = _windows(gf, cap)
        xs, ws = _gather(h2u, afft, gp, astart, need_tail, cap, tile_off, eg=8)
        ye = _expert_ffn(xs, ws, w_gate, w_up, w_down, layer)
        return _combine(x_in, gp, astart, need_tail, ye, modtab, final_g, tile_off, kind, fin)

    if final:
        return run(0, n, cap_lat, 1, 1, xu, True)
    xu = run(0, n, cap_lat, 1, 1, xu, False)
    return run(1, CTX, cap_ctx, 0, 0, xu, False)


def kernel(x, c, ctx, c_ctx, w_mod, b_mod, norm1_g, w_in, na_rpb, diff_lambda, diff_subln_g, swa_sink,
           w_branch_na, w_branch_diff, w_branch_swa, w_out, norm2_g, w_router, w_expert_gate, w_expert_up,
           w_expert_down, final_g):
    b, n, d = x.shape
    depth = w_mod.shape[0]
    assert ctx.shape[1] == CTX and d == D_MODEL and n % (2 * TILE) == 0 and n // TILE >= 3

    cin = jnp.concatenate([c, c_ctx[None], jnp.zeros((8 - b - 1, d), F32)], axis=0)
    mod_all = _modulation(cin, w_mod, b_mod)
    cos_u, sin_u = _rope_tables(n)
    xu = jnp.concatenate([ctx, x], axis=1)
    final_g2 = final_g.reshape(1, d)

    out = None
    for l in range(depth):
        final = l == depth - 1
        lam_init = 0.8 - 0.6 * math.exp(-0.3 * l)
        mod_l = mod_all[l].reshape(8, N_MOD, d)
        modtab = jnp.stack([jnp.broadcast_to(mod_l[b], (b, N_MOD, d)), mod_l[:b]], axis=1)
        lam_p = diff_lambda[l].astype(F32)
        lam = (jnp.exp(jnp.sum(lam_p[0] * lam_p[1])) - jnp.exp(jnp.sum(lam_p[2] * lam_p[3])) + lam_init).reshape(1)

        (q_na, k_na, v_na, q_d, k_d, v_d, q_s, k_s, v_s, gs) = _project(
            xu, modtab, norm1_g[l].reshape(1, d), w_in[l].astype(BF16), cos_u, sin_u)
        o_na = _na_attention(q_na, k_na, v_na, _na_bias_table(na_rpb[l]))
        o_d = _diff_attention(q_d, k_d, v_d, lam, diff_subln_g[l].reshape(1, DIFF_V), lam_init)
        o_s = _swa_attention(q_s, k_s, v_s, swa_sink[l].astype(F32))
        xu, h2u, afft = _merge(
            o_na, o_d, o_s, gs, xu, modtab, w_branch_na[l].astype(BF16), w_branch_diff[l].astype(BF16),
            w_branch_swa[l].astype(BF16), w_out[l].astype(BF16), norm2_g[l].reshape(1, d), w_router[l])
        weights = (w_expert_gate, w_expert_up, w_expert_down)
        res = _moe(xu, h2u, afft, modtab, weights, l, final_g2, final)
        if final:
            out = res
        else:
            xu = res
    return out
```

```python
import functools
import math

import numpy as np
import jax
import jax.numpy as jnp
from jax import lax
from jax.experimental import pallas as pl
from jax.experimental.pallas import tpu as pltpu

F32 = jnp.float32
BF16 = jnp.bfloat16
I32 = jnp.int32
HIGHEST = lax.Precision.HIGHEST

D_MODEL = 1024
CTX = 256
TILE = 256
GRID_W = 64
HEAD_DIM = 64
ROPE_PAIRS = HEAD_DIM // 4
ROPE_BASE = 10000.0
EPS = 1e-6
N_MOD = 6
NA_HEADS = 4
NA_WIN_ROWS = 8
NA_WIN_COLS = 16
NA_TILE_ROWS = TILE // GRID_W
DIFF_HEADS = 4
DIFF_V = 2 * HEAD_DIM
SWA_HEADS = 4
SWA_KV_HEADS = 2
SWA_WINDOW = 128
N_EXPERTS = 16
EC_CAPACITY = 2
NEG = -1e30
LANES = 128
MXU_TILE = 256
LOG2E = math.log2(math.e)
DIFF_TILES_PER_STEP = 2
DIFF_MAX_DENOMINATOR = 2.0 ** 64
BF16_ROWS = 16
WINDOW_MAIN_ROWS = 64

C_QNA, C_QD, C_QS = 0, 256, 768
C_KNA, C_KD, C_KS = 1024, 1280, 1792
C_VNA, C_VD, C_VS = 1920, 2176, 2688
C_GATE, C_END = 2816, 5888

VMEM_LIMIT = 56 * 1024 * 1024


def _params(n_axes, vmem=VMEM_LIMIT):
    return pltpu.CompilerParams(dimension_semantics=("arbitrary",) * n_axes, vmem_limit_bytes=vmem)


def _nt_dot(a, b):
    return lax.dot_general(a, b, (((1,), (1,)), ((), ())), preferred_element_type=F32)


def _dot(a, b):
    return jnp.dot(a, b, preferred_element_type=F32)


def _mod_kernel(c_ref, w_ref, b_ref, o_ref):
    c = c_ref[...]
    s = c / (1.0 + jnp.exp(-c))
    o_ref[0] = jnp.dot(s, w_ref[0], preferred_element_type=F32, precision=HIGHEST) + b_ref[0]


def _modulation(cin, w_mod, b_mod):
    depth, d, w = w_mod.shape
    tn = 1024
    return pl.pallas_call(
        _mod_kernel,
        out_shape=jax.ShapeDtypeStruct((depth, 8, w), F32),
        grid=(depth, w // tn),
        in_specs=[pl.BlockSpec((8, d), lambda l, j: (0, 0)),
                  pl.BlockSpec((1, d, tn), lambda l, j: (l, 0, j)),
                  pl.BlockSpec((1, 1, tn), lambda l, j: (l, 0, j))],
        out_specs=pl.BlockSpec((1, 8, tn), lambda l, j: (l, 0, j)),
        compiler_params=_params(2),
        name="modulation",
    )(cin, w_mod, b_mod.reshape(depth, 1, w))


def _row_modulation(mod_ref, rows):
    first = pl.program_id(1) * rows
    is_ctx = first + lax.broadcasted_iota(I32, (rows, 1), 0) < CTX
    return lambda k: jnp.where(is_ctx, mod_ref[0, 0, k:k + 1], mod_ref[0, 1, k:k + 1])


def _proj_kernel(x_ref, mod_ref, g_ref, w_ref, cos_ref, sin_ref,
                 qna_ref, kna_ref, vna_ref, qd_ref, kd_ref, vd_ref, qs_ref, ks_ref, vs_ref, gs_ref):
    x = x_ref[0]
    mod = _row_modulation(mod_ref, x.shape[0])
    y = x * lax.rsqrt(jnp.mean(x * x, axis=-1, keepdims=True) + EPS) * g_ref[...]
    h = (y * (1.0 + mod(1)) + mod(0)).astype(BF16)

    def proj(a, b):
        return _dot(h, w_ref[:, a:b])

    cos = cos_ref[...]
    sin = sin_ref[...]
    lane = lax.broadcasted_iota(I32, cos.shape, 1)
    first_half = (lane % (2 * ROPE_PAIRS)) < ROPE_PAIRS

    def rope(t):
        outs = []
        for j in range(t.shape[1] // LANES):
            c = t[:, j * LANES:(j + 1) * LANES]
            r = jnp.where(first_half, pltpu.roll(c, LANES - ROPE_PAIRS, 1), pltpu.roll(c, ROPE_PAIRS, 1))
            outs.append(c * cos + r * sin)
        return outs[0] if len(outs) == 1 else jnp.concatenate(outs, axis=1)

    scale = HEAD_DIM ** -0.5
    qna_ref[0] = (proj(C_QNA, C_QD) * scale).astype(BF16)
    qd_ref[0] = (rope(proj(C_QD, C_QS)) * (scale * LOG2E)).astype(BF16)
    qs_ref[0] = (rope(proj(C_QS, C_KNA)) * scale).astype(BF16)
    kna_ref[0] = proj(C_KNA, C_KD).astype(BF16)
    kd_ref[0] = rope(proj(C_KD, C_KS)).astype(BF16)
    ks_ref[0] = rope(proj(C_KS, C_VNA)).astype(BF16)
    vna_ref[0] = proj(C_VNA, C_VD).astype(BF16)
    vd = proj(C_VD, C_VS).astype(BF16)
    ones = jnp.ones((vd.shape[0], DIFF_V), BF16)
    vd_ref[0] = jnp.concatenate(
        [blk for hd in range(DIFF_HEADS) for blk in (vd[:, hd * DIFF_V:(hd + 1) * DIFF_V], ones)], axis=1)
    vs_ref[0] = proj(C_VS, C_GATE).astype(BF16)
    gates = proj(C_GATE, C_END)
    gs_ref[0] = (1.0 / (1.0 + jnp.exp(-gates))).astype(BF16)


def _matmul_rows(m, limit):
    return max(t for t in range(LANES, limit + 1, LANES) if m % t == 0)


def _project(xu, modtab, g, w_in, cos_u, sin_u):
    b, m, d = xu.shape
    tm = _matmul_rows(m, 256)
    widths =(256, 256, 256, 512, 512, 2 * DIFF_HEADS * DIFF_V, 256, 128, 128, C_END - C_GATE)
    row = lambda w: pl.BlockSpec((1, tm, w), lambda bb, i: (bb, i, 0))
    return pl.pallas_call(
        _proj_kernel,
        out_shape=[jax.ShapeDtypeStruct((b, m, w), BF16) for w in widths],
        grid=(b, m // tm),
        in_specs=[row(d),
                  pl.BlockSpec((1, 2, N_MOD, d), lambda bb, i: (bb, 0, 0, 0)),
                  pl.BlockSpec((1, d), lambda bb, i: (0, 0)),
                  pl.BlockSpec((d, C_END), lambda bb, i: (0, 0)),
                  pl.BlockSpec((tm, LANES), lambda bb, i: (i, 0)),
                  pl.BlockSpec((tm, LANES), lambda bb, i: (i, 0))],
        out_specs=[row(w) for w in widths],
        compiler_params=_params(2),
        name="norm_project_rope",
    )(xu, modtab, g, w_in, cos_u, sin_u)


def _tile_kind(t, nb):
    return jnp.where(t == 0, 0, jnp.where(t == 1, 1, jnp.where(t == nb, 3, 2)))


def _local_tiles(prev_ref, cur_ref, next_ref, u):
    tiles = cur_ref.shape[1] // TILE

    def tile(s):
        if s < 0:
            return prev_ref[0]
        if s >= tiles:
            return next_ref[0]
        return cur_ref[0, s * TILE:(s + 1) * TILE]

    return jnp.concatenate([tile(u - 1), tile(u), tile(u + 1)], axis=0)


def _na_kernel(q_ref, kc_ref, kp_ref, kcur_ref, kn_ref, vc_ref, vp_ref, vcur_ref, vn_ref, bias_ref, o_ref, *, nb):
    tiles = q_ref.shape[1] // TILE
    kc, vc = kc_ref[0], vc_ref[0]
    for u in range(tiles):
        rows = slice(u * TILE, (u + 1) * TILE)
        kind = _tile_kind(pl.program_id(1) * tiles + u, nb)
        q = q_ref[0, rows]
        kl = _local_tiles(kp_ref, kcur_ref, kn_ref, u)
        vl = _local_tiles(vp_ref, vcur_ref, vn_ref, u)
        outs = []
        for hd in range(NA_HEADS):
            sl = slice(hd * HEAD_DIM, (hd + 1) * HEAD_DIM)
            qh = q[:, sl]
            s_c = _nt_dot(qh, kc[:, sl])
            s_l = _nt_dot(qh, kl[:, sl]) + bias_ref[kind, hd]
            mx = jnp.maximum(jnp.max(s_c, axis=1, keepdims=True), jnp.max(s_l, axis=1, keepdims=True))
            p_c = jnp.exp(s_c - mx)
            p_l = jnp.exp(s_l - mx)
            den = jnp.sum(p_c, axis=1, keepdims=True) + jnp.sum(p_l, axis=1, keepdims=True)
            o = _dot(p_c.astype(BF16), vc[:, sl]) + _dot(p_l.astype(BF16), vl[:, sl])
            outs.append(o / den)
        o_ref[0, rows] = jnp.concatenate(outs, axis=1).astype(BF16)


def _na_bias_table(rpb):
    tr, nk = NA_TILE_ROWS, 3 * NA_TILE_ROWS
    qc = np.arange(GRID_W)[:, None]
    kc = np.arange(GRID_W)[None, :]
    cstart = np.clip(qc - NA_WIN_COLS // 2, 0, GRID_W - NA_WIN_COLS)
    col_ok = (kc >= cstart) & (kc < cstart + NA_WIN_COLS)
    dc = np.clip(kc - qc, -(NA_WIN_COLS - 1), NA_WIN_COLS - 1) + NA_WIN_COLS - 1
    onehot = jnp.asarray(np.arange(2 * NA_WIN_COLS - 1)[:, None, None] == dc[None], F32)
    cols = jnp.einsum("hrc,cqk->hrqk", rpb.astype(F32), onehot, precision=HIGHEST)
    qr = np.arange(tr)[:, None]
    krow = np.arange(nk)[None, :] - tr
    dr = np.clip(krow - qr, -(NA_WIN_ROWS - 1), NA_WIN_ROWS - 1) + NA_WIN_ROWS - 1
    starts = (0 * qr, qr - NA_WIN_ROWS // 2, 0 * qr + tr - NA_WIN_ROWS)
    row_ok = np.stack([(krow >= st) & (krow < st + NA_WIN_ROWS) for st in starts])
    t = jnp.take(cols, jnp.asarray(dr.reshape(-1)), axis=1)
    t = t.reshape(NA_HEADS, tr, nk, GRID_W, GRID_W).transpose(0, 1, 3, 2, 4)
    ok = row_ok[:, None, :, None, :, None] & col_ok[None, None, None, :, None, :]
    t = jnp.where(ok, t[None], NEG).reshape(3, NA_HEADS, TILE, 3 * TILE)
    return jnp.concatenate([jnp.full_like(t[:1], NEG), t], axis=0)


def _local_attention_specs(nt, wq, wk):
    nb = nt - 1
    tiles = max(t for t in range(1, 5 + 1) if nt % t == 0)
    rows = pl.BlockSpec((1, tiles * TILE, wq), lambda bb, i: (bb, i, 0))
    one = lambda f: pl.BlockSpec((1, TILE, wk), f)
    kv = [one(lambda bb, i: (bb, 0, 0)),
          one(lambda bb, i: (bb, jnp.clip(i * tiles - 1, 1, nb), 0)),
          pl.BlockSpec((1, tiles * TILE, wk), lambda bb, i: (bb, i, 0)),
          one(lambda bb, i: (bb, jnp.clip((i + 1) * tiles, 1, nb), 0))]
    return tiles, rows, kv


def _resident(a):
    return pl.BlockSpec(a.shape, lambda bb, i: (0,) * a.ndim, pipeline_mode=pl.Buffered(1))


def _na_attention(q, k, v, bias):
    b, m, w = q.shape
    nt = m // TILE
    tiles, rows, kv_specs = _local_attention_specs(nt, w, w)
    return pl.pallas_call(
        functools.partial(_na_kernel, nb=nt - 1),
        out_shape=jax.ShapeDtypeStruct((b, m, w), BF16),
        grid=(b, nt // tiles),
        in_specs=[rows] + kv_specs + kv_specs + [_resident(bias)],
        out_specs=rows,
        compiler_params=_params(2),
        name="neighbourhood_attention",
    )(q, k, k, k, k, v, v, v, v, bias)


def _swa_kernel(sink_ref, q_ref, kc_ref, kp_ref, kcur_ref, kn_ref, vc_ref, vp_ref, vcur_ref, vn_ref, mask_ref,
                o_ref, *, nb):
    tiles = q_ref.shape[1] // TILE
    kc, vc = kc_ref[0], vc_ref[0]
    group = SWA_HEADS // SWA_KV_HEADS
    rows = group * TILE
    rid = lax.broadcasted_iota(I32, (rows, 1), 0)
    for u in range(tiles):
        kind = _tile_kind(pl.program_id(1) * tiles + u, nb)
        q = q_ref[0, u * TILE:(u + 1) * TILE]
        kl = _local_tiles(kp_ref, kcur_ref, kn_ref, u)
        vl = _local_tiles(vp_ref, vcur_ref, vn_ref, u)
        mask = jnp.concatenate([mask_ref[kind]] * group, axis=0)
        outs = []
        for g in range(SWA_KV_HEADS):
            ksl = slice(g * HEAD_DIM, (g + 1) * HEAD_DIM)
            qg = jnp.concatenate([q[:, (g * group + j) * HEAD_DIM:(g * group + j + 1) * HEAD_DIM]
                                  for j in range(group)], axis=0)
            sink = jnp.zeros((rows, 1), F32)
            for j in range(group):
                sink = jnp.where(rid // TILE == j, sink_ref[g * group + j], sink)
            s_c = _nt_dot(qg, kc[:, ksl])
            s_l = _nt_dot(qg, kl[:, ksl]) + mask
            mx = jnp.maximum(jnp.maximum(jnp.max(s_c, axis=1, keepdims=True), jnp.max(s_l, axis=1, keepdims=True)),
                             sink)
            p_c = jnp.exp(s_c - mx)
            p_l = jnp.exp(s_l - mx)
            den = jnp.sum(p_c, axis=1, keepdims=True) + jnp.sum(p_l, axis=1, keepdims=True) + jnp.exp(sink - mx)
            o = (_dot(p_c.astype(BF16), vc[:, ksl]) + _dot(p_l.astype(BF16), vl[:, ksl])) / den
            outs.extend(o[j * TILE:(j + 1) * TILE] for j in range(group))
        o_ref[0, u * TILE:(u + 1) * TILE] = jnp.concatenate(outs, axis=1).astype(BF16)


def _swa_mask_table():
    qpos = np.arange(TILE)[:, None]
    kpos = np.arange(3 * TILE)[None, :] - TILE
    near = np.abs(qpos - kpos) <= SWA_WINDOW
    kinds = [np.zeros_like(near), near & (kpos >= 0), near, near & (kpos < TILE)]
    return jnp.asarray(np.where(np.stack(kinds), 0.0, NEG), F32)


def _swa_attention(q, k, v, sink):
    b, m, wq = q.shape
    nt = m // TILE
    tiles, rows, kv_specs = _local_attention_specs(nt, wq, k.shape[2])
    mask = _swa_mask_table()
    return pl.pallas_call(
        functools.partial(_swa_kernel, nb=nt - 1),
        out_shape=jax.ShapeDtypeStruct((b, m, wq), BF16),
        grid=(b, nt // tiles),
        in_specs=[pl.BlockSpec(memory_space=pltpu.SMEM), rows] + kv_specs + kv_specs + [_resident(mask)],
        out_specs=rows,
        compiler_params=_params(2),
        name="windowed_attention",
    )(sink, q, k, k, k, k, v, v, v, v, mask)


def _diff_kernel(lam_ref, q_ref, k_ref, v_ref, g_ref, o_ref, acc_ref, m_ref, s_ref, cmax_ref,
                 *, n_chunks, kblk, fast_kblk, lam_init):
    i = pl.program_id(2)
    parts = q_ref.shape[1] // TILE

    def stacked(part):
        q = q_ref[0, part * TILE:(part + 1) * TILE]
        lane = lax.broadcasted_iota(I32, q.shape, 1)
        zero = jnp.zeros_like(q)
        return jnp.concatenate([jnp.where(lane < HEAD_DIM, q, zero), jnp.where(lane >= HEAD_DIM, q, zero)], axis=0)

    def chunk(c):
        return pl.ds(pl.multiple_of(c * kblk, LANES), kblk)

    def accumulate(s, cmax, v):
        m_prev = m_ref[...]
        m_new = jnp.maximum(m_prev, cmax)
        p = jnp.exp2(s - m_new).astype(BF16)
        acc_ref[...] = jnp.exp2(m_prev - m_new) * acc_ref[...] + _dot(p, v)
        m_ref[...] = m_new

    def reset():
        m_ref[...] = jnp.full(m_ref.shape, NEG, F32)
        acc_ref[...] = jnp.zeros(acc_ref.shape, F32)

    def safe_sweep(qq):
        def scores(c, slot):
            s = _nt_dot(qq, k_ref[0, chunk(c), :])
            s_ref[slot] = s
            cmax_ref[slot] = jnp.max(s, axis=1, keepdims=True)

        reset()
        scores(0, 0)

        def pair(c, last):
            for cur in (0, 1):
                if not (last and cur == 1):
                    scores(c + cur + 1, 1 - cur)
                accumulate(s_ref[cur], cmax_ref[cur], v_ref[0, chunk(c + cur), :])

        def body(c2, carry):
            pair(2 * c2, False)
            return carry

        lax.fori_loop(0, n_chunks // 2 - 1, body, 0)
        pair(n_chunks - 2, True)
        return acc_ref[...]

    def fast_sweep(qq):
        keys = lambda c: slice(c * fast_kblk, (c + 1) * fast_kblk)
        s = _nt_dot(qq, k_ref[0, keys(0), :])
        m0 = jnp.max(s, axis=1, keepdims=True)
        acc = _dot(jnp.exp2(s - m0).astype(BF16), v_ref[0, keys(0), :])
        for c in range(1, n_chunks * kblk // fast_kblk):
            s = _nt_dot(qq, k_ref[0, keys(c), :])
            acc = acc + _dot(jnp.exp2(s - m0).astype(BF16), v_ref[0, keys(c), :])
        return acc

    def finish(acc):
        o = acc[:, :DIFF_V] / acc[:, DIFF_V:]
        d = o[:TILE] - lam_ref[0] * o[TILE:]
        y = d * lax.rsqrt(jnp.mean(d * d, axis=-1, keepdims=True) + EPS) * g_ref[...]
        return (y * (1.0 - lam_init)).astype(BF16)

    def rows(part):
        return pl.ds(part * TILE, TILE)

    @pl.when(i == 0)
    def _():
        o_ref[0, 0:(parts - 1) * TILE] = jnp.zeros(((parts - 1) * TILE, DIFF_V), BF16)
        reset()
        s = _nt_dot(stacked(parts - 1), k_ref[0, 0:CTX, :])
        accumulate(s, jnp.max(s, axis=1, keepdims=True), v_ref[0, 0:CTX, :])
        o_ref[0, rows(parts - 1)] = finish(acc_ref[...])

    @pl.when(i > 0)
    def _():
        accs = [fast_sweep(stacked(part)) for part in range(parts)]
        for part in range(parts):
            o_ref[0, rows(part)] = finish(accs[part])
        bad = [jnp.logical_not(jnp.max(acc[:, DIFF_V:]) < DIFF_MAX_DENOMINATOR) for acc in accs]
        for part in range(parts):
            @pl.when(bad[part])
            def _(part=part):
                o_ref[0, rows(part)] = finish(safe_sweep(stacked(part)))


def _diff_chunking(m, max_chunk=2048):
    for n_chunks in range(2, m // LANES + 1, 2):
        if m % (n_chunks * LANES) == 0 and m // n_chunks <= max_chunk:
            return n_chunks, m // n_chunks
    raise ValueError(f"no chunking for {m} keys")


def _diff_attention(q, k, v, lam, g, lam_init):
    b, m, _ = q.shape
    n_chunks, kblk = _diff_chunking(m)
    fast_kblk = max(k for k in range(MXU_TILE, 2048 + 1, MXU_TILE) if m % k == 0)
    parts = DIFF_TILES_PER_STEP
    pad = (parts - 1) * TILE
    assert (m + pad) % (parts * TILE) == 0
    q_pad = jnp.pad(q, ((0, 0), (pad, 0), (0, 0)))
    out = pl.pallas_call(
        functools.partial(_diff_kernel, n_chunks=n_chunks, kblk=kblk, fast_kblk=fast_kblk, lam_init=lam_init),
        out_shape=jax.ShapeDtypeStruct((b, m + pad, DIFF_HEADS * DIFF_V), BF16),
        grid=(b, DIFF_HEADS, (m + pad) // (parts * TILE)),
        in_specs=[pl.BlockSpec(memory_space=pltpu.SMEM),
                  pl.BlockSpec((1, parts * TILE, 2 * HEAD_DIM), lambda bb, hh, i: (bb, i, hh)),
                  pl.BlockSpec((1, m, 2 * HEAD_DIM), lambda bb, hh, i: (bb, 0, hh)),
                  pl.BlockSpec((1, m, 2 * DIFF_V), lambda bb, hh, i: (bb, 0, hh)),
                  pl.BlockSpec((1, DIFF_V), lambda bb, hh, i: (0, 0))],
        out_specs=pl.BlockSpec((1, parts * TILE, DIFF_V), lambda bb, hh, i: (bb, i, hh)),
        scratch_shapes=[pltpu.VMEM((2 * TILE, 2 * DIFF_V), F32), pltpu.VMEM((2 * TILE, 1), F32),
                        pltpu.VMEM((2, 2 * TILE, kblk), F32), pltpu.VMEM((2, 2 * TILE, 1), F32)],
        compiler_params=_params(3),
        name="differential_attention",
    )(lam, q_pad, k, v, g)
    return out[:, pad:]


def _merge_kernel(ona_ref, od_ref, osw_ref, gs_ref, x_ref, mod_ref, wna_ref, wd_ref, wsw_ref, wo_ref,
                  g2_ref, wrt_ref, xo_ref, h2_ref, afft_ref):
    d = x_ref.shape[2]
    gs = gs_ref[0]
    y = (gs[:, :d].astype(F32) * _dot(ona_ref[0], wna_ref[...])
         + gs[:, d:2 * d].astype(F32) * _dot(od_ref[0], wd_ref[...])
         + gs[:, 2 * d:].astype(F32) * _dot(osw_ref[0], wsw_ref[...]))
    mod = _row_modulation(mod_ref, x_ref.shape[1])
    xn = x_ref[0] + mod(2) * _dot(y.astype(BF16), wo_ref[...])
    xo_ref[0] = xn
    r = xn * lax.rsqrt(jnp.mean(xn * xn, axis=-1, keepdims=True) + EPS) * g2_ref[...]
    h2 = r * (1.0 + mod(4)) + mod(3)
    h2_ref[0] = h2.astype(BF16)
    lt = lax.dot_general(wrt_ref[...], h2, (((1,), (1,)), ((), ())),
                         preferred_element_type=F32, precision=HIGHEST)
    et = jnp.exp(lt - jnp.max(lt, axis=0, keepdims=True))
    afft_ref[0] = et / jnp.sum(et, axis=0, keepdims=True)


def _merge(o_na, o_d, o_sw, gs, xu, modtab, w_na, w_d, w_sw, w_o, g2, w_r):
    b, m, d = xu.shape
    tm = _matmul_rows(m, 640)
    ne = w_r.shape[1]
    row = lambda w: pl.BlockSpec((1, tm, w), lambda bb, i: (bb, i, 0))
    full = lambda a: pl.BlockSpec(a.shape, lambda bb, i: (0,) * a.ndim)
    w_rt = w_r.T
    return pl.pallas_call(
        _merge_kernel,
        out_shape=[jax.ShapeDtypeStruct((b, m, d), F32), jax.ShapeDtypeStruct((b, m, d), BF16),
                   jax.ShapeDtypeStruct((b, ne, m), F32)],
        grid=(b, m // tm),
        in_specs=[row(o_na.shape[2]), row(o_d.shape[2]), row(o_sw.shape[2]), row(gs.shape[2]), row(d),
                  pl.BlockSpec((1, 2, N_MOD, d), lambda bb, i: (bb, 0, 0, 0)),
                  full(w_na), full(w_d), full(w_sw), full(w_o), full(g2), full(w_rt)],
        out_specs=[row(d), row(d), pl.BlockSpec((1, ne, tm), lambda bb, i: (bb, 0, i))],
        compiler_params=_params(2),
        name="merge_residual_router",
    )(o_na, o_d, o_sw, gs, xu, modtab, w_na, w_d, w_sw, w_o, g2, w_rt)


def _route_kernel(a_ref, gpos_ref, gfull_ref, *, n_latent_sets, cap_lat, cap_ctx):
    ne, c, w = a_ref.shape[1:]
    a = a_ref[0]
    bits = lax.bitcast_convert_type(a, I32)
    cap = jnp.where(pl.program_id(0) < n_latent_sets, cap_lat, cap_ctx).astype(F32)

    def count(mask):
        return jnp.sum(jnp.sum(mask.astype(F32), axis=2, keepdims=True), axis=1, keepdims=True)

    def search(it, thr):
        cand = thr | jnp.left_shift(jnp.int32(1), 30 - it)
        return jnp.where(count(bits >= cand) >= cap, cand, thr)

    thr = lax.fori_loop(0, 31, search, jnp.zeros((ne, 1, 1), I32))
    gt = bits > thr
    eq = bits == thr
    need = cap - count(gt)

    upper = (lax.broadcasted_iota(I32, (w, w), 0) <= lax.broadcasted_iota(I32, (w, w), 1)).astype(BF16)
    lower = (lax.broadcasted_iota(I32, (c, c), 1) < lax.broadcasted_iota(I32, (c, c), 0)).astype(BF16)

    def exclusive_cumsum(mask):
        x = mask.astype(F32).reshape(ne * c, w)
        within = _dot(x.astype(BF16), upper)
        tot = jnp.broadcast_to(within[:, w - 1:w], (ne * c, w)).astype(BF16)
        before = jnp.concatenate([_dot(lower, tot[e * c:(e + 1) * c]) for e in range(ne)], axis=0)
        return (before + within - x).reshape(ne, c, w)

    sel = gt | (eq & (exclusive_cumsum(eq) < need))
    g = exclusive_cumsum(sel).astype(I32)
    gfull_ref[0] = g
    gpos_ref[0] = jnp.where(sel, g, -1)


def _route(aff_sets, n_latent_sets, cap_lat, cap_ctx):
    s, ne, n = aff_sets.shape
    c = n // LANES
    a4 = aff_sets.reshape(s, ne, c, LANES)
    blk = pl.BlockSpec((1, ne, c, LANES), lambda i: (i, 0, 0, 0))
    gpos, gfull = pl.pallas_call(
        functools.partial(_route_kernel, n_latent_sets=n_latent_sets, cap_lat=cap_lat, cap_ctx=cap_ctx),
        out_shape=[jax.ShapeDtypeStruct(a4.shape, I32)] * 2,
        grid=(s,),
        in_specs=[blk],
        out_specs=[blk, blk],
        compiler_params=_params(1),
        name="expert_choice_select",
    )(a4)
    return gpos.reshape(s, ne, n), gfull.reshape(s, ne, n)


def _window(cap):
    w = min(TILE + BF16_ROWS, cap)
    main = min(WINDOW_MAIN_ROWS, w)
    return main, w - main, cap - w


def _windows(gfull, cap):
    b, ne, _ = gfull.shape
    main, _, max_start = _window(cap)
    seg_start = gfull[:, :, ::TILE]
    seg_end = jnp.concatenate([gfull[:, :, TILE::TILE], jnp.full((b, ne, 1), cap, I32)], axis=2)
    astart = jnp.minimum(seg_start // BF16_ROWS * BF16_ROWS, max_start)
    return astart.reshape(-1), (seg_end > astart + main).astype(I32).reshape(-1)


def _gather_kernel(a_ref, t_ref, g_ref, aff_ref, h_ref, o_ref, w_ref, *, ne, nblk, eg, main, tail):
    b, egi, j = pl.program_id(0), pl.program_id(1), pl.program_id(2)

    @pl.when(j == 0)
    def _():
        o_ref[...] = jnp.zeros(o_ref.shape, o_ref.dtype)
        w_ref[...] = jnp.zeros(w_ref.shape, w_ref.dtype)

    hb = h_ref[0]
    t = hb.shape[0]

    def idx(k):
        return (b * ne + egi * eg + k) * nblk + j

    def place(experts, first, rows):
        hits = [lax.broadcasted_iota(I32, (rows, t), 0) == g_ref[0, k] - (a_ref[idx(k)] + first) for k in experts]
        onehot = jnp.concatenate([jnp.where(h, 1.0, 0.0).astype(BF16) for h in hits], axis=0)
        picked = _dot(onehot, hb).astype(BF16)
        for n, (k, hit) in enumerate(zip(experts, hits)):
            weight = jnp.sum(jnp.where(hit, aff_ref[0, k], 0.0), axis=1, keepdims=True)
            sl = (0, k, pl.ds(pl.multiple_of(a_ref[idx(k)] + first, BF16_ROWS), rows), slice(None))
            o_ref[sl] = o_ref[sl] + picked[n * rows:(n + 1) * rows]
            w_ref[sl] = w_ref[sl] + weight

    place(list(range(eg)), 0, main)
    for k in range(eg if tail else 0):
        @pl.when(t_ref[idx(k)] != 0)
        def _(k=k):
            place([k], main, tail)


def _gather(h2u, afft, gpos, astart, need_tail, cap, tile_off, eg):
    b, ne, n = gpos.shape
    m, d = h2u.shape[1:]
    nblk = n // TILE
    main, tail, _ = _window(cap)
    lists = lambda w: pl.BlockSpec((1, eg, cap, w), lambda bb, e, j, a, nt: (bb, e, 0, 0),
                                   pipeline_mode=pl.Buffered(1))
    return pl.pallas_call(
        functools.partial(_gather_kernel, ne=ne, nblk=nblk, eg=eg, main=main, tail=tail),
        out_shape=[jax.ShapeDtypeStruct((b, ne, cap, d), BF16), jax.ShapeDtypeStruct((b, ne, cap, 1), F32)],
        grid_spec=pltpu.PrefetchScalarGridSpec(
            num_scalar_prefetch=2,
            grid=(b, ne // eg, nblk),
            in_specs=[pl.BlockSpec((1, eg, 1, TILE), lambda bb, e, j, a, nt: (bb, e, 0, j)),
                      pl.BlockSpec((1, eg, 1, TILE), lambda bb, e, j, a, nt: (bb, e, 0, tile_off + j)),
                      pl.BlockSpec((1, TILE, d), lambda bb, e, j, a, nt: (bb, tile_off + j, 0))],
            out_specs=[lists(d), lists(1)]),
        compiler_params=_params(3),
        name="expert_gather",
    )(astart, need_tail, gpos.reshape(b, ne, 1, n), afft.reshape(b, ne, 1, m), h2u)


def _ffn_kernel(x_ref, rw_ref, wg_ref, wu_ref, wd_ref, o_ref, wg_s, wu_s, wd_s):
    @pl.when((pl.program_id(1) == 0) & (pl.program_id(2) == 0))
    def _():
        wg_s[...] = wg_ref[0, 0].astype(BF16)
        wu_s[...] = wu_ref[0, 0].astype(BF16)
        wd_s[...] = wd_ref[0, 0].astype(BF16)

    x = x_ref[0, 0]
    gate = _dot(x, wg_s[...])
    up = _dot(x, wu_s[...])
    hid = (gate / (1.0 + jnp.exp(-gate)) * up).astype(BF16)
    o_ref[0, 0] = (_dot(hid, wd_s[...]) * rw_ref[0, 0]).astype(BF16)


def _expert_ffn(xs, ws, w_gate, w_up, w_down, layer):
    b, ne, cap, d = xs.shape
    tr = min(cap, 512)
    wspec = lambda w: pl.BlockSpec((1, 1) + w.shape[2:], lambda e, bb, r: (layer, e, 0, 0))
    rows = lambda w: pl.BlockSpec((1, 1, tr, w), lambda e, bb, r: (bb, e, r, 0))
    return pl.pallas_call(
        _ffn_kernel,
        out_shape=jax.ShapeDtypeStruct(xs.shape, BF16),
        grid=(ne, b, cap // tr),
        in_specs=[rows(d), rows(1), wspec(w_gate), wspec(w_up), wspec(w_down)],
        out_specs=rows(d),
        scratch_shapes=[pltpu.VMEM(w.shape[2:], BF16) for w in (w_gate, w_up, w_down)],
        compiler_params=_params(3),
        name="expert_swiglu",
    )(xs, ws, w_gate, w_up, w_down)


def _combine_kernel(a_ref, t_ref, x_ref, g_ref, mod_ref, fg_ref, ye_ref, o_ref,
                    win_ref, tail_ref, tacc_ref, sem_ref, tsem_ref, *, ne, nblk, main, tail, final):
    b, j = pl.program_id(0), pl.program_id(1)
    step = b * nblk + j
    slot = step % 2

    def start_at(bb, jj, e):
        return pl.multiple_of(a_ref[(bb * ne + e) * nblk + jj], BF16_ROWS)

    def start_of(e):
        return start_at(b, j, e)

    def window_copy(bb, jj, e, sl):
        return pltpu.make_async_copy(ye_ref.at[bb, e, pl.ds(start_at(bb, jj, e), main)],
                                     win_ref.at[sl, e * main:(e + 1) * main], sem_ref.at[sl, e])

    @pl.when(step == 0)
    def _():
        for e in range(ne):
            window_copy(b, j, e, slot).start()

    @pl.when(step + 1 < pl.num_programs(0) * nblk)
    def _():
        nxt = step + 1
        for e in range(ne):
            window_copy(nxt // nblk, nxt % nblk, e, 1 - slot).start()

    t = x_ref.shape[1]

    def pick(experts, first, rows, windows):
        onehot = jnp.concatenate(
            [jnp.where(lax.broadcasted_iota(I32, (rows, t), 0) == g_ref[0, e] - (start_of(e) + first), 1.0, 0.0)
             .astype(BF16) for e in experts], axis=0)
        return lax.dot_general(onehot, windows, (((0,), (0,)), ((), ())), preferred_element_type=F32)

    if tail:
        tacc_ref[...] = jnp.zeros(tacc_ref.shape, F32)
        for e in range(ne):
            @pl.when(t_ref[(b * ne + e) * nblk + j] != 0)
            def _(e=e):
                cp = pltpu.make_async_copy(ye_ref.at[b, e, pl.ds(start_of(e) + main, tail)], tail_ref, tsem_ref.at[0])
                cp.start()
                cp.wait()
                tacc_ref[...] += pick([e], main, tail, tail_ref[...])

    for e in range(ne):
        window_copy(b, j, e, slot).wait()
    acc = pick(list(range(ne)), 0, main, win_ref[slot])
    if tail:
        acc = acc + tacc_ref[...]
    x = x_ref[0] + mod_ref[0, 0][5:6] * acc
    if final:
        x = x * lax.rsqrt(jnp.mean(x * x, axis=-1, keepdims=True) + EPS) * fg_ref[...]
    o_ref[0] = x


def _combine(xu, gpos, astart, need_tail, ye, modtab, final_g, tile_off, kind, final):
    b, m, d = xu.shape
    _, ne, n = gpos.shape
    cap = ye.shape[2]
    nblk = n // TILE
    main, tail, _ = _window(cap)
    out_spec = pl.BlockSpec((1, TILE, d), (lambda bb, j, a, nt: (bb, j, 0)) if final
                            else (lambda bb, j, a, nt: (bb, tile_off + j, 0)))
    return pl.pallas_call(
        functools.partial(_combine_kernel, ne=ne, nblk=nblk, main=main, tail=tail, final=final),
        out_shape=jax.ShapeDtypeStruct((b, n, d) if final else xu.shape, F32),
        grid_spec=pltpu.PrefetchScalarGridSpec(
            num_scalar_prefetch=2,
            grid=(b, nblk),
            in_specs=[pl.BlockSpec((1, TILE, d), lambda bb, j, a, nt: (bb, tile_off + j, 0)),
                      pl.BlockSpec((1, ne, 1, TILE), lambda bb, j, a, nt: (bb, 0, 0, j)),
                      pl.BlockSpec((1, 1, N_MOD, d), lambda bb, j, a, nt: (bb, kind, 0, 0)),
                      pl.BlockSpec((1, d), lambda bb, j, a, nt: (0, 0)),
                      pl.BlockSpec(memory_space=pl.ANY)],
            out_specs=out_spec,
            scratch_shapes=[pltpu.VMEM((2, ne * main, d), BF16), pltpu.VMEM((max(tail, BF16_ROWS), d), BF16),
                            pltpu.VMEM((TILE, d), F32), pltpu.SemaphoreType.DMA((2, ne)),
                            pltpu.SemaphoreType.DMA((1,))]),
        input_output_aliases={} if final else {2: 0},
        compiler_params=_params(2),
        name="expert_combine",
    )(astart, need_tail, xu, gpos.reshape(b, ne, 1, n), modtab, final_g, ye)


def _rope_tables(n):
    t = np.arange(n)
    row = (t // GRID_W).astype(np.float32)[:, None]
    col = (t % GRID_W).astype(np.float32)[:, None]
    inv = (ROPE_BASE ** (-np.arange(ROPE_PAIRS, dtype=np.float32) / ROPE_PAIRS)).astype(np.float32)
    ang = np.concatenate([row * inv, row * inv, col * inv, col * inv], axis=-1)
    cos, sin = np.cos(ang), np.sin(ang)
    half = (np.arange(HEAD_DIM) % (2 * ROPE_PAIRS)) < ROPE_PAIRS
    sin = np.where(half[None], -sin, sin)
    cos = np.concatenate([np.ones((CTX, HEAD_DIM)), cos], axis=0)
    sin = np.concatenate([np.zeros((CTX, HEAD_DIM)), sin], axis=0)
    tile2 = lambda a: jnp.asarray(np.concatenate([a, a], axis=1), F32)
    return tile2(cos), tile2(sin)


def _moe(xu, h2u, afft, modtab, weights, layer, final_g, final):
    b, m, d = xu.shape
    n = m - CTX
    w_gate, w_up, w_down = weights
    cap_lat = max(1, EC_CAPACITY * n // N_EXPERTS)
    cap_ctx = max(1, EC_CAPACITY * CTX // N_EXPERTS)
    sets = [afft[:, :, CTX:]]
    if not final:
        sets.append(jnp.concatenate([afft[:, :, :CTX], jnp.full((b, N_EXPERTS, n - CTX), -1.0, F32)], axis=2))
    gpos, gfull = _route(jnp.concatenate(sets, axis=0), b, cap_lat, cap_ctx)

    def run(idx, n_set, cap, tile_off, kind, x_in, fin):
        gp = gpos[idx * b:(idx + 1) * b, :, :n_set]
        gf = gfull[idx * b:(idx + 1) * b, :, :n_set]
        astart, need_tail = _windows(gf, cap)
        xs, ws = _gather(h2u, afft, gp, astart, need_tail, cap, tile_off, eg=8)
        ye = _expert_ffn(xs, ws, w_gate, w_up, w_down, layer)
        return _combine(x_in, gp, astart, need_tail, ye, modtab, final_g, tile_off, kind, fin)

    if final:
        return run(0, n, cap_lat, 1, 1, xu, True)
    xu = run(0, n, cap_lat, 1, 1, xu, False)
    return run(1, CTX, cap_ctx, 0, 0, xu, False)


def kernel(x, c, ctx, c_ctx, w_mod, b_mod, norm1_g, w_in, na_rpb, diff_lambda, diff_subln_g, swa_sink,
           w_branch_na, w_branch_diff, w_branch_swa, w_out, norm2_g, w_router, w_expert_gate, w_expert_up,
           w_expert_down, final_g):
    b, n, d = x.shape
    depth = w_mod.shape[0]
    assert ctx.shape[1] == CTX and d == D_MODEL and n % (2 * TILE) == 0 and n // TILE >= 3

    cin = jnp.concatenate([c, c_ctx[None], jnp.zeros((8 - b - 1, d), F32)], axis=0)
    mod_all = _modulation(cin, w_mod, b_mod)
    cos_u, sin_u = _rope_tables(n)
    xu = jnp.concatenate([ctx, x], axis=1)
    final_g2 = final_g.reshape(1, d)

    out = None
    for l in range(depth):
        final = l == depth - 1
        lam_init = 0.8 - 0.6 * math.exp(-0.3 * l)
        mod_l = mod_all[l].reshape(8, N_MOD, d)
        modtab = jnp.stack([jnp.broadcast_to(mod_l[b], (b, N_MOD, d)), mod_l[:b]], axis=1)
        lam_p = diff_lambda[l].astype(F32)
        lam = (jnp.exp(jnp.sum(lam_p[0] * lam_p[1])) - jnp.exp(jnp.sum(lam_p[2] * lam_p[3])) + lam_init).reshape(1)

        (q_na, k_na, v_na, q_d, k_d, v_d, q_s, k_s, v_s, gs) = _project(
            xu, modtab, norm1_g[l].reshape(1, d), w_in[l].astype(BF16), cos_u, sin_u)
        o_na = _na_attention(q_na, k_na, v_na, _na_bias_table(na_rpb[l]))
        o_d = _diff_attention(q_d, k_d, v_d, lam, diff_subln_g[l].reshape(1, DIFF_V), lam_init)
        o_s = _swa_attention(q_s, k_s, v_s, swa_sink[l].astype(F32))
        xu, h2u, afft = _merge(
            o_na, o_d, o_s, gs, xu, modtab, w_branch_na[l].astype(BF16), w_branch_diff[l].astype(BF16),
            w_branch_swa[l].astype(BF16), w_out[l].astype(BF16), norm2_g[l].reshape(1, d), w_router[l])
        weights = (w_expert_gate, w_expert_up, w_expert_down)
        res = _moe(xu, h2u, afft, modtab, weights, l, final_g2, final)
        if final:
            out = res
        else:
            xu = res
    return out
```

```python
import functools
import math

import numpy as np
import jax
import jax.numpy as jnp
from jax import lax
from jax.experimental import pallas as pl
from jax.experimental.pallas import tpu as pltpu

F32 = jnp.float32
BF16 = jnp.bfloat16
I32 = jnp.int32
HIGHEST = lax.Precision.HIGHEST

D_MODEL = 1024
CTX = 256
TILE = 256
GRID_W = 64
HEAD_DIM = 64
ROPE_PAIRS = HEAD_DIM // 4
ROPE_BASE = 10000.0
EPS = 1e-6
N_MOD = 6
NA_HEADS = 4
NA_WIN_ROWS = 8
NA_WIN_COLS = 16
NA_TILE_ROWS = TILE // GRID_W
DIFF_HEADS = 4
DIFF_V = 2 * HEAD_DIM
SWA_HEADS = 4
SWA_KV_HEADS = 2
SWA_WINDOW = 128
N_EXPERTS = 16
EC_CAPACITY = 2
NEG = -1e30
LANES = 128
MXU_TILE = 256
LOG2E = math.log2(math.e)
DIFF_TILES_PER_STEP = 2
DIFF_MAX_DENOMINATOR = 2.0 ** 64
BF16_ROWS = 16
WINDOW_MAIN_ROWS = 64

C_QNA, C_QD, C_QS = 0, 256, 768
C_KNA, C_KD, C_KS = 1024, 1280, 1792
C_VNA, C_VD, C_VS = 1920, 2176, 2688
C_GATE, C_END = 2816, 5888

VMEM_LIMIT = 56 * 1024 * 1024


def _params(n_axes, vmem=VMEM_LIMIT):
    return pltpu.CompilerParams(dimension_semantics=("arbitrary",) * n_axes, vmem_limit_bytes=vmem)


def _nt_dot(a, b):
    return lax.dot_general(a, b, (((1,), (1,)), ((), ())), preferred_element_type=F32)


def _dot(a, b):
    return jnp.dot(a, b, preferred_element_type=F32)


def _mod_kernel(c_ref, w_ref, b_ref, o_ref):
    c = c_ref[...]
    s = c / (1.0 + jnp.exp(-c))
    o_ref[0] = jnp.dot(s, w_ref[0], preferred_element_type=F32, precision=HIGHEST) + b_ref[0]


def _modulation(cin, w_mod, b_mod):
    depth, d, w = w_mod.shape
    tn = 1024
    return pl.pallas_call(
        _mod_kernel,
        out_shape=jax.ShapeDtypeStruct((depth, 8, w), F32),
        grid=(depth, w // tn),
        in_specs=[pl.BlockSpec((8, d), lambda l, j: (0, 0)),
                  pl.BlockSpec((1, d, tn), lambda l, j: (l, 0, j)),
                  pl.BlockSpec((1, 1, tn), lambda l, j: (l, 0, j))],
        out_specs=pl.BlockSpec((1, 8, tn), lambda l, j: (l, 0, j)),
        compiler_params=_params(2),
        name="modulation",
    )(cin, w_mod, b_mod.reshape(depth, 1, w))


def _row_modulation(mod_ref, first, rows):
    is_ctx = first + lax.broadcasted_iota(I32, (rows, 1), 0) < CTX
    return lambda k: jnp.where(is_ctx, mod_ref[0, 0, k:k + 1], mod_ref[0, 1, k:k + 1])


def _independent_parts(rows, parts=2):
    parts = parts if rows % (parts * BF16_ROWS) == 0 else 1
    return [slice(p * (rows // parts), (p + 1) * (rows // parts)) for p in range(parts)]


def _proj_kernel(x_ref, mod_ref, g_ref, w_ref, cos_ref, sin_ref,
                 qna_ref, kna_ref, vna_ref, qd_ref, kd_ref, vd_ref, qs_ref, ks_ref, vs_ref, gs_ref):
    x = x_ref[0]
    mod = mod_ref[0, 0]
    y = x * lax.rsqrt(jnp.mean(x * x, axis=-1, keepdims=True) + EPS) * g_ref[...]
    h = (y * (1.0 + mod[1:2]) + mod[0:1]).astype(BF16)

    def proj(a, b):
        return _dot(h, w_ref[:, a:b])

    cos = cos_ref[...]
    sin = sin_ref[...]
    lane = lax.broadcasted_iota(I32, cos.shape, 1)
    first_half = (lane % (2 * ROPE_PAIRS)) < ROPE_PAIRS

    def rope(t):
        outs = []
        for j in range(t.shape[1] // LANES):
            c = t[:, j * LANES:(j + 1) * LANES]
            r = jnp.where(first_half, pltpu.roll(c, LANES - ROPE_PAIRS, 1), pltpu.roll(c, ROPE_PAIRS, 1))
            outs.append(c * cos + r * sin)
        return outs[0] if len(outs) == 1 else jnp.concatenate(outs, axis=1)

    scale = HEAD_DIM ** -0.5
    qna_ref[0] = (proj(C_QNA, C_QD) * scale).astype(BF16)
    qd_ref[0] = (rope(proj(C_QD, C_QS)) * (scale * LOG2E)).astype(BF16)
    qs_ref[0] = (rope(proj(C_QS, C_KNA)) * scale).astype(BF16)
    kna_ref[0] = proj(C_KNA, C_KD).astype(BF16)
    kd_ref[0] = rope(proj(C_KD, C_KS)).astype(BF16)
    ks_ref[0] = rope(proj(C_KS, C_VNA)).astype(BF16)
    vna_ref[0] = proj(C_VNA, C_VD).astype(BF16)
    vd = proj(C_VD, C_VS).astype(BF16)
    ones = jnp.ones((vd.shape[0], DIFF_V), BF16)
    vd_ref[0] = jnp.concatenate(
        [blk for hd in range(DIFF_HEADS) for blk in (vd[:, hd * DIFF_V:(hd + 1) * DIFF_V], ones)], axis=1)
    vs_ref[0] = proj(C_VS, C_GATE).astype(BF16)
    gates = proj(C_GATE, C_END)
    gs_ref[0] = (1.0 / (1.0 + jnp.exp(-gates))).astype(BF16)


def _matmul_rows(m, limit):
    return max(t for t in range(LANES, limit + 1, LANES) if m % t == 0)


def _project(xu, modtab, g, w_in, cos_u, sin_u):
    b, m, d = xu.shape
    tm = TILE
    widths = (256, 256, 256, 512, 512, 2 * DIFF_HEADS * DIFF_V, 256, 128, 128, C_END - C_GATE)
    row = lambda w: pl.BlockSpec((1, tm, w), lambda bb, i: (bb, i, 0))
    return pl.pallas_call(
        _proj_kernel,
        out_shape=[jax.ShapeDtypeStruct((b, m, w), BF16) for w in widths],
        grid=(b, m // tm),
        in_specs=[row(d),
                  pl.BlockSpec((1, 1, N_MOD, d), lambda bb, i: (bb, jnp.minimum(i, 1), 0, 0)),
                  pl.BlockSpec((1, d), lambda bb, i: (0, 0)),
                  pl.BlockSpec((d, C_END), lambda bb, i: (0, 0)),
                  pl.BlockSpec((tm, LANES), lambda bb, i: (i, 0)),
                  pl.BlockSpec((tm, LANES), lambda bb, i: (i, 0))],
        out_specs=[row(w) for w in widths],
        compiler_params=_params(2),
        name="norm_project_rope",
    )(xu, modtab, g, w_in, cos_u, sin_u)


def _tile_kind(t, nb):
    return jnp.where(t == 0, 0, jnp.where(t == 1, 1, jnp.where(t == nb, 3, 2)))


def _local_tiles(prev_ref, cur_ref, next_ref, u):
    tiles = cur_ref.shape[1] // TILE

    def tile(s):
        if s < 0:
            return prev_ref[0]
        if s >= tiles:
            return next_ref[0]
        return cur_ref[0, s * TILE:(s + 1) * TILE]

    return jnp.concatenate([tile(u - 1), tile(u), tile(u + 1)], axis=0)


def _na_kernel(q_ref, kc_ref, kp_ref, kcur_ref, kn_ref, vc_ref, vp_ref, vcur_ref, vn_ref, bias_ref, o_ref, *, nb):
    tiles = q_ref.shape[1] // TILE
    kc, vc = kc_ref[0], vc_ref[0]
    for u in range(tiles):
        rows = slice(u * TILE, (u + 1) * TILE)
        kind = _tile_kind(pl.program_id(1) * tiles + u, nb)
        q = q_ref[0, rows]
        kl = _local_tiles(kp_ref, kcur_ref, kn_ref, u)
        vl = _local_tiles(vp_ref, vcur_ref, vn_ref, u)
        outs = []
        for hd in range(NA_HEADS):
            sl = slice(hd * HEAD_DIM, (hd + 1) * HEAD_DIM)
            qh = q[:, sl]
            s_c = _nt_dot(qh, kc[:, sl])
            s_l = _nt_dot(qh, kl[:, sl]) + bias_ref[kind, hd]
            mx = jnp.maximum(jnp.max(s_c, axis=1, keepdims=True), jnp.max(s_l, axis=1, keepdims=True))
            p_c = jnp.exp(s_c - mx)
            p_l = jnp.exp(s_l - mx)
            den = jnp.sum(p_c, axis=1, keepdims=True) + jnp.sum(p_l, axis=1, keepdims=True)
            o = _dot(p_c.astype(BF16), vc[:, sl]) + _dot(p_l.astype(BF16), vl[:, sl])
            outs.append(o / den)
        o_ref[0, rows] = jnp.concatenate(outs, axis=1).astype(BF16)


def _na_bias_table(rpb):
    tr, nk = NA_TILE_ROWS, 3 * NA_TILE_ROWS
    qc = np.arange(GRID_W)[:, None]
    kc = np.arange(GRID_W)[None, :]
    cstart = np.clip(qc - NA_WIN_COLS // 2, 0, GRID_W - NA_WIN_COLS)
    col_ok = (kc >= cstart) & (kc < cstart + NA_WIN_COLS)
    dc = np.clip(kc - qc, -(NA_WIN_COLS - 1), NA_WIN_COLS - 1) + NA_WIN_COLS - 1
    onehot = jnp.asarray(np.arange(2 * NA_WIN_COLS - 1)[:, None, None] == dc[None], F32)
    cols = jnp.einsum("hrc,cqk->hrqk", rpb.astype(F32), onehot, precision=HIGHEST)
    qr = np.arange(tr)[:, None]
    krow = np.arange(nk)[None, :] - tr
    dr = np.clip(krow - qr, -(NA_WIN_ROWS - 1), NA_WIN_ROWS - 1) + NA_WIN_ROWS - 1
    starts = (0 * qr, qr - NA_WIN_ROWS // 2, 0 * qr + tr - NA_WIN_ROWS)
    row_ok = np.stack([(krow >= st) & (krow < st + NA_WIN_ROWS) for st in starts])
    t = jnp.take(cols, jnp.asarray(dr.reshape(-1)), axis=1)
    t = t.reshape(NA_HEADS, tr, nk, GRID_W, GRID_W).transpose(0, 1, 3, 2, 4)
    ok = row_ok[:, None, :, None, :, None] & col_ok[None, None, None, :, None, :]
    t = jnp.where(ok, t[None], NEG).reshape(3, NA_HEADS, TILE, 3 * TILE)
    return jnp.concatenate([jnp.full_like(t[:1], NEG), t], axis=0)


def _local_attention_specs(nt, wq, wk):
    nb = nt - 1
    tiles = max(t for t in range(1, 5 + 1) if nt % t == 0)
    rows = pl.BlockSpec((1, tiles * TILE, wq), lambda bb, i: (bb, i, 0))
    one = lambda f: pl.BlockSpec((1, TILE, wk), f)
    kv = [one(lambda bb, i: (bb, 0, 0)),
          one(lambda bb, i: (bb, jnp.clip(i * tiles - 1, 1, nb), 0)),
          pl.BlockSpec((1, tiles * TILE, wk), lambda bb, i: (bb, i, 0)),
          one(lambda bb, i: (bb, jnp.clip((i + 1) * tiles, 1, nb), 0))]
    return tiles, rows, kv


def _resident(a):
    return pl.BlockSpec(a.shape, lambda bb, i: (0,) * a.ndim, pipeline_mode=pl.Buffered(1))


def _na_attention(q, k, v, bias):
    b, m, w = q.shape
    nt = m // TILE
    tiles, rows, kv_specs = _local_attention_specs(nt, w, w)
    return pl.pallas_call(
        functools.partial(_na_kernel, nb=nt - 1),
        out_shape=jax.ShapeDtypeStruct((b, m, w), BF16),
        grid=(b, nt // tiles),
        in_specs=[rows] + kv_specs + kv_specs + [_resident(bias)],
        out_specs=rows,
        compiler_params=_params(2),
        name="neighbourhood_attention",
    )(q, k, k, k, k, v, v, v, v, bias)


def _swa_kernel(sink_ref, q_ref, kc_ref, kp_ref, kcur_ref, kn_ref, vc_ref, vp_ref, vcur_ref, vn_ref, mask_ref,
                o_ref, *, nb):
    tiles = q_ref.shape[1] // TILE
    kc, vc = kc_ref[0], vc_ref[0]
    group = SWA_HEADS // SWA_KV_HEADS
    rows = group * TILE
    rid = lax.broadcasted_iota(I32, (rows, 1), 0)
    for u in range(tiles):
        kind = _tile_kind(pl.program_id(1) * tiles + u, nb)
        q = q_ref[0, u * TILE:(u + 1) * TILE]
        kl = _local_tiles(kp_ref, kcur_ref, kn_ref, u)
        vl = _local_tiles(vp_ref, vcur_ref, vn_ref, u)
        mask = jnp.concatenate([mask_ref[kind]] * group, axis=0)
        outs = []
        for g in range(SWA_KV_HEADS):
            ksl = slice(g * HEAD_DIM, (g + 1) * HEAD_DIM)
            qg = jnp.concatenate([q[:, (g * group + j) * HEAD_DIM:(g * group + j + 1) * HEAD_DIM]
                                  for j in range(group)], axis=0)
            sink = jnp.zeros((rows, 1), F32)
            for j in range(group):
                sink = jnp.where(rid // TILE == j, sink_ref[g * group + j], sink)
            s_c = _nt_dot(qg, kc[:, ksl])
            s_l = _nt_dot(qg, kl[:, ksl]) + mask
            mx = jnp.maximum(jnp.maximum(jnp.max(s_c, axis=1, keepdims=True), jnp.max(s_l, axis=1, keepdims=True)),
                             sink)
            p_c = jnp.exp(s_c - mx)
            p_l = jnp.exp(s_l - mx)
            den = jnp.sum(p_c, axis=1, keepdims=True) + jnp.sum(p_l, axis=1, keepdims=True) + jnp.exp(sink - mx)
            o = (_dot(p_c.astype(BF16), vc[:, ksl]) + _dot(p_l.astype(BF16), vl[:, ksl])) / den
            outs.extend(o[j * TILE:(j + 1) * TILE] for j in range(group))
        o_ref[0, u * TILE:(u + 1) * TILE] = jnp.concatenate(outs, axis=1).astype(BF16)


def _swa_mask_table():
    qpos = np.arange(TILE)[:, None]
    kpos = np.arange(3 * TILE)[None, :] - TILE
    near = np.abs(qpos - kpos) <= SWA_WINDOW
    kinds = [np.zeros_like(near), near & (kpos >= 0), near, near & (kpos < TILE)]
    return jnp.asarray(np.where(np.stack(kinds), 0.0, NEG), F32)


def _swa_attention(q, k, v, sink):
    b, m, wq = q.shape
    nt = m // TILE
    tiles, rows, kv_specs = _local_attention_specs(nt, wq, k.shape[2])
    mask = _swa_mask_table()
    return pl.pallas_call(
        functools.partial(_swa_kernel, nb=nt - 1),
        out_shape=jax.ShapeDtypeStruct((b, m, wq), BF16),
        grid=(b, nt // tiles),
        in_specs=[pl.BlockSpec(memory_space=pltpu.SMEM), rows] + kv_specs + kv_specs + [_resident(mask)],
        out_specs=rows,
        compiler_params=_params(2),
        name="windowed_attention",
    )(sink, q, k, k, k, k, v, v, v, v, mask)


def _diff_kernel(lam_ref, q_ref, k_ref, v_ref, g_ref, o_ref, acc_ref, m_ref, s_ref, cmax_ref,
                 *, n_chunks, kblk, fast_kblk, lam_init):
    i = pl.program_id(2)
    parts = q_ref.shape[1] // TILE

    def stacked(part):
        q = q_ref[0, part * TILE:(part + 1) * TILE]
        lane = lax.broadcasted_iota(I32, q.shape, 1)
        zero = jnp.zeros_like(q)
        return jnp.concatenate([jnp.where(lane < HEAD_DIM, q, zero), jnp.where(lane >= HEAD_DIM, q, zero)], axis=0)

    def chunk(c):
        return pl.ds(pl.multiple_of(c * kblk, LANES), kblk)

    def accumulate(s, cmax, v):
        m_prev = m_ref[...]
        m_new = jnp.maximum(m_prev, cmax)
        p = jnp.exp2(s - m_new).astype(BF16)
        acc_ref[...] = jnp.exp2(m_prev - m_new) * acc_ref[...] + _dot(p, v)
        m_ref[...] = m_new

    def reset():
        m_ref[...] = jnp.full(m_ref.shape, NEG, F32)
        acc_ref[...] = jnp.zeros(acc_ref.shape, F32)

    def safe_sweep(qq):
        def scores(c, slot):
            s = _nt_dot(qq, k_ref[0, chunk(c), :])
            s_ref[slot] = s
            cmax_ref[slot] = jnp.max(s, axis=1, keepdims=True)

        reset()
        scores(0, 0)

        def pair(c, last):
            for cur in (0, 1):
                if not (last and cur == 1):
                    scores(c + cur + 1, 1 - cur)
                accumulate(s_ref[cur], cmax_ref[cur], v_ref[0, chunk(c + cur), :])

        def body(c2, carry):
            pair(2 * c2, False)
            return carry

        lax.fori_loop(0, n_chunks // 2 - 1, body, 0)
        pair(n_chunks - 2, True)
        return acc_ref[...]

    def fast_sweep(qq):
        keys = lambda c: slice(c * fast_kblk, (c + 1) * fast_kblk)
        s = _nt_dot(qq, k_ref[0, keys(0), :])
        m0 = jnp.max(s, axis=1, keepdims=True)
        acc = _dot(jnp.exp2(s - m0).astype(BF16), v_ref[0, keys(0), :])
        for c in range(1, n_chunks * kblk // fast_kblk):
            s = _nt_dot(qq, k_ref[0, keys(c), :])
            acc = acc + _dot(jnp.exp2(s - m0).astype(BF16), v_ref[0, keys(c), :])
        return acc

    def finish(acc):
        o = acc[:, :DIFF_V] / acc[:, DIFF_V:]
        d = o[:TILE] - lam_ref[0] * o[TILE:]
        y = d * lax.rsqrt(jnp.mean(d * d, axis=-1, keepdims=True) + EPS) * g_ref[...]
        return (y * (1.0 - lam_init)).astype(BF16)

    def rows(part):
        return pl.ds(part * TILE, TILE)

    @pl.when(i == 0)
    def _():
        o_ref[0, 0:(parts - 1) * TILE] = jnp.zeros(((parts - 1) * TILE, DIFF_V), BF16)
        reset()
        s = _nt_dot(stacked(parts - 1), k_ref[0, 0:CTX, :])
        accumulate(s, jnp.max(s, axis=1, keepdims=True), v_ref[0, 0:CTX, :])
        o_ref[0, rows(parts - 1)] = finish(acc_ref[...])

    @pl.when(i > 0)
    def _():
        accs = [fast_sweep(stacked(part)) for part in range(parts)]
        for part in range(parts):
            o_ref[0, rows(part)] = finish(accs[part])
        bad = [jnp.logical_not(jnp.max(acc[:, DIFF_V:]) < DIFF_MAX_DENOMINATOR) for acc in accs]
        for part in range(parts):
            @pl.when(bad[part])
            def _(part=part):
                o_ref[0, rows(part)] = finish(safe_sweep(stacked(part)))


def _diff_chunking(m, max_chunk=2048):
    for n_chunks in range(2, m // LANES + 1, 2):
        if m % (n_chunks * LANES) == 0 and m // n_chunks <= max_chunk:
            return n_chunks, m // n_chunks
    raise ValueError(f"no chunking for {m} keys")


def _diff_attention(q, k, v, lam, g, lam_init):
    b, m, _ = q.shape
    n_chunks, kblk = _diff_chunking(m)
    fast_kblk = max(k for k in range(MXU_TILE, 2048 + 1, MXU_TILE) if m % k == 0)
    parts = DIFF_TILES_PER_STEP
    pad = (parts - 1) * TILE
    assert (m + pad) % (parts * TILE) == 0
    q_pad = jnp.pad(q, ((0, 0), (pad, 0), (0, 0)))
    out = pl.pallas_call(
        functools.partial(_diff_kernel, n_chunks=n_chunks, kblk=kblk, fast_kblk=fast_kblk, lam_init=lam_init),
        out_shape=jax.ShapeDtypeStruct((b, m + pad, DIFF_HEADS * DIFF_V), BF16),
        grid=(b, DIFF_HEADS, (m + pad) // (parts * TILE)),
        in_specs=[pl.BlockSpec(memory_space=pltpu.SMEM),
                  pl.BlockSpec((1, parts * TILE, 2 * HEAD_DIM), lambda bb, hh, i: (bb, i, hh)),
                  pl.BlockSpec((1, m, 2 * HEAD_DIM), lambda bb, hh, i: (bb, 0, hh)),
                  pl.BlockSpec((1, m, 2 * DIFF_V), lambda bb, hh, i: (bb, 0, hh)),
                  pl.BlockSpec((1, DIFF_V), lambda bb, hh, i: (0, 0))],
        out_specs=pl.BlockSpec((1, parts * TILE, DIFF_V), lambda bb, hh, i: (bb, i, hh)),
        scratch_shapes=[pltpu.VMEM((2 * TILE, 2 * DIFF_V), F32), pltpu.VMEM((2 * TILE, 1), F32),
                        pltpu.VMEM((2, 2 * TILE, kblk), F32), pltpu.VMEM((2, 2 * TILE, 1), F32)],
        compiler_params=_params(3),
        name="differential_attention",
    )(lam, q_pad, k, v, g)
    return out[:, pad:]


def _merge_kernel(ona_ref, od_ref, osw_ref, gs_ref, x_ref, mod_ref, wna_ref, wd_ref, wsw_ref, wo_ref,
                  g2_ref, wrt_ref, xo_ref, h2_ref, afft_ref):
    tm, d = x_ref.shape[1:]
    h2_parts = []
    for rows in _independent_parts(tm):
        gs = gs_ref[0, rows]
        y = (gs[:, :d].astype(F32) * _dot(ona_ref[0, rows], wna_ref[...])
             + gs[:, d:2 * d].astype(F32) * _dot(od_ref[0, rows], wd_ref[...])
             + gs[:, 2 * d:].astype(F32) * _dot(osw_ref[0, rows], wsw_ref[...]))
        mod = _row_modulation(mod_ref, pl.program_id(1) * tm + rows.start, rows.stop - rows.start)
        xn = x_ref[0, rows] + mod(2) * _dot(y.astype(BF16), wo_ref[...])
        xo_ref[0, rows] = xn
        r = xn * lax.rsqrt(jnp.mean(xn * xn, axis=-1, keepdims=True) + EPS) * g2_ref[...]
        h2 = r * (1.0 + mod(4)) + mod(3)
        h2_ref[0, rows] = h2.astype(BF16)
        h2_parts.append(h2)
    h2 = jnp.concatenate(h2_parts, axis=0)
    lt = lax.dot_general(wrt_ref[...], h2, (((1,), (1,)), ((), ())),
                         preferred_element_type=F32, precision=HIGHEST)
    et = jnp.exp(lt - jnp.max(lt, axis=0, keepdims=True))
    afft_ref[0] = et / jnp.sum(et, axis=0, keepdims=True)


def _merge(o_na, o_d, o_sw, gs, xu, modtab, w_na, w_d, w_sw, w_o, g2, w_r):
    b, m, d = xu.shape
    tm = _matmul_rows(m, 640)
    ne = w_r.shape[1]
    row = lambda w: pl.BlockSpec((1, tm, w), lambda bb, i: (bb, i, 0))
    full = lambda a: pl.BlockSpec(a.shape, lambda bb, i: (0,) * a.ndim)
    w_rt = w_r.T
    return pl.pallas_call(
        _merge_kernel,
        out_shape=[jax.ShapeDtypeStruct((b, m, d), F32), jax.ShapeDtypeStruct((b, m, d), BF16),
                   jax.ShapeDtypeStruct((b, ne, m), F32)],
        grid=(b, m // tm),
        in_specs=[row(o_na.shape[2]), row(o_d.shape[2]), row(o_sw.shape[2]), row(gs.shape[2]), row(d),
                  pl.BlockSpec((1, 2, N_MOD, d), lambda bb, i: (bb, 0, 0, 0)),
                  full(w_na), full(w_d), full(w_sw), full(w_o), full(g2), full(w_rt)],
        out_specs=[row(d), row(d), pl.BlockSpec((1, ne, tm), lambda bb, i: (bb, 0, i))],
        compiler_params=_params(2),
        name="merge_residual_router",
    )(o_na, o_d, o_sw, gs, xu, modtab, w_na, w_d, w_sw, w_o, g2, w_rt)


def _route_kernel(a_ref, gpos_ref, gfull_ref, *, n_latent_sets, cap_lat, cap_ctx):
    ne, c, w = a_ref.shape[1:]
    a = a_ref[0]
    bits = lax.bitcast_convert_type(a, I32)
    cap = jnp.where(pl.program_id(0) < n_latent_sets, cap_lat, cap_ctx).astype(F32)

    def count(mask):
        return jnp.sum(jnp.sum(mask.astype(F32), axis=2, keepdims=True), axis=1, keepdims=True)

    def search(it, thr):
        cand = thr | jnp.left_shift(jnp.int32(1), 30 - it)
        return jnp.where(count(bits >= cand) >= cap, cand, thr)

    thr = lax.fori_loop(0, 31, search, jnp.zeros((ne, 1, 1), I32))
    gt = bits > thr
    eq = bits == thr
    need = cap - count(gt)

    upper = (lax.broadcasted_iota(I32, (w, w), 0) <= lax.broadcasted_iota(I32, (w, w), 1)).astype(BF16)
    lower = (lax.broadcasted_iota(I32, (c, c), 1) < lax.broadcasted_iota(I32, (c, c), 0)).astype(BF16)

    def exclusive_cumsum(mask):
        x = mask.astype(F32).reshape(ne * c, w)
        within = _dot(x.astype(BF16), upper)
        tot = jnp.broadcast_to(within[:, w - 1:w], (ne * c, w)).astype(BF16)
        before = jnp.concatenate([_dot(lower, tot[e * c:(e + 1) * c]) for e in range(ne)], axis=0)
        return (before + within - x).reshape(ne, c, w)

    sel = gt | (eq & (exclusive_cumsum(eq) < need))
    g = exclusive_cumsum(sel).astype(I32)
    gfull_ref[0] = g
    gpos_ref[0] = jnp.where(sel, g, -1)


def _route(aff_sets, n_latent_sets, cap_lat, cap_ctx):
    s, ne, n = aff_sets.shape
    c = n // LANES
    a4 = aff_sets.reshape(s, ne, c, LANES)
    blk = pl.BlockSpec((1, ne, c, LANES), lambda i: (i, 0, 0, 0))
    gpos, gfull = pl.pallas_call(
        functools.partial(_route_kernel, n_latent_sets=n_latent_sets, cap_lat=cap_lat, cap_ctx=cap_ctx),
        out_shape=[jax.ShapeDtypeStruct(a4.shape, I32)] * 2,
        grid=(s,),
        in_specs=[blk],
        out_specs=[blk, blk],
        compiler_params=_params(1),
        name="expert_choice_select",
    )(a4)
    return gpos.reshape(s, ne, n), gfull.reshape(s, ne, n)


def _window(cap):
    w = min(TILE + BF16_ROWS, cap)
    main = min(WINDOW_MAIN_ROWS, w)
    return main, w - main, cap - w


def _windows(gfull, cap):
    b, ne, _ = gfull.shape
    main, _, max_start = _window(cap)
    seg_start = gfull[:, :, ::TILE]
    seg_end = jnp.concatenate([gfull[:, :, TILE::TILE], jnp.full((b, ne, 1), cap, I32)], axis=2)
    astart = jnp.minimum(seg_start // BF16_ROWS * BF16_ROWS, max_start)
    return astart.reshape(-1), (seg_end > astart + main).astype(I32).reshape(-1)


def _gather_kernel(a_ref, t_ref, g_ref, aff_ref, h_ref, o_ref, w_ref, *, ne, nblk, eg, main, tail):
    b, egi, j = pl.program_id(0), pl.program_id(1), pl.program_id(2)

    @pl.when(j == 0)
    def _():
        o_ref[...] = jnp.zeros(o_ref.shape, o_ref.dtype)
        w_ref[...] = jnp.zeros(w_ref.shape, w_ref.dtype)

    hb = h_ref[0]
    t = hb.shape[0]

    def idx(k):
        return (b * ne + egi * eg + k) * nblk + j

    def place(experts, first, rows):
        hits = [lax.broadcasted_iota(I32, (rows, t), 0) == g_ref[0, k] - (a_ref[idx(k)] + first) for k in experts]
        onehot = jnp.concatenate([jnp.where(h, 1.0, 0.0).astype(BF16) for h in hits], axis=0)
        picked = _dot(onehot, hb).astype(BF16)
        for n, (k, hit) in enumerate(zip(experts, hits)):
            weight = jnp.sum(jnp.where(hit, aff_ref[0, k], 0.0), axis=1, keepdims=True)
            sl = (0, k, pl.ds(pl.multiple_of(a_ref[idx(k)] + first, BF16_ROWS), rows), slice(None))
            o_ref[sl] = o_ref[sl] + picked[n * rows:(n + 1) * rows]
            w_ref[sl] = w_ref[sl] + weight

    place(list(range(eg)), 0, main)
    for k in range(eg if tail else 0):
        @pl.when(t_ref[idx(k)] != 0)
        def _(k=k):
            place([k], main, tail)


def _gather(h2u, afft, gpos, astart, need_tail, cap, tile_off, eg):
    b, ne, n = gpos.shape
    m, d = h2u.shape[1:]
    nblk = n // TILE
    main, tail, _ = _window(cap)
    lists = lambda w: pl.BlockSpec((1, eg, cap, w), lambda bb, e, j, a, nt: (bb, e, 0, 0),
                                   pipeline_mode=pl.Buffered(1))
    return pl.pallas_call(
        functools.partial(_gather_kernel, ne=ne, nblk=nblk, eg=eg, main=main, tail=tail),
        out_shape=[jax.ShapeDtypeStruct((b, ne, cap, d), BF16), jax.ShapeDtypeStruct((b, ne, cap, 1), F32)],
        grid_spec=pltpu.PrefetchScalarGridSpec(
            num_scalar_prefetch=2,
            grid=(b, ne // eg, nblk),
            in_specs=[pl.BlockSpec((1, eg, 1, TILE), lambda bb, e, j, a, nt: (bb, e, 0, j)),
                      pl.BlockSpec((1, eg, 1, TILE), lambda bb, e, j, a, nt: (bb, e, 0, tile_off + j)),
                      pl.BlockSpec((1, TILE, d), lambda bb, e, j, a, nt: (bb, tile_off + j, 0))],
            out_specs=[lists(d), lists(1)]),
        compiler_params=_params(3),
        name="expert_gather",
    )(astart, need_tail, gpos.reshape(b, ne, 1, n), afft.reshape(b, ne, 1, m), h2u)


def _ffn_kernel(x_ref, rw_ref, wg_ref, wu_ref, wd_ref, o_ref, wg_s, wu_s, wd_s):
    @pl.when((pl.program_id(1) == 0) & (pl.program_id(2) == 0))
    def _():
        wg_s[...] = wg_ref[0, 0].astype(BF16)
        wu_s[...] = wu_ref[0, 0].astype(BF16)
        wd_s[...] = wd_ref[0, 0].astype(BF16)

    x = x_ref[0, 0]
    gate = _dot(x, wg_s[...])
    up = _dot(x, wu_s[...])
    hid = (gate / (1.0 + jnp.exp(-gate)) * up).astype(BF16)
    o_ref[0, 0] = (_dot(hid, wd_s[...]) * rw_ref[0, 0]).astype(BF16)


def _expert_ffn(xs, ws, w_gate, w_up, w_down, layer):
    b, ne, cap, d = xs.shape
    tr = min(cap, 1024)
    wspec = lambda w: pl.BlockSpec((1, 1) + w.shape[2:], lambda e, bb, r: (layer, e, 0, 0))
    rows = lambda w: pl.BlockSpec((1, 1, tr, w), lambda e, bb, r: (bb, e, r, 0))
    return pl.pallas_call(
        _ffn_kernel,
        out_shape=jax.ShapeDtypeStruct(xs.shape, BF16),
        grid=(ne, b, cap // tr),
        in_specs=[rows(d), rows(1), wspec(w_gate), wspec(w_up), wspec(w_down)],
        out_specs=rows(d),
        scratch_shapes=[pltpu.VMEM(w.shape[2:], BF16) for w in (w_gate, w_up, w_down)],
        compiler_params=_params(3),
        name="expert_swiglu",
    )(xs, ws, w_gate, w_up, w_down)


def _combine_kernel(a_ref, t_ref, x_ref, g_ref, mod_ref, fg_ref, ye_ref, o_ref,
                    win_ref, tail_ref, tacc_ref, sem_ref, tsem_ref, *, ne, nblk, main, tail, final):
    b, j = pl.program_id(0), pl.program_id(1)
    step = b * nblk + j
    slot = step % 2

    def start_at(bb, jj, e):
        return pl.multiple_of(a_ref[(bb * ne + e) * nblk + jj], BF16_ROWS)

    def start_of(e):
        return start_at(b, j, e)

    def window_copy(bb, jj, e, sl):
        return pltpu.make_async_copy(ye_ref.at[bb, e, pl.ds(start_at(bb, jj, e), main)],
                                     win_ref.at[sl, e * main:(e + 1) * main], sem_ref.at[sl, e])

    @pl.when(step == 0)
    def _():
        for e in range(ne):
            window_copy(b, j, e, slot).start()

    @pl.when(step + 1 < pl.num_programs(0) * nblk)
    def _():
        nxt = step + 1
        for e in range(ne):
            window_copy(nxt // nblk, nxt % nblk, e, 1 - slot).start()

    t = x_ref.shape[1]

    def pick(experts, first, rows, windows):
        onehot = jnp.concatenate(
            [jnp.where(lax.broadcasted_iota(I32, (rows, t), 0) == g_ref[0, e] - (start_of(e) + first), 1.0, 0.0)
             .astype(BF16) for e in experts], axis=0)
        return lax.dot_general(onehot, windows, (((0,), (0,)), ((), ())), preferred_element_type=F32)

    if tail:
        tacc_ref[...] = jnp.zeros(tacc_ref.shape, F32)
        for e in range(ne):
            @pl.when(t_ref[(b * ne + e) * nblk + j] != 0)
            def _(e=e):
                cp = pltpu.make_async_copy(ye_ref.at[b, e, pl.ds(start_of(e) + main, tail)], tail_ref, tsem_ref.at[0])
                cp.start()
                cp.wait()
                tacc_ref[...] += pick([e], main, tail, tail_ref[...])

    for e in range(ne):
        window_copy(b, j, e, slot).wait()
    acc = pick(list(range(ne)), 0, main, win_ref[slot])
    if tail:
        acc = acc + tacc_ref[...]
    x = x_ref[0] + mod_ref[0, 0][5:6] * acc
    if final:
        x = x * lax.rsqrt(jnp.mean(x * x, axis=-1, keepdims=True) + EPS) * fg_ref[...]
    o_ref[0] = x


def _combine(xu, gpos, astart, need_tail, ye, modtab, final_g, tile_off, kind, final):
    b, m, d = xu.shape
    _, ne, n = gpos.shape
    cap = ye.shape[2]
    nblk = n // TILE
    main, tail, _ = _window(cap)
    out_spec = pl.BlockSpec((1, TILE, d), (lambda bb, j, a, nt: (bb, j, 0)) if final
                            else (lambda bb, j, a, nt: (bb, tile_off + j, 0)))
    return pl.pallas_call(
        functools.partial(_combine_kernel, ne=ne, nblk=nblk, main=main, tail=tail, final=final),
        out_shape=jax.ShapeDtypeStruct((b, n, d) if final else xu.shape, F32),
        grid_spec=pltpu.PrefetchScalarGridSpec(
            num_scalar_prefetch=2,
            grid=(b, nblk),
            in_specs=[pl.BlockSpec((1, TILE, d), lambda bb, j, a, nt: (bb, tile_off + j, 0)),
                      pl.BlockSpec((1, ne, 1, TILE), lambda bb, j, a, nt: (bb, 0, 0, j)),
                      pl.BlockSpec((1, 1, N_MOD, d), lambda bb, j, a, nt: (bb, kind, 0, 0)),
                      pl.BlockSpec((1, d), lambda bb, j, a, nt: (0, 0)),
                      pl.BlockSpec(memory_space=pl.ANY)],
            out_specs=out_spec,
            scratch_shapes=[pltpu.VMEM((2, ne * main, d), BF16), pltpu.VMEM((max(tail, BF16_ROWS), d), BF16),
                            pltpu.VMEM((TILE, d), F32), pltpu.SemaphoreType.DMA((2, ne)),
                            pltpu.SemaphoreType.DMA((1,))]),
        input_output_aliases={} if final else {2: 0},
        compiler_params=_params(2),
        name="expert_combine",
    )(astart, need_tail, xu, gpos.reshape(b, ne, 1, n), modtab, final_g, ye)


def _rope_tables(n):
    t = np.arange(n)
    row = (t // GRID_W).astype(np.float32)[:, None]
    col = (t % GRID_W).astype(np.float32)[:, None]
    inv = (ROPE_BASE ** (-np.arange(ROPE_PAIRS, dtype=np.float32) / ROPE_PAIRS)).astype(np.float32)
    ang = np.concatenate([row * inv, row * inv, col * inv, col * inv], axis=-1)
    cos, sin = np.cos(ang), np.sin(ang)
    half = (np.arange(HEAD_DIM) % (2 * ROPE_PAIRS)) < ROPE_PAIRS
    sin = np.where(half[None], -sin, sin)
    cos = np.concatenate([np.ones((CTX, HEAD_DIM)), cos], axis=0)
    sin = np.concatenate([np.zeros((CTX, HEAD_DIM)), sin], axis=0)
    tile2 = lambda a: jnp.asarray(np.concatenate([a, a], axis=1), F32)
    return tile2(cos), tile2(sin)


def _moe(xu, h2u, afft, modtab, weights, layer, final_g, final):
    b, m, d = xu.shape
    n = m - CTX
    w_gate, w_up, w_down = weights
    cap_lat = max(1, EC_CAPACITY * n // N_EXPERTS)
    cap_ctx = max(1, EC_CAPACITY * CTX // N_EXPERTS)
    sets = [afft[:, :, CTX:]]
    if not final:
        sets.append(jnp.concatenate([afft[:, :, :CTX], jnp.full((b, N_EXPERTS, n - CTX), -1.0, F32)], axis=2))
    gpos, gfull = _route(jnp.concatenate(sets, axis=0), b, cap_lat, cap_ctx)

    def run(idx, n_set, cap, tile_off, kind, x_in, fin):
        gp = gpos[idx * b:(idx + 1) * b, :, :n_set]
        gf = gfull[idx * b:(idx + 1) * b, :, :n_set]
        astart, need_tail = _windows(gf, cap)
        xs, ws = _gather(h2u, afft, gp, astart, need_tail, cap, tile_off, eg=8)
        ye = _expert_ffn(xs, ws, w_gate, w_up, w_down, layer)
        return _combine(x_in, gp, astart, need_tail, ye, modtab, final_g, tile_off, kind, fin)

    if final:
        return run(0, n, cap_lat, 1, 1, xu, True)
    xu = run(0, n, cap_lat, 1, 1, xu, False)
    return run(1, CTX, cap_ctx, 0, 0, xu, False)


def kernel(x, c, ctx, c_ctx, w_mod, b_mod, norm1_g, w_in, na_rpb, diff_lambda, diff_subln_g, swa_sink,
           w_branch_na, w_branch_diff, w_branch_swa, w_out, norm2_g, w_router, w_expert_gate, w_expert_up,
           w_expert_down, final_g):
    b, n, d = x.shape
    depth = w_mod.shape[0]
    assert ctx.shape[1] == CTX and d == D_MODEL and n % (2 * TILE) == 0 and n // TILE >= 3

    cin = jnp.concatenate([c, c_ctx[None], jnp.zeros((8 - b - 1, d), F32)], axis=0)
    mod_all = _modulation(cin, w_mod, b_mod)
    cos_u, sin_u = _rope_tables(n)
    xu = jnp.concatenate([ctx, x], axis=1)
    final_g2 = final_g.reshape(1, d)

    out = None
    for l in range(depth):
        final = l == depth - 1
        lam_init = 0.8 - 0.6 * math.exp(-0.3 * l)
        mod_l = mod_all[l].reshape(8, N_MOD, d)
        modtab = jnp.stack([jnp.broadcast_to(mod_l[b], (b, N_MOD, d)), mod_l[:b]], axis=1)
        lam_p = diff_lambda[l].astype(F32)
        lam = (jnp.exp(jnp.sum(lam_p[0] * lam_p[1])) - jnp.exp(jnp.sum(lam_p[2] * lam_p[3])) + lam_init).reshape(1)

        (q_na, k_na, v_na, q_d, k_d, v_d, q_s, k_s, v_s, gs) = _project(
            xu, modtab, norm1_g[l].reshape(1, d), w_in[l].astype(BF16), cos_u, sin_u)
        o_na = _na_attention(q_na, k_na, v_na, _na_bias_table(na_rpb[l]))
        o_d = _diff_attention(q_d, k_d, v_d, lam, diff_subln_g[l].reshape(1, DIFF_V), lam_init)
        o_s = _swa_attention(q_s, k_s, v_s, swa_sink[l].astype(F32))
        xu, h2u, afft = _merge(
            o_na, o_d, o_s, gs, xu, modtab, w_branch_na[l].astype(BF16), w_branch_diff[l].astype(BF16),
            w_branch_swa[l].astype(BF16), w_out[l].astype(BF16), norm2_g[l].reshape(1, d), w_router[l])
        weights = (w_expert_gate, w_expert_up, w_expert_down)
        res = _moe(xu, h2u, afft, modtab, weights, l, final_g2, final)
        if final:
            out = res
        else:
            xu = res
    return out
```

```python
import functools
import math

import numpy as np
import jax
import jax.numpy as jnp
from jax import lax
from jax.experimental import pallas as pl
from jax.experimental.pallas import tpu as pltpu

F32 = jnp.float32
BF16 = jnp.bfloat16
I32 = jnp.int32
HIGHEST = lax.Precision.HIGHEST

D_MODEL = 1024
CTX = 256
TILE = 256
GRID_W = 64
HEAD_DIM = 64
ROPE_PAIRS = HEAD_DIM // 4
ROPE_BASE = 10000.0
EPS = 1e-6
N_MOD = 6
NA_HEADS = 4
NA_WIN_ROWS = 8
NA_WIN_COLS = 16
NA_TILE_ROWS = TILE // GRID_W
DIFF_HEADS = 4
DIFF_V = 2 * HEAD_DIM
SWA_HEADS = 4
SWA_KV_HEADS = 2
SWA_WINDOW = 128
N_EXPERTS = 16
EC_CAPACITY = 2
NEG = -1e30
LANES = 128
MXU_TILE = 256
LOG2E = math.log2(math.e)
DIFF_VT_PAD = 16
DIFF_FAST_CHUNK = 1024
DIFF_TILES_PER_STEP = 2
DIFF_MAX_DENOMINATOR = 2.0 ** 64
BF16_ROWS = 16
WINDOW_MAIN_ROWS = 64

C_QNA, C_QD, C_QS = 0, 256, 768
C_KNA, C_KD, C_KS = 1024, 1280, 1792
C_VNA, C_VD, C_VS = 1920, 2176, 2688
C_GATE, C_END = 2816, 5888

VMEM_LIMIT = 56 * 1024 * 1024


def _params(n_axes, vmem=VMEM_LIMIT):
    return pltpu.CompilerParams(dimension_semantics=("arbitrary",) * n_axes, vmem_limit_bytes=vmem)


def _nt_dot(a, b):
    return lax.dot_general(a, b, (((1,), (1,)), ((), ())), preferred_element_type=F32)


def _dot(a, b):
    return jnp.dot(a, b, preferred_element_type=F32)


def _mod_kernel(c_ref, w_ref, b_ref, o_ref):
    c = c_ref[...]
    s = c / (1.0 + jnp.exp(-c))
    o_ref[0] = jnp.dot(s, w_ref[0], preferred_element_type=F32, precision=HIGHEST) + b_ref[0]


def _modulation(cin, w_mod, b_mod):
    depth, d, w = w_mod.shape
    tn = 1024
    return pl.pallas_call(
        _mod_kernel,
        out_shape=jax.ShapeDtypeStruct((depth, 8, w), F32),
        grid=(depth, w // tn),
        in_specs=[pl.BlockSpec((8, d), lambda l, j: (0, 0)),
                  pl.BlockSpec((1, d, tn), lambda l, j: (l, 0, j)),
                  pl.BlockSpec((1, 1, tn), lambda l, j: (l, 0, j))],
        out_specs=pl.BlockSpec((1, 8, tn), lambda l, j: (l, 0, j)),
        compiler_params=_params(2),
        name="modulation",
    )(cin, w_mod, b_mod.reshape(depth, 1, w))


def _row_modulation(mod_ref, first, rows):
    is_ctx = first + lax.broadcasted_iota(I32, (rows, 1), 0) < CTX
    return lambda k: jnp.where(is_ctx, mod_ref[0, 0, k:k + 1], mod_ref[0, 1, k:k + 1])


def _independent_parts(rows, parts=2):
    parts = parts if rows % (parts * BF16_ROWS) == 0 else 1
    return [slice(p * (rows // parts), (p + 1) * (rows // parts)) for p in range(parts)]


def _proj_kernel(x_ref, mod_ref, g_ref, w_ref, cos_ref, sin_ref,
                 qna_ref, kna_ref, vna_ref, qd_ref, kd_ref, vd_ref, qs_ref, ks_ref, vs_ref, gs_ref, vdt_ref):
    x = x_ref[0]
    mod = mod_ref[0, 0]
    y = x * lax.rsqrt(jnp.mean(x * x, axis=-1, keepdims=True) + EPS) * g_ref[...]
    h = (y * (1.0 + mod[1:2]) + mod[0:1]).astype(BF16)

    def proj(a, b):
        return _dot(h, w_ref[:, a:b])

    cos = cos_ref[...]
    sin = sin_ref[...]
    lane = lax.broadcasted_iota(I32, cos.shape, 1)
    first_half = (lane % (2 * ROPE_PAIRS)) < ROPE_PAIRS

    def rope(t):
        outs = []
        for j in range(t.shape[1] // LANES):
            c = t[:, j * LANES:(j + 1) * LANES]
            r = jnp.where(first_half, pltpu.roll(c, LANES - ROPE_PAIRS, 1), pltpu.roll(c, ROPE_PAIRS, 1))
            outs.append(c * cos + r * sin)
        return outs[0] if len(outs) == 1 else jnp.concatenate(outs, axis=1)

    scale = HEAD_DIM ** -0.5
    qna_ref[0] = (proj(C_QNA, C_QD) * scale).astype(BF16)
    qd_ref[0] = (rope(proj(C_QD, C_QS)) * (scale * LOG2E)).astype(BF16)
    qs_ref[0] = (rope(proj(C_QS, C_KNA)) * scale).astype(BF16)
    kna_ref[0] = proj(C_KNA, C_KD).astype(BF16)
    kd_ref[0] = rope(proj(C_KD, C_KS)).astype(BF16)
    ks_ref[0] = rope(proj(C_KS, C_VNA)).astype(BF16)
    vna_ref[0] = proj(C_VNA, C_VD).astype(BF16)
    vd_f32 = proj(C_VD, C_VS)
    pad_rows = (lax.broadcasted_iota(I32, (DIFF_VT_PAD, vd_f32.shape[0]), 0) == 0).astype(BF16)
    vdt_ref[0] = jnp.concatenate(
        [blk for hd in range(DIFF_HEADS)
         for blk in (vd_f32[:, hd * DIFF_V:(hd + 1) * DIFF_V].T.astype(BF16), pad_rows)], axis=0)
    vd = vd_f32.astype(BF16)
    ones = jnp.ones((vd.shape[0], DIFF_V), BF16)
    vd_ref[0] = jnp.concatenate(
        [blk for hd in range(DIFF_HEADS) for blk in (vd[:, hd * DIFF_V:(hd + 1) * DIFF_V], ones)], axis=1)
    vs_ref[0] = proj(C_VS, C_GATE).astype(BF16)
    gates = proj(C_GATE, C_END)
    gs_ref[0] = (1.0 / (1.0 + jnp.exp(-gates))).astype(BF16)


def _matmul_rows(m, limit):
    return max(t for t in range(LANES, limit + 1, LANES) if m % t == 0)


def _project(xu, modtab, g, w_in, cos_u, sin_u):
    b, m, d = xu.shape
    tm = TILE
    widths = (256, 256, 256, 512, 512, 2 * DIFF_HEADS * DIFF_V, 256, 128, 128, C_END - C_GATE)
    row = lambda w: pl.BlockSpec((1, tm, w), lambda bb, i: (bb, i, 0))
    vt_rows = DIFF_HEADS * (DIFF_V + DIFF_VT_PAD)
    return pl.pallas_call(
        _proj_kernel,
        out_shape=[jax.ShapeDtypeStruct((b, m, w), BF16) for w in widths]
                  + [jax.ShapeDtypeStruct((b, vt_rows, m), BF16)],
        grid=(b, m // tm),
        in_specs=[row(d),
                  pl.BlockSpec((1, 1, N_MOD, d), lambda bb, i: (bb, jnp.minimum(i, 1), 0, 0)),
                  pl.BlockSpec((1, d), lambda bb, i: (0, 0)),
                  pl.BlockSpec((d, C_END), lambda bb, i: (0, 0)),
                  pl.BlockSpec((tm, LANES), lambda bb, i: (i, 0)),
                  pl.BlockSpec((tm, LANES), lambda bb, i: (i, 0))],
        out_specs=[row(w) for w in widths] + [pl.BlockSpec((1, vt_rows, tm), lambda bb, i: (bb, 0, i))],
        compiler_params=_params(2),
        name="norm_project_rope",
    )(xu, modtab, g, w_in, cos_u, sin_u)


def _tile_kind(t, nb):
    return jnp.where(t == 0, 0, jnp.where(t == 1, 1, jnp.where(t == nb, 3, 2)))


def _local_tiles(prev_ref, cur_ref, next_ref, u):
    tiles = cur_ref.shape[1] // TILE

    def tile(s):
        if s < 0:
            return prev_ref[0]
        if s >= tiles:
            return next_ref[0]
        return cur_ref[0, s * TILE:(s + 1) * TILE]

    return jnp.concatenate([tile(u - 1), tile(u), tile(u + 1)], axis=0)


def _na_kernel(q_ref, kc_ref, kp_ref, kcur_ref, kn_ref, vc_ref, vp_ref, vcur_ref, vn_ref, bias_ref, o_ref, *, nb):
    tiles = q_ref.shape[1] // TILE
    kc, vc = kc_ref[0], vc_ref[0]
    for u in range(tiles):
        rows = slice(u * TILE, (u + 1) * TILE)
        kind = _tile_kind(pl.program_id(1) * tiles + u, nb)
        q = q_ref[0, rows]
        kl = _local_tiles(kp_ref, kcur_ref, kn_ref, u)
        vl = _local_tiles(vp_ref, vcur_ref, vn_ref, u)
        outs = []
        for hd in range(NA_HEADS):
            sl = slice(hd * HEAD_DIM, (hd + 1) * HEAD_DIM)
            qh = q[:, sl]
            s_c = _nt_dot(qh, kc[:, sl])
            s_l = _nt_dot(qh, kl[:, sl]) + bias_ref[kind, hd]
            mx = jnp.maximum(jnp.max(s_c, axis=1, keepdims=True), jnp.max(s_l, axis=1, keepdims=True))
            p_c = jnp.exp(s_c - mx)
            p_l = jnp.exp(s_l - mx)
            den = jnp.sum(p_c, axis=1, keepdims=True) + jnp.sum(p_l, axis=1, keepdims=True)
            o = _dot(p_c.astype(BF16), vc[:, sl]) + _dot(p_l.astype(BF16), vl[:, sl])
            outs.append(o / den)
        o_ref[0, rows] = jnp.concatenate(outs, axis=1).astype(BF16)


def _na_bias_table(rpb):
    tr, nk = NA_TILE_ROWS, 3 * NA_TILE_ROWS
    qc = np.arange(GRID_W)[:, None]
    kc = np.arange(GRID_W)[None, :]
    cstart = np.clip(qc - NA_WIN_COLS // 2, 0, GRID_W - NA_WIN_COLS)
    col_ok = (kc >= cstart) & (kc < cstart + NA_WIN_COLS)
    dc = np.clip(kc - qc, -(NA_WIN_COLS - 1), NA_WIN_COLS - 1) + NA_WIN_COLS - 1
    onehot = jnp.asarray(np.arange(2 * NA_WIN_COLS - 1)[:, None, None] == dc[None], F32)
    cols = jnp.einsum("hrc,cqk->hrqk", rpb.astype(F32), onehot, precision=HIGHEST)
    qr = np.arange(tr)[:, None]
    krow = np.arange(nk)[None, :] - tr
    dr = np.clip(krow - qr, -(NA_WIN_ROWS - 1), NA_WIN_ROWS - 1) + NA_WIN_ROWS - 1
    starts = (0 * qr, qr - NA_WIN_ROWS // 2, 0 * qr + tr - NA_WIN_ROWS)
    row_ok = np.stack([(krow >= st) & (krow < st + NA_WIN_ROWS) for st in starts])
    t = jnp.take(cols, jnp.asarray(dr.reshape(-1)), axis=1)
    t = t.reshape(NA_HEADS, tr, nk, GRID_W, GRID_W).transpose(0, 1, 3, 2, 4)
    ok = row_ok[:, None, :, None, :, None] & col_ok[None, None, None, :, None, :]
    t = jnp.where(ok, t[None], NEG).reshape(3, NA_HEADS, TILE, 3 * TILE)
    return jnp.concatenate([jnp.full_like(t[:1], NEG), t], axis=0)


def _local_attention_specs(nt, wq, wk):
    nb = nt - 1
    tiles = max(t for t in range(1, 5 + 1) if nt % t == 0)
    rows = pl.BlockSpec((1, tiles * TILE, wq), lambda bb, i: (bb, i, 0))
    one = lambda f: pl.BlockSpec((1, TILE, wk), f)
    kv = [one(lambda bb, i: (bb, 0, 0)),
          one(lambda bb, i: (bb, jnp.clip(i * tiles - 1, 1, nb), 0)),
          pl.BlockSpec((1, tiles * TILE, wk), lambda bb, i: (bb, i, 0)),
          one(lambda bb, i: (bb, jnp.clip((i + 1) * tiles, 1, nb), 0))]
    return tiles, rows, kv


def _resident(a):
    return pl.BlockSpec(a.shape, lambda bb, i: (0,) * a.ndim, pipeline_mode=pl.Buffered(1))


def _na_attention(q, k, v, bias):
    b, m, w = q.shape
    nt = m // TILE
    tiles, rows, kv_specs = _local_attention_specs(nt, w, w)
    return pl.pallas_call(
        functools.partial(_na_kernel, nb=nt - 1),
        out_shape=jax.ShapeDtypeStruct((b, m, w), BF16),
        grid=(b, nt // tiles),
        in_specs=[rows] + kv_specs + kv_specs + [_resident(bias)],
        out_specs=rows,
        compiler_params=_params(2),
        name="neighbourhood_attention",
    )(q, k, k, k, k, v, v, v, v, bias)


def _swa_kernel(sink_ref, q_ref, kc_ref, kp_ref, kcur_ref, kn_ref, vc_ref, vp_ref, vcur_ref, vn_ref, mask_ref,
                o_ref, *, nb):
    tiles = q_ref.shape[1] // TILE
    kc, vc = kc_ref[0], vc_ref[0]
    group = SWA_HEADS // SWA_KV_HEADS
    rows = group * TILE
    rid = lax.broadcasted_iota(I32, (rows, 1), 0)
    for u in range(tiles):
        kind = _tile_kind(pl.program_id(1) * tiles + u, nb)
        q = q_ref[0, u * TILE:(u + 1) * TILE]
        kl = _local_tiles(kp_ref, kcur_ref, kn_ref, u)
        vl = _local_tiles(vp_ref, vcur_ref, vn_ref, u)
        mask = jnp.concatenate([mask_ref[kind]] * group, axis=0)
        outs = []
        for g in range(SWA_KV_HEADS):
            ksl = slice(g * HEAD_DIM, (g + 1) * HEAD_DIM)
            qg = jnp.concatenate([q[:, (g * group + j) * HEAD_DIM:(g * group + j + 1) * HEAD_DIM]
                                  for j in range(group)], axis=0)
            sink = jnp.zeros((rows, 1), F32)
            for j in range(group):
                sink = jnp.where(rid // TILE == j, sink_ref[g * group + j], sink)
            s_c = _nt_dot(qg, kc[:, ksl])
            s_l = _nt_dot(qg, kl[:, ksl]) + mask
            mx = jnp.maximum(jnp.maximum(jnp.max(s_c, axis=1, keepdims=True), jnp.max(s_l, axis=1, keepdims=True)),
                             sink)
            p_c = jnp.exp(s_c - mx)
            p_l = jnp.exp(s_l - mx)
            den = jnp.sum(p_c, axis=1, keepdims=True) + jnp.sum(p_l, axis=1, keepdims=True) + jnp.exp(sink - mx)
            o = (_dot(p_c.astype(BF16), vc[:, ksl]) + _dot(p_l.astype(BF16), vl[:, ksl])) / den
            outs.extend(o[j * TILE:(j + 1) * TILE] for j in range(group))
        o_ref[0, u * TILE:(u + 1) * TILE] = jnp.concatenate(outs, axis=1).astype(BF16)


def _swa_mask_table():
    qpos = np.arange(TILE)[:, None]
    kpos = np.arange(3 * TILE)[None, :] - TILE
    near = np.abs(qpos - kpos) <= SWA_WINDOW
    kinds = [np.zeros_like(near), near & (kpos >= 0), near, near & (kpos < TILE)]
    return jnp.asarray(np.where(np.stack(kinds), 0.0, NEG), F32)


def _swa_attention(q, k, v, sink):
    b, m, wq = q.shape
    nt = m // TILE
    tiles, rows, kv_specs = _local_attention_specs(nt, wq, k.shape[2])
    mask = _swa_mask_table()
    return pl.pallas_call(
        functools.partial(_swa_kernel, nb=nt - 1),
        out_shape=jax.ShapeDtypeStruct((b, m, wq), BF16),
        grid=(b, nt // tiles),
        in_specs=[pl.BlockSpec(memory_space=pltpu.SMEM), rows] + kv_specs + kv_specs + [_resident(mask)],
        out_specs=rows,
        compiler_params=_params(2),
        name="windowed_attention",
    )(sink, q, k, k, k, k, v, v, v, v, mask)


def _diff_kernel(lam_ref, q_ref, k_ref, v_ref, vt_ref, g_ref, gt_ref, o_ref, acc_ref, m_ref, s_ref, cmax_ref,
                 *, n_chunks, kblk, fast_kblk, lam_init):
    i = pl.program_id(2)
    parts = q_ref.shape[1] // TILE

    def stacked(part):
        q = q_ref[0, part * TILE:(part + 1) * TILE]
        lane = lax.broadcasted_iota(I32, q.shape, 1)
        zero = jnp.zeros_like(q)
        return jnp.concatenate([jnp.where(lane < HEAD_DIM, q, zero), jnp.where(lane >= HEAD_DIM, q, zero)], axis=0)

    def chunk(c):
        return pl.ds(pl.multiple_of(c * kblk, LANES), kblk)

    def accumulate(s, cmax, v):
        m_prev = m_ref[...]
        m_new = jnp.maximum(m_prev, cmax)
        p = jnp.exp2(s - m_new).astype(BF16)
        acc_ref[...] = jnp.exp2(m_prev - m_new) * acc_ref[...] + _dot(p, v)
        m_ref[...] = m_new

    def reset():
        m_ref[...] = jnp.full(m_ref.shape, NEG, F32)
        acc_ref[...] = jnp.zeros(acc_ref.shape, F32)

    def safe_sweep(qq):
        def scores(c, slot):
            s = _nt_dot(qq, k_ref[0, chunk(c), :])
            s_ref[slot] = s
            cmax_ref[slot] = jnp.max(s, axis=1, keepdims=True)

        reset()
        scores(0, 0)

        def pair(c, last):
            for cur in (0, 1):
                if not (last and cur == 1):
                    scores(c + cur + 1, 1 - cur)
                accumulate(s_ref[cur], cmax_ref[cur], v_ref[0, chunk(c + cur), :])

        def body(c2, carry):
            pair(2 * c2, False)
            return carry

        lax.fori_loop(0, n_chunks // 2 - 1, body, 0)
        pair(n_chunks - 2, True)
        return acc_ref[...]

    def fast_sweep(qq):
        bounds = [0, CTX] + list(range(CTX + fast_kblk, n_chunks * kblk, fast_kblk)) + [n_chunks * kblk]
        keys = lambda c: slice(bounds[c], bounds[c + 1])
        s = _nt_dot(k_ref[0, keys(0), :], qq)
        m0 = jnp.max(s, axis=0, keepdims=True)
        acc = _dot(vt_ref[0, :, keys(0)], jnp.exp2(s - m0).astype(BF16))
        for c in range(1, len(bounds) - 1):
            s = _nt_dot(k_ref[0, keys(c), :], qq)
            acc = acc + _dot(vt_ref[0, :, keys(c)], jnp.exp2(s - m0).astype(BF16))
        return acc

    def finish_transposed(acc):
        o = acc[:DIFF_V] / acc[DIFF_V:DIFF_V + 1]
        d = o[:, :TILE] - lam_ref[0] * o[:, TILE:]
        y = d * lax.rsqrt(jnp.mean(d * d, axis=0, keepdims=True) + EPS) * gt_ref[...]
        return (y * (1.0 - lam_init)).T.astype(BF16)

    def finish(acc):
        o = acc[:, :DIFF_V] / acc[:, DIFF_V:]
        d = o[:TILE] - lam_ref[0] * o[TILE:]
        y = d * lax.rsqrt(jnp.mean(d * d, axis=-1, keepdims=True) + EPS) * g_ref[...]
        return (y * (1.0 - lam_init)).astype(BF16)

    def rows(part):
        return pl.ds(part * TILE, TILE)

    @pl.when(i == 0)
    def _():
        o_ref[0, 0:(parts - 1) * TILE] = jnp.zeros(((parts - 1) * TILE, DIFF_V), BF16)
        reset()
        s = _nt_dot(stacked(parts - 1), k_ref[0, 0:CTX, :])
        accumulate(s, jnp.max(s, axis=1, keepdims=True), v_ref[0, 0:CTX, :])
        o_ref[0, rows(parts - 1)] = finish(acc_ref[...])

    @pl.when(i > 0)
    def _():
        accs = [fast_sweep(stacked(part)) for part in range(parts)]
        for part in range(parts):
            o_ref[0, rows(part)] = finish_transposed(accs[part])
        bad = [jnp.logical_not(jnp.max(acc[DIFF_V:DIFF_V + 1]) < DIFF_MAX_DENOMINATOR) for acc in accs]
        for part in range(parts):
            @pl.when(bad[part])
            def _(part=part):
                o_ref[0, rows(part)] = finish(safe_sweep(stacked(part)))


def _diff_chunking(m, max_chunk=2048):
    for n_chunks in range(2, m // LANES + 1, 2):
        if m % (n_chunks * LANES) == 0 and m // n_chunks <= max_chunk:
            return n_chunks, m // n_chunks
    raise ValueError(f"no chunking for {m} keys")


def _diff_attention(q, k, v, vt, lam, g, lam_init):
    b, m, _ = q.shape
    vt_rows = DIFF_V + DIFF_VT_PAD
    n_chunks, kblk = _diff_chunking(m)
    fast_kblk = DIFF_FAST_CHUNK
    parts = DIFF_TILES_PER_STEP
    pad = (parts - 1) * TILE
    assert (m + pad) % (parts * TILE) == 0
    q_pad = jnp.pad(q, ((0, 0), (pad, 0), (0, 0)))
    out = pl.pallas_call(
        functools.partial(_diff_kernel, n_chunks=n_chunks, kblk=kblk, fast_kblk=fast_kblk, lam_init=lam_init),
        out_shape=jax.ShapeDtypeStruct((b, m + pad, DIFF_HEADS * DIFF_V), BF16),
        grid=(b, DIFF_HEADS, (m + pad) // (parts * TILE)),
        in_specs=[pl.BlockSpec(memory_space=pltpu.SMEM),
                  pl.BlockSpec((1, parts * TILE, 2 * HEAD_DIM), lambda bb, hh, i: (bb, i, hh)),
                  pl.BlockSpec((1, m, 2 * HEAD_DIM), lambda bb, hh, i: (bb, 0, hh)),
                  pl.BlockSpec((1, m, 2 * DIFF_V), lambda bb, hh, i: (bb, 0, hh), pipeline_mode=pl.Buffered(1)),
                  pl.BlockSpec((1, vt_rows, m), lambda bb, hh, i: (bb, hh, 0)),
                  pl.BlockSpec((1, DIFF_V), lambda bb, hh, i: (0, 0)),
                  pl.BlockSpec((DIFF_V, 1), lambda bb, hh, i: (0, 0))],
        out_specs=pl.BlockSpec((1, parts * TILE, DIFF_V), lambda bb, hh, i: (bb, i, hh)),
        scratch_shapes=[pltpu.VMEM((2 * TILE, 2 * DIFF_V), F32), pltpu.VMEM((2 * TILE, 1), F32),
                        pltpu.VMEM((2, 2 * TILE, kblk), F32), pltpu.VMEM((2, 2 * TILE, 1), F32)],
        compiler_params=_params(3),
        name="differential_attention",
    )(lam, q_pad, k, v, vt, g, g.reshape(DIFF_V, 1))
    return out[:, pad:]


def _merge_kernel(ona_ref, od_ref, osw_ref, gs_ref, x_ref, mod_ref, wna_ref, wd_ref, wsw_ref, wo_ref,
                  g2_ref, wrt_ref, xo_ref, h2_ref, afft_ref):
    tm, d = x_ref.shape[1:]
    h2_parts = []
    for rows in _independent_parts(tm):
        gs = gs_ref[0, rows]
        y = (gs[:, :d].astype(F32) * _dot(ona_ref[0, rows], wna_ref[...])
             + gs[:, d:2 * d].astype(F32) * _dot(od_ref[0, rows], wd_ref[...])
             + gs[:, 2 * d:].astype(F32) * _dot(osw_ref[0, rows], wsw_ref[...]))
        mod = _row_modulation(mod_ref, pl.program_id(1) * tm + rows.start, rows.stop - rows.start)
        xn = x_ref[0, rows] + mod(2) * _dot(y.astype(BF16), wo_ref[...])
        xo_ref[0, rows] = xn
        r = xn * lax.rsqrt(jnp.mean(xn * xn, axis=-1, keepdims=True) + EPS) * g2_ref[...]
        h2 = r * (1.0 + mod(4)) + mod(3)
        h2_ref[0, rows] = h2.astype(BF16)
        h2_parts.append(h2)
    h2 = jnp.concatenate(h2_parts, axis=0)
    lt = lax.dot_general(wrt_ref[...], h2, (((1,), (1,)), ((), ())),
                         preferred_element_type=F32, precision=HIGHEST)
    et = jnp.exp(lt - jnp.max(lt, axis=0, keepdims=True))
    afft_ref[0] = et / jnp.sum(et, axis=0, keepdims=True)


def _merge(o_na, o_d, o_sw, gs, xu, modtab, w_na, w_d, w_sw, w_o, g2, w_r):
    b, m, d = xu.shape
    tm = _matmul_rows(m, 640)
    ne = w_r.shape[1]
    row = lambda w: pl.BlockSpec((1, tm, w), lambda bb, i: (bb, i, 0))
    full = lambda a: pl.BlockSpec(a.shape, lambda bb, i: (0,) * a.ndim)
    w_rt = w_r.T
    return pl.pallas_call(
        _merge_kernel,
        out_shape=[jax.ShapeDtypeStruct((b, m, d), F32), jax.ShapeDtypeStruct((b, m, d), BF16),
                   jax.ShapeDtypeStruct((b, ne, m), F32)],
        grid=(b, m // tm),
        in_specs=[row(o_na.shape[2]), row(o_d.shape[2]), row(o_sw.shape[2]), row(gs.shape[2]), row(d),
                  pl.BlockSpec((1, 2, N_MOD, d), lambda bb, i: (bb, 0, 0, 0)),
                  full(w_na), full(w_d), full(w_sw), full(w_o), full(g2), full(w_rt)],
        out_specs=[row(d), row(d), pl.BlockSpec((1, ne, tm), lambda bb, i: (bb, 0, i))],
        compiler_params=_params(2),
        name="merge_residual_router",
    )(o_na, o_d, o_sw, gs, xu, modtab, w_na, w_d, w_sw, w_o, g2, w_rt)


def _route_kernel(a_ref, gpos_ref, gfull_ref, *, n_latent_sets, cap_lat, cap_ctx):
    ne, c, w = a_ref.shape[1:]
    a = a_ref[0]
    bits = lax.bitcast_convert_type(a, I32)
    cap = jnp.where(pl.program_id(0) < n_latent_sets, cap_lat, cap_ctx).astype(F32)

    def count(mask):
        return jnp.sum(jnp.sum(mask.astype(F32), axis=2, keepdims=True), axis=1, keepdims=True)

    def search(it, thr):
        cand = thr | jnp.left_shift(jnp.int32(1), 30 - it)
        return jnp.where(count(bits >= cand) >= cap, cand, thr)

    thr = lax.fori_loop(0, 31, search, jnp.zeros((ne, 1, 1), I32))
    gt = bits > thr
    eq = bits == thr
    need = cap - count(gt)

    upper = (lax.broadcasted_iota(I32, (w, w), 0) <= lax.broadcasted_iota(I32, (w, w), 1)).astype(BF16)
    lower = (lax.broadcasted_iota(I32, (c, c), 1) < lax.broadcasted_iota(I32, (c, c), 0)).astype(BF16)

    def exclusive_cumsum(mask):
        x = mask.astype(F32).reshape(ne * c, w)
        within = _dot(x.astype(BF16), upper)
        tot = jnp.broadcast_to(within[:, w - 1:w], (ne * c, w)).astype(BF16)
        before = jnp.concatenate([_dot(lower, tot[e * c:(e + 1) * c]) for e in range(ne)], axis=0)
        return (before + within - x).reshape(ne, c, w)

    sel = gt | (eq & (exclusive_cumsum(eq) < need))
    g = exclusive_cumsum(sel).astype(I32)
    gfull_ref[0] = g
    gpos_ref[0] = jnp.where(sel, g, -1)


def _route(aff_sets, n_latent_sets, cap_lat, cap_ctx):
    s, ne, n = aff_sets.shape
    c = n // LANES
    a4 = aff_sets.reshape(s, ne, c, LANES)
    blk = pl.BlockSpec((1, ne, c, LANES), lambda i: (i, 0, 0, 0))
    gpos, gfull = pl.pallas_call(
        functools.partial(_route_kernel, n_latent_sets=n_latent_sets, cap_lat=cap_lat, cap_ctx=cap_ctx),
        out_shape=[jax.ShapeDtypeStruct(a4.shape, I32)] * 2,
        grid=(s,),
        in_specs=[blk],
        out_specs=[blk, blk],
        compiler_params=_params(1),
        name="expert_choice_select",
    )(a4)
    return gpos.reshape(s, ne, n), gfull.reshape(s, ne, n)


def _window(cap):
    w = min(TILE + BF16_ROWS, cap)
    main = min(WINDOW_MAIN_ROWS, w)
    return main, w - main, cap - w


def _windows(gfull, cap):
    b, ne, _ = gfull.shape
    main, _, max_start = _window(cap)
    seg_start = gfull[:, :, ::TILE]
    seg_end = jnp.concatenate([gfull[:, :, TILE::TILE], jnp.full((b, ne, 1), cap, I32)], axis=2)
    astart = jnp.minimum(seg_start // BF16_ROWS * BF16_ROWS, max_start)
    return astart.reshape(-1), (seg_end > astart + main).astype(I32).reshape(-1)


def _gather_kernel(a_ref, t_ref, g_ref, aff_ref, h_ref, o_ref, w_ref, *, ne, nblk, eg, main, tail):
    b, egi, j = pl.program_id(0), pl.program_id(1), pl.program_id(2)

    @pl.when(j == 0)
    def _():
        o_ref[...] = jnp.zeros(o_ref.shape, o_ref.dtype)
        w_ref[...] = jnp.zeros(w_ref.shape, w_ref.dtype)

    hb = h_ref[0]
    t = hb.shape[0]

    def idx(k):
        return (b * ne + egi * eg + k) * nblk + j

    def place(experts, first, rows):
        hits = [lax.broadcasted_iota(I32, (rows, t), 0) == g_ref[0, k] - (a_ref[idx(k)] + first) for k in experts]
        onehot = jnp.concatenate([jnp.where(h, 1.0, 0.0).astype(BF16) for h in hits], axis=0)
        picked = _dot(onehot, hb).astype(BF16)
        for n, (k, hit) in enumerate(zip(experts, hits)):
            weight = jnp.sum(jnp.where(hit, aff_ref[0, k], 0.0), axis=1, keepdims=True)
            sl = (0, k, pl.ds(pl.multiple_of(a_ref[idx(k)] + first, BF16_ROWS), rows), slice(None))
            o_ref[sl] = o_ref[sl] + picked[n * rows:(n + 1) * rows]
            w_ref[sl] = w_ref[sl] + weight

    place(list(range(eg)), 0, main)
    for k in range(eg if tail else 0):
        @pl.when(t_ref[idx(k)] != 0)
        def _(k=k):
            place([k], main, tail)


def _gather(h2u, afft, gpos, astart, need_tail, cap, tile_off, eg):
    b, ne, n = gpos.shape
    m, d = h2u.shape[1:]
    nblk = n // TILE
    main, tail, _ = _window(cap)
    lists = lambda w: pl.BlockSpec((1, eg, cap, w), lambda bb, e, j, a, nt: (bb, e, 0, 0),
                                   pipeline_mode=pl.Buffered(1))
    return pl.pallas_call(
        functools.partial(_gather_kernel, ne=ne, nblk=nblk, eg=eg, main=main, tail=tail),
        out_shape=[jax.ShapeDtypeStruct((b, ne, cap, d), BF16), jax.ShapeDtypeStruct((b, ne, cap, 1), F32)],
        grid_spec=pltpu.PrefetchScalarGridSpec(
            num_scalar_prefetch=2,
            grid=(b, ne // eg, nblk),
            in_specs=[pl.BlockSpec((1, eg, 1, TILE), lambda bb, e, j, a, nt: (bb, e, 0, j)),
                      pl.BlockSpec((1, eg, 1, TILE), lambda bb, e, j, a, nt: (bb, e, 0, tile_off + j)),
                      pl.BlockSpec((1, TILE, d), lambda bb, e, j, a, nt: (bb, tile_off + j, 0))],
            out_specs=[lists(d), lists(1)]),
        compiler_params=_params(3),
        name="expert_gather",
    )(astart, need_tail, gpos.reshape(b, ne, 1, n), afft.reshape(b, ne, 1, m), h2u)


def _ffn_kernel(x_ref, rw_ref, wg_ref, wu_ref, wd_ref, o_ref, wg_s, wu_s, wd_s):
    @pl.when((pl.program_id(1) == 0) & (pl.program_id(2) == 0))
    def _():
        wg_s[...] = wg_ref[0, 0].astype(BF16)
        wu_s[...] = wu_ref[0, 0].astype(BF16)
        wd_s[...] = wd_ref[0, 0].astype(BF16)

    x = x_ref[0, 0]
    gate = _dot(x, wg_s[...])
    up = _dot(x, wu_s[...])
    hid = (gate / (1.0 + jnp.exp(-gate)) * up).astype(BF16)
    o_ref[0, 0] = (_dot(hid, wd_s[...]) * rw_ref[0, 0]).astype(BF16)


def _expert_ffn(xs, ws, w_gate, w_up, w_down, layer):
    b, ne, cap, d = xs.shape
    tr = min(cap, 1024)
    wspec = lambda w: pl.BlockSpec((1, 1) + w.shape[2:], lambda e, bb, r: (layer, e, 0, 0))
    rows = lambda w: pl.BlockSpec((1, 1, tr, w), lambda e, bb, r: (bb, e, r, 0))
    return pl.pallas_call(
        _ffn_kernel,
        out_shape=jax.ShapeDtypeStruct(xs.shape, BF16),
        grid=(ne, b, cap // tr),
        in_specs=[rows(d), rows(1), wspec(w_gate), wspec(w_up), wspec(w_down)],
        out_specs=rows(d),
        scratch_shapes=[pltpu.VMEM(w.shape[2:], BF16) for w in (w_gate, w_up, w_down)],
        compiler_params=_params(3),
        name="expert_swiglu",
    )(xs, ws, w_gate, w_up, w_down)


def _combine_kernel(a_ref, t_ref, x_ref, g_ref, mod_ref, fg_ref, ye_ref, o_ref,
                    win_ref, tail_ref, tacc_ref, sem_ref, tsem_ref, *, ne, nblk, main, tail, final):
    b, j = pl.program_id(0), pl.program_id(1)
    step = b * nblk + j
    slot = step % 2

    def start_at(bb, jj, e):
        return pl.multiple_of(a_ref[(bb * ne + e) * nblk + jj], BF16_ROWS)

    def start_of(e):
        return start_at(b, j, e)

    def window_copy(bb, jj, e, sl):
        return pltpu.make_async_copy(ye_ref.at[bb, e, pl.ds(start_at(bb, jj, e), main)],
                                     win_ref.at[sl, e * main:(e + 1) * main], sem_ref.at[sl, e])

    @pl.when(step == 0)
    def _():
        for e in range(ne):
            window_copy(b, j, e, slot).start()

    @pl.when(step + 1 < pl.num_programs(0) * nblk)
    def _():
        nxt = step + 1
        for e in range(ne):
            window_copy(nxt // nblk, nxt % nblk, e, 1 - slot).start()

    t = x_ref.shape[1]

    def pick(experts, first, rows, windows):
        onehot = jnp.concatenate(
            [jnp.where(lax.broadcasted_iota(I32, (rows, t), 0) == g_ref[0, e] - (start_of(e) + first), 1.0, 0.0)
             .astype(BF16) for e in experts], axis=0)
        return lax.dot_general(onehot, windows, (((0,), (0,)), ((), ())), preferred_element_type=F32)

    if tail:
        tacc_ref[...] = jnp.zeros(tacc_ref.shape, F32)
        for e in range(ne):
            @pl.when(t_ref[(b * ne + e) * nblk + j] != 0)
            def _(e=e):
                cp = pltpu.make_async_copy(ye_ref.at[b, e, pl.ds(start_of(e) + main, tail)], tail_ref, tsem_ref.at[0])
                cp.start()
                cp.wait()
                tacc_ref[...] += pick([e], main, tail, tail_ref[...])

    for e in range(ne):
        window_copy(b, j, e, slot).wait()
    acc = pick(list(range(ne)), 0, main, win_ref[slot])
    if tail:
        acc = acc + tacc_ref[...]
    x = x_ref[0] + mod_ref[0, 0][5:6] * acc
    if final:
        x = x * lax.rsqrt(jnp.mean(x * x, axis=-1, keepdims=True) + EPS) * fg_ref[...]
    o_ref[0] = x


def _combine(xu, gpos, astart, need_tail, ye, modtab, final_g, tile_off, kind, final):
    b, m, d = xu.shape
    _, ne, n = gpos.shape
    cap = ye.shape[2]
    nblk = n // TILE
    main, tail, _ = _window(cap)
    out_spec = pl.BlockSpec((1, TILE, d), (lambda bb, j, a, nt: (bb, j, 0)) if final
                            else (lambda bb, j, a, nt: (bb, tile_off + j, 0)))
    return pl.pallas_call(
        functools.partial(_combine_kernel, ne=ne, nblk=nblk, main=main, tail=tail, final=final),
        out_shape=jax.ShapeDtypeStruct((b, n, d) if final else xu.shape, F32),
        grid_spec=pltpu.PrefetchScalarGridSpec(
            num_scalar_prefetch=2,
            grid=(b, nblk),
            in_specs=[pl.BlockSpec((1, TILE, d), lambda bb, j, a, nt: (bb, tile_off + j, 0)),
                      pl.BlockSpec((1, ne, 1, TILE), lambda bb, j, a, nt: (bb, 0, 0, j)),
                      pl.BlockSpec((1, 1, N_MOD, d), lambda bb, j, a, nt: (bb, kind, 0, 0)),
                      pl.BlockSpec((1, d), lambda bb, j, a, nt: (0, 0)),
                      pl.BlockSpec(memory_space=pl.ANY)],
            out_specs=out_spec,
            scratch_shapes=[pltpu.VMEM((2, ne * main, d), BF16), pltpu.VMEM((max(tail, BF16_ROWS), d), BF16),
                            pltpu.VMEM((TILE, d), F32), pltpu.SemaphoreType.DMA((2, ne)),
                            pltpu.SemaphoreType.DMA((1,))]),
        input_output_aliases={} if final else {2: 0},
        compiler_params=_params(2),
        name="expert_combine",
    )(astart, need_tail, xu, gpos.reshape(b, ne, 1, n), modtab, final_g, ye)


def _rope_tables(n):
    t = np.arange(n)
    row = (t // GRID_W).astype(np.float32)[:, None]
    col = (t % GRID_W).astype(np.float32)[:, None]
    inv = (ROPE_BASE ** (-np.arange(ROPE_PAIRS, dtype=np.float32) / ROPE_PAIRS)).astype(np.float32)
    ang = np.concatenate([row * inv, row * inv, col * inv, col * inv], axis=-1)
    cos, sin = np.cos(ang), np.sin(ang)
    half = (np.arange(HEAD_DIM) % (2 * ROPE_PAIRS)) < ROPE_PAIRS
    sin = np.where(half[None], -sin, sin)
    cos = np.concatenate([np.ones((CTX, HEAD_DIM)), cos], axis=0)
    sin = np.concatenate([np.zeros((CTX, HEAD_DIM)), sin], axis=0)
    tile2 = lambda a: jnp.asarray(np.concatenate([a, a], axis=1), F32)
    return tile2(cos), tile2(sin)


def _moe(xu, h2u, afft, modtab, weights, layer, final_g, final):
    b, m, d = xu.shape
    n = m - CTX
    w_gate, w_up, w_down = weights
    cap_lat = max(1, EC_CAPACITY * n // N_EXPERTS)
    cap_ctx = max(1, EC_CAPACITY * CTX // N_EXPERTS)
    sets = [afft[:, :, CTX:]]
    if not final:
        sets.append(jnp.concatenate([afft[:, :, :CTX], jnp.full((b, N_EXPERTS, n - CTX), -1.0, F32)], axis=2))
    gpos, gfull = _route(jnp.concatenate(sets, axis=0), b, cap_lat, cap_ctx)

    def run(idx, n_set, cap, tile_off, kind, x_in, fin):
        gp = gpos[idx * b:(idx + 1) * b, :, :n_set]
        gf = gfull[idx * b:(idx + 1) * b, :, :n_set]
        astart, need_tail = _windows(gf, cap)
        xs, ws = _gather(h2u, afft, gp, astart, need_tail, cap, tile_off, eg=8)
        ye = _expert_ffn(xs, ws, w_gate, w_up, w_down, layer)
        return _combine(x_in, gp, astart, need_tail, ye, modtab, final_g, tile_off, kind, fin)

    if final:
        return run(0, n, cap_lat, 1, 1, xu, True)
    xu = run(0, n, cap_lat, 1, 1, xu, False)
    return run(1, CTX, cap_ctx, 0, 0, xu, False)


def kernel(x, c, ctx, c_ctx, w_mod, b_mod, norm1_g, w_in, na_rpb, diff_lambda, diff_subln_g, swa_sink,
           w_branch_na, w_branch_diff, w_branch_swa, w_out, norm2_g, w_router, w_expert_gate, w_expert_up,
           w_expert_down, final_g):
    b, n, d = x.shape
    depth = w_mod.shape[0]
    assert ctx.shape[1] == CTX and d == D_MODEL and n % (2 * TILE) == 0 and n // TILE >= 3

    cin = jnp.concatenate([c, c_ctx[None], jnp.zeros((8 - b - 1, d), F32)], axis=0)
    mod_all = _modulation(cin, w_mod, b_mod)
    cos_u, sin_u = _rope_tables(n)
    xu = jnp.concatenate([ctx, x], axis=1)
    final_g2 = final_g.reshape(1, d)

    out = None
    for l in range(depth):
        final = l == depth - 1
        lam_init = 0.8 - 0.6 * math.exp(-0.3 * l)
        mod_l = mod_all[l].reshape(8, N_MOD, d)
        modtab = jnp.stack([jnp.broadcast_to(mod_l[b], (b, N_MOD, d)), mod_l[:b]], axis=1)
        lam_p = diff_lambda[l].astype(F32)
        lam = (jnp.exp(jnp.sum(lam_p[0] * lam_p[1])) - jnp.exp(jnp.sum(lam_p[2] * lam_p[3])) + lam_init).reshape(1)

        (q_na, k_na, v_na, q_d, k_d, v_d, q_s, k_s, v_s, gs, v_dt) = _project(
            xu, modtab, norm1_g[l].reshape(1, d), w_in[l].astype(BF16), cos_u, sin_u)
        o_na = _na_attention(q_na, k_na, v_na, _na_bias_table(na_rpb[l]))
        o_d = _diff_attention(q_d, k_d, v_d, v_dt, lam, diff_subln_g[l].reshape(1, DIFF_V), lam_init)
        o_s = _swa_attention(q_s, k_s, v_s, swa_sink[l].astype(F32))
        xu, h2u, afft = _merge(
            o_na, o_d, o_s, gs, xu, modtab, w_branch_na[l].astype(BF16), w_branch_diff[l].astype(BF16),
            w_branch_swa[l].astype(BF16), w_out[l].astype(BF16), norm2_g[l].reshape(1, d), w_router[l])
        weights = (w_expert_gate, w_expert_up, w_expert_down)
        res = _moe(xu, h2u, afft, modtab, weights, l, final_g2, final)
        if final:
            out = res
        else:
            xu = res
    return out
```

```python
import functools
import math

import numpy as np
import jax
import jax.numpy as jnp
from jax import lax
from jax.experimental import pallas as pl
from jax.experimental.pallas import tpu as pltpu

F32 = jnp.float32
BF16 = jnp.bfloat16
I32 = jnp.int32
HIGHEST = lax.Precision.HIGHEST

D_MODEL = 1024
CTX = 256
TILE = 256
GRID_W = 64
HEAD_DIM = 64
ROPE_PAIRS = HEAD_DIM // 4
ROPE_BASE = 10000.0
EPS = 1e-6
N_MOD = 6
NA_HEADS = 4
NA_WIN_ROWS = 8
NA_WIN_COLS = 16
NA_TILE_ROWS = TILE // GRID_W
DIFF_HEADS = 4
DIFF_V = 2 * HEAD_DIM
SWA_HEADS = 4
SWA_KV_HEADS = 2
SWA_WINDOW = 128
N_EXPERTS = 16
EC_CAPACITY = 2
NEG = -1e30
LANES = 128
MXU_TILE = 256
LOG2E = math.log2(math.e)
DIFF_VT_PAD = 16
DIFF_FAST_CHUNK = 4 * MXU_TILE
DIFF_TILES_PER_STEP = 2
DIFF_MAX_DENOMINATOR = 2.0 ** 64
BF16_ROWS = 16
WINDOW_MAIN_ROWS = 64

C_QNA, C_QD, C_QS = 0, 256, 768
C_KNA, C_KD, C_KS = 1024, 1280, 1792
C_VNA, C_VD, C_VS = 1920, 2176, 2688
C_GATE, C_END = 2816, 5888

VMEM_LIMIT = 56 * 1024 * 1024


def _params(n_axes, vmem=VMEM_LIMIT):
    return pltpu.CompilerParams(dimension_semantics=("arbitrary",) * n_axes, vmem_limit_bytes=vmem)


def _nt_dot(a, b):
    return lax.dot_general(a, b, (((1,), (1,)), ((), ())), preferred_element_type=F32)


def _dot(a, b):
    return jnp.dot(a, b, preferred_element_type=F32)


def _mod_kernel(c_ref, w_ref, b_ref, o_ref):
    c = c_ref[...]
    s = c / (1.0 + jnp.exp(-c))
    o_ref[0] = jnp.dot(s, w_ref[0], preferred_element_type=F32, precision=HIGHEST) + b_ref[0]


def _modulation(cin, w_mod, b_mod):
    depth, d, w = w_mod.shape
    tn = 1024
    return pl.pallas_call(
        _mod_kernel,
        out_shape=jax.ShapeDtypeStruct((depth, 8, w), F32),
        grid=(depth, w // tn),
        in_specs=[pl.BlockSpec((8, d), lambda l, j: (0, 0)),
                  pl.BlockSpec((1, d, tn), lambda l, j: (l, 0, j)),
                  pl.BlockSpec((1, 1, tn), lambda l, j: (l, 0, j))],
        out_specs=pl.BlockSpec((1, 8, tn), lambda l, j: (l, 0, j)),
        compiler_params=_params(2),
        name="modulation",
    )(cin, w_mod, b_mod.reshape(depth, 1, w))


def _row_modulation(mod_ref, first, rows):
    is_ctx = first + lax.broadcasted_iota(I32, (rows, 1), 0) < CTX
    return lambda k: jnp.where(is_ctx, mod_ref[0, 0, k:k + 1], mod_ref[0, 1, k:k + 1])


def _independent_parts(rows, parts=2):
    parts = parts if rows % (parts * BF16_ROWS) == 0 else 1
    return [slice(p * (rows // parts), (p + 1) * (rows // parts)) for p in range(parts)]


def _proj_kernel(x_ref, mod_ref, g_ref, w_ref, cos_ref, sin_ref,
                 qna_ref, kna_ref, vna_ref, qd_ref, kd_ref, vd_ref, qs_ref, ks_ref, vs_ref, gs_ref, vdt_ref):
    x = x_ref[0]
    mod = mod_ref[0, 0]
    y = x * lax.rsqrt(jnp.mean(x * x, axis=-1, keepdims=True) + EPS) * g_ref[...]
    h = (y * (1.0 + mod[1:2]) + mod[0:1]).astype(BF16)

    def proj(a, b):
        return _dot(h, w_ref[:, a:b])

    cos = cos_ref[...]
    sin = sin_ref[...]
    lane = lax.broadcasted_iota(I32, cos.shape, 1)
    first_half = (lane % (2 * ROPE_PAIRS)) < ROPE_PAIRS

    def rope(t):
        outs = []
        for j in range(t.shape[1] // LANES):
            c = t[:, j * LANES:(j + 1) * LANES]
            r = jnp.where(first_half, pltpu.roll(c, LANES - ROPE_PAIRS, 1), pltpu.roll(c, ROPE_PAIRS, 1))
            outs.append(c * cos + r * sin)
        return outs[0] if len(outs) == 1 else jnp.concatenate(outs, axis=1)

    scale = HEAD_DIM ** -0.5
    qna_ref[0] = (proj(C_QNA, C_QD) * scale).astype(BF16)
    qd_ref[0] = (rope(proj(C_QD, C_QS)) * (scale * LOG2E)).astype(BF16)
    qs_ref[0] = (rope(proj(C_QS, C_KNA)) * scale).astype(BF16)
    kna_ref[0] = proj(C_KNA, C_KD).astype(BF16)
    kd_ref[0] = rope(proj(C_KD, C_KS)).astype(BF16)
    ks_ref[0] = rope(proj(C_KS, C_VNA)).astype(BF16)
    vna_ref[0] = proj(C_VNA, C_VD).astype(BF16)
    vd_f32 = proj(C_VD, C_VS)
    pad_rows = (lax.broadcasted_iota(I32, (DIFF_VT_PAD, vd_f32.shape[0]), 0) == 0).astype(BF16)
    vdt_ref[0] = jnp.concatenate(
        [blk for hd in range(DIFF_HEADS)
         for blk in (vd_f32[:, hd * DIFF_V:(hd + 1) * DIFF_V].T.astype(BF16), pad_rows)], axis=0)
    vd = vd_f32.astype(BF16)
    ones = jnp.ones((vd.shape[0], DIFF_V), BF16)
    vd_ref[0] = jnp.concatenate(
        [blk for hd in range(DIFF_HEADS) for blk in (vd[:, hd * DIFF_V:(hd + 1) * DIFF_V], ones)], axis=1)
    vs_ref[0] = proj(C_VS, C_GATE).astype(BF16)
    gates = proj(C_GATE, C_END)
    gs_ref[0] = (1.0 / (1.0 + jnp.exp(-gates))).astype(BF16)


def _matmul_rows(m, limit):
    return max(t for t in range(LANES, limit + 1, LANES) if m % t == 0)


def _project(xu, modtab, g, w_in, cos_u, sin_u):
    b, m, d = xu.shape
    tm = TILE
    widths = (256, 256, 256, 512, 512, 2 * DIFF_HEADS * DIFF_V, 256, 128, 128, C_END - C_GATE)
    row = lambda w: pl.BlockSpec((1, tm, w), lambda bb, i: (bb, i, 0))
    vt_rows = DIFF_HEADS * (DIFF_V + DIFF_VT_PAD)
    return pl.pallas_call(
        _proj_kernel,
        out_shape=[jax.ShapeDtypeStruct((b, m, w), BF16) for w in widths]
                  + [jax.ShapeDtypeStruct((b, vt_rows, m), BF16)],
        grid=(b, m // tm),
        in_specs=[row(d),
                  pl.BlockSpec((1, 1, N_MOD, d), lambda bb, i: (bb, jnp.minimum(i, 1), 0, 0)),
                  pl.BlockSpec((1, d), lambda bb, i: (0, 0)),
                  pl.BlockSpec((d, C_END), lambda bb, i: (0, 0)),
                  pl.BlockSpec((tm, LANES), lambda bb, i: (i, 0)),
                  pl.BlockSpec((tm, LANES), lambda bb, i: (i, 0))],
        out_specs=[row(w) for w in widths] + [pl.BlockSpec((1, vt_rows, tm), lambda bb, i: (bb, 0, i))],
        compiler_params=_params(2),
        name="norm_project_rope",
    )(xu, modtab, g, w_in, cos_u, sin_u)


def _tile_kind(t, nb):
    return jnp.where(t == 0, 0, jnp.where(t == 1, 1, jnp.where(t == nb, 3, 2)))


def _local_tiles(prev_ref, cur_ref, next_ref, u):
    tiles = cur_ref.shape[1] // TILE

    def tile(s):
        if s < 0:
            return prev_ref[0]
        if s >= tiles:
            return next_ref[0]
        return cur_ref[0, s * TILE:(s + 1) * TILE]

    return jnp.concatenate([tile(u - 1), tile(u), tile(u + 1)], axis=0)


def _na_kernel(q_ref, kc_ref, kp_ref, kcur_ref, kn_ref, vc_ref, vp_ref, vcur_ref, vn_ref, bias_ref, o_ref, *, nb):
    tiles = q_ref.shape[1] // TILE
    kc, vc = kc_ref[0], vc_ref[0]
    for u in range(tiles):
        rows = slice(u * TILE, (u + 1) * TILE)
        kind = _tile_kind(pl.program_id(1) * tiles + u, nb)
        q = q_ref[0, rows]
        kl = _local_tiles(kp_ref, kcur_ref, kn_ref, u)
        vl = _local_tiles(vp_ref, vcur_ref, vn_ref, u)
        outs = []
        for hd in range(NA_HEADS):
            sl = slice(hd * HEAD_DIM, (hd + 1) * HEAD_DIM)
            qh = q[:, sl]
            s_c = _nt_dot(qh, kc[:, sl])
            s_l = _nt_dot(qh, kl[:, sl]) + bias_ref[kind, hd]
            mx = jnp.maximum(jnp.max(s_c, axis=1, keepdims=True), jnp.max(s_l, axis=1, keepdims=True))
            p_c = jnp.exp(s_c - mx)
            p_l = jnp.exp(s_l - mx)
            den = jnp.sum(p_c, axis=1, keepdims=True) + jnp.sum(p_l, axis=1, keepdims=True)
            o = _dot(p_c.astype(BF16), vc[:, sl]) + _dot(p_l.astype(BF16), vl[:, sl])
            outs.append(o / den)
        o_ref[0, rows] = jnp.concatenate(outs, axis=1).astype(BF16)


def _na_bias_table(rpb):
    tr, nk = NA_TILE_ROWS, 3 * NA_TILE_ROWS
    qc = np.arange(GRID_W)[:, None]
    kc = np.arange(GRID_W)[None, :]
    cstart = np.clip(qc - NA_WIN_COLS // 2, 0, GRID_W - NA_WIN_COLS)
    col_ok = (kc >= cstart) & (kc < cstart + NA_WIN_COLS)
    dc = np.clip(kc - qc, -(NA_WIN_COLS - 1), NA_WIN_COLS - 1) + NA_WIN_COLS - 1
    onehot = jnp.asarray(np.arange(2 * NA_WIN_COLS - 1)[:, None, None] == dc[None], F32)
    cols = jnp.einsum("hrc,cqk->hrqk", rpb.astype(F32), onehot, precision=HIGHEST)
    qr = np.arange(tr)[:, None]
    krow = np.arange(nk)[None, :] - tr
    dr = np.clip(krow - qr, -(NA_WIN_ROWS - 1), NA_WIN_ROWS - 1) + NA_WIN_ROWS - 1
    starts = (0 * qr, qr - NA_WIN_ROWS // 2, 0 * qr + tr - NA_WIN_ROWS)
    row_ok = np.stack([(krow >= st) & (krow < st + NA_WIN_ROWS) for st in starts])
    t = jnp.take(cols, jnp.asarray(dr.reshape(-1)), axis=1)
    t = t.reshape(NA_HEADS, tr, nk, GRID_W, GRID_W).transpose(0, 1, 3, 2, 4)
    ok = row_ok[:, None, :, None, :, None] & col_ok[None, None, None, :, None, :]
    t = jnp.where(ok, t[None], NEG).reshape(3, NA_HEADS, TILE, 3 * TILE)
    return jnp.concatenate([jnp.full_like(t[:1], NEG), t], axis=0)


def _local_attention_specs(nt, wq, wk):
    nb = nt - 1
    tiles = max(t for t in range(1, 5 + 1) if nt % t == 0)
    rows = pl.BlockSpec((1, tiles * TILE, wq), lambda bb, i: (bb, i, 0))
    one = lambda f: pl.BlockSpec((1, TILE, wk), f)
    kv = [one(lambda bb, i: (bb, 0, 0)),
          one(lambda bb, i: (bb, jnp.clip(i * tiles - 1, 1, nb), 0)),
          pl.BlockSpec((1, tiles * TILE, wk), lambda bb, i: (bb, i, 0)),
          one(lambda bb, i: (bb, jnp.clip((i + 1) * tiles, 1, nb), 0))]
    return tiles, rows, kv


def _resident(a):
    return pl.BlockSpec(a.shape, lambda bb, i: (0,) * a.ndim, pipeline_mode=pl.Buffered(1))


def _na_attention(q, k, v, bias):
    b, m, w = q.shape
    nt = m // TILE
    tiles, rows, kv_specs = _local_attention_specs(nt, w, w)
    return pl.pallas_call(
        functools.partial(_na_kernel, nb=nt - 1),
        out_shape=jax.ShapeDtypeStruct((b, m, w), BF16),
        grid=(b, nt // tiles),
        in_specs=[rows] + kv_specs + kv_specs + [_resident(bias)],
        out_specs=rows,
        compiler_params=_params(2),
        name="neighbourhood_attention",
    )(q, k, k, k, k, v, v, v, v, bias)


def _swa_kernel(sink_ref, q_ref, kc_ref, kp_ref, kcur_ref, kn_ref, vc_ref, vp_ref, vcur_ref, vn_ref, mask_ref,
                o_ref, *, nb):
    tiles = q_ref.shape[1] // TILE
    kc, vc = kc_ref[0], vc_ref[0]
    group = SWA_HEADS // SWA_KV_HEADS
    rows = group * TILE
    rid = lax.broadcasted_iota(I32, (rows, 1), 0)
    for u in range(tiles):
        kind = _tile_kind(pl.program_id(1) * tiles + u, nb)
        q = q_ref[0, u * TILE:(u + 1) * TILE]
        kl = _local_tiles(kp_ref, kcur_ref, kn_ref, u)
        vl = _local_tiles(vp_ref, vcur_ref, vn_ref, u)
        mask = jnp.concatenate([mask_ref[kind]] * group, axis=0)
        outs = []
        for g in range(SWA_KV_HEADS):
            ksl = slice(g * HEAD_DIM, (g + 1) * HEAD_DIM)
            qg = jnp.concatenate([q[:, (g * group + j) * HEAD_DIM:(g * group + j + 1) * HEAD_DIM]
                                  for j in range(group)], axis=0)
            sink = jnp.zeros((rows, 1), F32)
            for j in range(group):
                sink = jnp.where(rid // TILE == j, sink_ref[g * group + j], sink)
            s_c = _nt_dot(qg, kc[:, ksl])
            s_l = _nt_dot(qg, kl[:, ksl]) + mask
            mx = jnp.maximum(jnp.maximum(jnp.max(s_c, axis=1, keepdims=True), jnp.max(s_l, axis=1, keepdims=True)),
                             sink)
            p_c = jnp.exp(s_c - mx)
            p_l = jnp.exp(s_l - mx)
            den = jnp.sum(p_c, axis=1, keepdims=True) + jnp.sum(p_l, axis=1, keepdims=True) + jnp.exp(sink - mx)
            o = (_dot(p_c.astype(BF16), vc[:, ksl]) + _dot(p_l.astype(BF16), vl[:, ksl])) / den
            outs.extend(o[j * TILE:(j + 1) * TILE] for j in range(group))
        o_ref[0, u * TILE:(u + 1) * TILE] = jnp.concatenate(outs, axis=1).astype(BF16)


def _swa_mask_table():
    qpos = np.arange(TILE)[:, None]
    kpos = np.arange(3 * TILE)[None, :] - TILE
    near = np.abs(qpos - kpos) <= SWA_WINDOW
    kinds = [np.zeros_like(near), near & (kpos >= 0), near, near & (kpos < TILE)]
    return jnp.asarray(np.where(np.stack(kinds), 0.0, NEG), F32)


def _swa_attention(q, k, v, sink):
    b, m, wq = q.shape
    nt = m // TILE
    tiles, rows, kv_specs = _local_attention_specs(nt, wq, k.shape[2])
    mask = _swa_mask_table()
    return pl.pallas_call(
        functools.partial(_swa_kernel, nb=nt - 1),
        out_shape=jax.ShapeDtypeStruct((b, m, wq), BF16),
        grid=(b, nt // tiles),
        in_specs=[pl.BlockSpec(memory_space=pltpu.SMEM), rows] + kv_specs + kv_specs + [_resident(mask)],
        out_specs=rows,
        compiler_params=_params(2),
        name="windowed_attention",
    )(sink, q, k, k, k, k, v, v, v, v, mask)


def _diff_kernel(lam_ref, q_ref, k_ref, v_ref, vt_ref, g_ref, gt_ref, o_ref, acc_ref, m_ref, s_ref, cmax_ref,
                 *, n_chunks, kblk, fast_kblk, lam_init):
    i = pl.program_id(2)
    parts = q_ref.shape[1] // TILE

    def stacked(part):
        q = q_ref[0, part * TILE:(part + 1) * TILE]
        lane = lax.broadcasted_iota(I32, q.shape, 1)
        zero = jnp.zeros_like(q)
        return jnp.concatenate([jnp.where(lane < HEAD_DIM, q, zero), jnp.where(lane >= HEAD_DIM, q, zero)], axis=0)

    def chunk(c):
        return pl.ds(pl.multiple_of(c * kblk, LANES), kblk)

    def accumulate(s, cmax, v):
        m_prev = m_ref[...]
        m_new = jnp.maximum(m_prev, cmax)
        p = jnp.exp2(s - m_new).astype(BF16)
        acc_ref[...] = jnp.exp2(m_prev - m_new) * acc_ref[...] + _dot(p, v)
        m_ref[...] = m_new

    def reset():
        m_ref[...] = jnp.full(m_ref.shape, NEG, F32)
        acc_ref[...] = jnp.zeros(acc_ref.shape, F32)

    def safe_sweep(qq):
        def scores(c, slot):
            s = _nt_dot(qq, k_ref[0, chunk(c), :])
            s_ref[slot] = s
            cmax_ref[slot] = jnp.max(s, axis=1, keepdims=True)

        reset()
        scores(0, 0)

        def pair(c, last):
            for cur in (0, 1):
                if not (last and cur == 1):
                    scores(c + cur + 1, 1 - cur)
                accumulate(s_ref[cur], cmax_ref[cur], v_ref[0, chunk(c + cur), :])

        def body(c2, carry):
            pair(2 * c2, False)
            return carry

        lax.fori_loop(0, n_chunks // 2 - 1, body, 0)
        pair(n_chunks - 2, True)
        return acc_ref[...]

    def fast_sweep(qq):
        bounds = [0, CTX] + list(range(CTX + fast_kblk, n_chunks * kblk, fast_kblk)) + [n_chunks * kblk]
        keys = lambda c: slice(bounds[c], bounds[c + 1])
        s = _nt_dot(k_ref[0, keys(0), :], qq)
        m0 = jnp.max(s, axis=0, keepdims=True)
        acc = _dot(vt_ref[0, :, keys(0)], jnp.exp2(s - m0).astype(BF16))
        for c in range(1, len(bounds) - 1):
            s = _nt_dot(k_ref[0, keys(c), :], qq)
            acc = acc + _dot(vt_ref[0, :, keys(c)], jnp.exp2(s - m0).astype(BF16))
        return acc

    def finish_transposed(acc):
        o = acc[:DIFF_V] / acc[DIFF_V:DIFF_V + 1]
        d = o[:, :TILE] - lam_ref[0] * o[:, TILE:]
        y = d * lax.rsqrt(jnp.mean(d * d, axis=0, keepdims=True) + EPS) * gt_ref[...]
        return (y * (1.0 - lam_init)).T.astype(BF16)

    def finish(acc):
        o = acc[:, :DIFF_V] / acc[:, DIFF_V:]
        d = o[:TILE] - lam_ref[0] * o[TILE:]
        y = d * lax.rsqrt(jnp.mean(d * d, axis=-1, keepdims=True) + EPS) * g_ref[...]
        return (y * (1.0 - lam_init)).astype(BF16)

    def rows(part):
        return pl.ds(part * TILE, TILE)

    @pl.when(i == 0)
    def _():
        o_ref[0, 0:(parts - 1) * TILE] = jnp.zeros(((parts - 1) * TILE, DIFF_V), BF16)
        reset()
        s = _nt_dot(stacked(parts - 1), k_ref[0, 0:CTX, :])
        accumulate(s, jnp.max(s, axis=1, keepdims=True), v_ref[0, 0:CTX, :])
        o_ref[0, rows(parts - 1)] = finish(acc_ref[...])

    @pl.when(i > 0)
    def _():
        accs = [fast_sweep(stacked(part)) for part in range(parts)]
        for part in range(parts):
            o_ref[0, rows(part)] = finish_transposed(accs[part])
        bad = [jnp.logical_not(jnp.max(acc[DIFF_V:DIFF_V + 1]) < DIFF_MAX_DENOMINATOR) for acc in accs]
        for part in range(parts):
            @pl.when(bad[part])
            def _(part=part):
                o_ref[0, rows(part)] = finish(safe_sweep(stacked(part)))


def _diff_chunking(m, max_chunk=2048):
    for n_chunks in range(2, m // LANES + 1, 2):
        if m % (n_chunks * LANES) == 0 and m // n_chunks <= max_chunk:
            return n_chunks, m // n_chunks
    raise ValueError(f"no chunking for {m} keys")


def _diff_attention(q, k, v, vt, lam, g, lam_init):
    b, m, _ = q.shape
    vt_rows = DIFF_V + DIFF_VT_PAD
    n_chunks, kblk = _diff_chunking(m)
    fast_kblk = DIFF_FAST_CHUNK
    parts = DIFF_TILES_PER_STEP
    pad = (parts - 1) * TILE
    assert (m + pad) % (parts * TILE) == 0
    q_pad = jnp.pad(q, ((0, 0), (pad, 0), (0, 0)))
    out = pl.pallas_call(
        functools.partial(_diff_kernel, n_chunks=n_chunks, kblk=kblk, fast_kblk=fast_kblk, lam_init=lam_init),
        out_shape=jax.ShapeDtypeStruct((b, m + pad, DIFF_HEADS * DIFF_V), BF16),
        grid=(b, DIFF_HEADS, (m + pad) // (parts * TILE)),
        in_specs=[pl.BlockSpec(memory_space=pltpu.SMEM),
                  pl.BlockSpec((1, parts * TILE, 2 * HEAD_DIM), lambda bb, hh, i: (bb, i, hh)),
                  pl.BlockSpec((1, m, 2 * HEAD_DIM), lambda bb, hh, i: (bb, 0, hh)),
                  pl.BlockSpec((1, m, 2 * DIFF_V), lambda bb, hh, i: (bb, 0, hh), pipeline_mode=pl.Buffered(1)),
                  pl.BlockSpec((1, vt_rows, m), lambda bb, hh, i: (bb, hh, 0)),
                  pl.BlockSpec((1, DIFF_V), lambda bb, hh, i: (0, 0)),
                  pl.BlockSpec((DIFF_V, 1), lambda bb, hh, i: (0, 0))],
        out_specs=pl.BlockSpec((1, parts * TILE, DIFF_V), lambda bb, hh, i: (bb, i, hh)),
        scratch_shapes=[pltpu.VMEM((2 * TILE, 2 * DIFF_V), F32), pltpu.VMEM((2 * TILE, 1), F32),
                        pltpu.VMEM((2, 2 * TILE, kblk), F32), pltpu.VMEM((2, 2 * TILE, 1), F32)],
        compiler_params=_params(3),
        name="differential_attention",
    )(lam, q_pad, k, v, vt, g, g.reshape(DIFF_V, 1))
    return out[:, pad:]


def _merge_kernel(ona_ref, od_ref, osw_ref, gs_ref, x_ref, mod_ref, wna_ref, wd_ref, wsw_ref, wo_ref,
                  g2_ref, wrt_ref, xo_ref, h2_ref, afft_ref):
    tm, d = x_ref.shape[1:]
    h2_parts = []
    for rows in _independent_parts(tm):
        gs = gs_ref[0, rows]
        y = (gs[:, :d].astype(F32) * _dot(ona_ref[0, rows], wna_ref[...])
             + gs[:, d:2 * d].astype(F32) * _dot(od_ref[0, rows], wd_ref[...])
             + gs[:, 2 * d:].astype(F32) * _dot(osw_ref[0, rows], wsw_ref[...]))
        mod = _row_modulation(mod_ref, pl.program_id(1) * tm + rows.start, rows.stop - rows.start)
        xn = x_ref[0, rows] + mod(2) * _dot(y.astype(BF16), wo_ref[...])
        xo_ref[0, rows] = xn
        r = xn * lax.rsqrt(jnp.mean(xn * xn, axis=-1, keepdims=True) + EPS) * g2_ref[...]
        h2 = r * (1.0 + mod(4)) + mod(3)
        h2_ref[0, rows] = h2.astype(BF16)
        h2_parts.append(h2)
    h2 = jnp.concatenate(h2_parts, axis=0)
    lt = lax.dot_general(wrt_ref[...], h2, (((1,), (1,)), ((), ())),
                         preferred_element_type=F32, precision=HIGHEST)
    et = jnp.exp(lt - jnp.max(lt, axis=0, keepdims=True))
    afft_ref[0] = et / jnp.sum(et, axis=0, keepdims=True)


def _merge(o_na, o_d, o_sw, gs, xu, modtab, w_na, w_d, w_sw, w_o, g2, w_r):
    b, m, d = xu.shape
    tm = _matmul_rows(m, 640)
    ne = w_r.shape[1]
    row = lambda w: pl.BlockSpec((1, tm, w), lambda bb, i: (bb, i, 0))
    full = lambda a: pl.BlockSpec(a.shape, lambda bb, i: (0,) * a.ndim)
    w_rt = w_r.T
    return pl.pallas_call(
        _merge_kernel,
        out_shape=[jax.ShapeDtypeStruct((b, m, d), F32), jax.ShapeDtypeStruct((b, m, d), BF16),
                   jax.ShapeDtypeStruct((b, ne, m), F32)],
        grid=(b, m // tm),
        in_specs=[row(o_na.shape[2]), row(o_d.shape[2]), row(o_sw.shape[2]), row(gs.shape[2]), row(d),
                  pl.BlockSpec((1, 2, N_MOD, d), lambda bb, i: (bb, 0, 0, 0)),
                  full(w_na), full(w_d), full(w_sw), full(w_o), full(g2), full(w_rt)],
        out_specs=[row(d), row(d), pl.BlockSpec((1, ne, tm), lambda bb, i: (bb, 0, i))],
        compiler_params=_params(2),
        name="merge_residual_router",
    )(o_na, o_d, o_sw, gs, xu, modtab, w_na, w_d, w_sw, w_o, g2, w_rt)


def _route_kernel(a_ref, gpos_ref, gfull_ref, *, n_latent_sets, cap_lat, cap_ctx):
    ne, c, w = a_ref.shape[1:]
    a = a_ref[0]
    bits = lax.bitcast_convert_type(a, I32)
    cap = jnp.where(pl.program_id(0) < n_latent_sets, cap_lat, cap_ctx).astype(F32)

    def count(mask):
        return jnp.sum(jnp.sum(mask.astype(F32), axis=2, keepdims=True), axis=1, keepdims=True)

    def search(it, thr):
        cand = thr | jnp.left_shift(jnp.int32(1), 30 - it)
        return jnp.where(count(bits >= cand) >= cap, cand, thr)

    thr = lax.fori_loop(0, 31, search, jnp.zeros((ne, 1, 1), I32))
    gt = bits > thr
    eq = bits == thr
    need = cap - count(gt)

    upper = (lax.broadcasted_iota(I32, (w, w), 0) <= lax.broadcasted_iota(I32, (w, w), 1)).astype(BF16)
    lower = (lax.broadcasted_iota(I32, (c, c), 1) < lax.broadcasted_iota(I32, (c, c), 0)).astype(BF16)

    def exclusive_cumsum(mask):
        x = mask.astype(F32).reshape(ne * c, w)
        within = _dot(x.astype(BF16), upper)
        tot = jnp.broadcast_to(within[:, w - 1:w], (ne * c, w)).astype(BF16)
        before = jnp.concatenate([_dot(lower, tot[e * c:(e + 1) * c]) for e in range(ne)], axis=0)
        return (before + within - x).reshape(ne, c, w)

    sel = gt | (eq & (exclusive_cumsum(eq) < need))
    g = exclusive_cumsum(sel).astype(I32)
    gfull_ref[0] = g
    gpos_ref[0] = jnp.where(sel, g, -1)


def _route(aff_sets, n_latent_sets, cap_lat, cap_ctx):
    s, ne, n = aff_sets.shape
    c = n // LANES
    a4 = aff_sets.reshape(s, ne, c, LANES)
    blk = pl.BlockSpec((1, ne, c, LANES), lambda i: (i, 0, 0, 0))
    gpos, gfull = pl.pallas_call(
        functools.partial(_route_kernel, n_latent_sets=n_latent_sets, cap_lat=cap_lat, cap_ctx=cap_ctx),
        out_shape=[jax.ShapeDtypeStruct(a4.shape, I32)] * 2,
        grid=(s,),
        in_specs=[blk],
        out_specs=[blk, blk],
        compiler_params=_params(1),
        name="expert_choice_select",
    )(a4)
    return gpos.reshape(s, ne, n), gfull.reshape(s, ne, n)


def _window(cap):
    w = min(TILE + BF16_ROWS, cap)
    main = min(WINDOW_MAIN_ROWS, w)
    return main, w - main, cap - w


def _windows(gfull, cap):
    b, ne, _ = gfull.shape
    main, _, max_start = _window(cap)
    seg_start = gfull[:, :, ::TILE]
    seg_end = jnp.concatenate([gfull[:, :, TILE::TILE], jnp.full((b, ne, 1), cap, I32)], axis=2)
    astart = jnp.minimum(seg_start // BF16_ROWS * BF16_ROWS, max_start)
    return astart.reshape(-1), (seg_end > astart + main).astype(I32).reshape(-1)


def _gather_kernel(a_ref, t_ref, g_ref, aff_ref, h_ref, o_ref, w_ref, *, ne, nblk, eg, main, tail):
    b, egi, j = pl.program_id(0), pl.program_id(1), pl.program_id(2)

    @pl.when(j == 0)
    def _():
        o_ref[...] = jnp.zeros(o_ref.shape, o_ref.dtype)
        w_ref[...] = jnp.zeros(w_ref.shape, w_ref.dtype)

    hb = h_ref[0]
    t = hb.shape[0]

    def idx(k):
        return (b * ne + egi * eg + k) * nblk + j

    def place(experts, first, rows):
        hits = [lax.broadcasted_iota(I32, (rows, t), 0) == g_ref[0, k] - (a_ref[idx(k)] + first) for k in experts]
        onehot = jnp.concatenate([jnp.where(h, 1.0, 0.0).astype(BF16) for h in hits], axis=0)
        picked = _dot(onehot, hb).astype(BF16)
        for n, (k, hit) in enumerate(zip(experts, hits)):
            weight = jnp.sum(jnp.where(hit, aff_ref[0, k], 0.0), axis=1, keepdims=True)
            sl = (0, k, pl.ds(pl.multiple_of(a_ref[idx(k)] + first, BF16_ROWS), rows), slice(None))
            o_ref[sl] = o_ref[sl] + picked[n * rows:(n + 1) * rows]
            w_ref[sl] = w_ref[sl] + weight

    place(list(range(eg)), 0, main)
    for k in range(eg if tail else 0):
        @pl.when(t_ref[idx(k)] != 0)
        def _(k=k):
            place([k], main, tail)


def _gather(h2u, afft, gpos, astart, need_tail, cap, tile_off, eg):
    b, ne, n = gpos.shape
    m, d = h2u.shape[1:]
    nblk = n // TILE
    main, tail, _ = _window(cap)
    lists = lambda w: pl.BlockSpec((1, eg, cap, w), lambda bb, e, j, a, nt: (bb, e, 0, 0),
                                   pipeline_mode=pl.Buffered(1))
    return pl.pallas_call(
        functools.partial(_gather_kernel, ne=ne, nblk=nblk, eg=eg, main=main, tail=tail),
        out_shape=[jax.ShapeDtypeStruct((b, ne, cap, d), BF16), jax.ShapeDtypeStruct((b, ne, cap, 1), F32)],
        grid_spec=pltpu.PrefetchScalarGridSpec(
            num_scalar_prefetch=2,
            grid=(b, ne // eg, nblk),
            in_specs=[pl.BlockSpec((1, eg, 1, TILE), lambda bb, e, j, a, nt: (bb, e, 0, j)),
                      pl.BlockSpec((1, eg, 1, TILE), lambda bb, e, j, a, nt: (bb, e, 0, tile_off + j)),
                      pl.BlockSpec((1, TILE, d), lambda bb, e, j, a, nt: (bb, tile_off + j, 0))],
            out_specs=[lists(d), lists(1)]),
        compiler_params=_params(3),
        name="expert_gather",
    )(astart, need_tail, gpos.reshape(b, ne, 1, n), afft.reshape(b, ne, 1, m), h2u)


def _ffn_kernel(x_ref, rw_ref, wg_ref, wu_ref, wd_ref, o_ref, wg_s, wu_s, wd_s):
    @pl.when((pl.program_id(1) == 0) & (pl.program_id(2) == 0))
    def _():
        wg_s[...] = wg_ref[0, 0].astype(BF16)
        wu_s[...] = wu_ref[0, 0].astype(BF16)
        wd_s[...] = wd_ref[0, 0].astype(BF16)

    x = x_ref[0, 0]
    gate = _dot(x, wg_s[...])
    up = _dot(x, wu_s[...])
    hid = (gate / (1.0 + jnp.exp(-gate)) * up).astype(BF16)
    o_ref[0, 0] = (_dot(hid, wd_s[...]) * rw_ref[0, 0]).astype(BF16)


def _expert_ffn(xs, ws, w_gate, w_up, w_down, layer):
    b, ne, cap, d = xs.shape
    tr = min(cap, 1024)
    wspec = lambda w: pl.BlockSpec((1, 1) + w.shape[2:], lambda e, bb, r: (layer, e, 0, 0))
    rows = lambda w: pl.BlockSpec((1, 1, tr, w), lambda e, bb, r: (bb, e, r, 0))
    return pl.pallas_call(
        _ffn_kernel,
        out_shape=jax.ShapeDtypeStruct(xs.shape, BF16),
        grid=(ne, b, cap // tr),
        in_specs=[rows(d), rows(1), wspec(w_gate), wspec(w_up), wspec(w_down)],
        out_specs=rows(d),
        scratch_shapes=[pltpu.VMEM(w.shape[2:], BF16) for w in (w_gate, w_up, w_down)],
        compiler_params=_params(3),
        name="expert_swiglu",
    )(xs, ws, w_gate, w_up, w_down)


def _combine_kernel(a_ref, t_ref, x_ref, g_ref, mod_ref, fg_ref, ye_ref, o_ref,
                    win_ref, tail_ref, tacc_ref, sem_ref, tsem_ref, *, ne, nblk, main, tail, final):
    b, j = pl.program_id(0), pl.program_id(1)
    step = b * nblk + j
    slot = step % 2

    def start_at(bb, jj, e):
        return pl.multiple_of(a_ref[(bb * ne + e) * nblk + jj], BF16_ROWS)

    def start_of(e):
        return start_at(b, j, e)

    def window_copy(bb, jj, e, sl):
        return pltpu.make_async_copy(ye_ref.at[bb, e, pl.ds(start_at(bb, jj, e), main)],
                                     win_ref.at[sl, e * main:(e + 1) * main], sem_ref.at[sl, e])

    @pl.when(step == 0)
    def _():
        for e in range(ne):
            window_copy(b, j, e, slot).start()

    @pl.when(step + 1 < pl.num_programs(0) * nblk)
    def _():
        nxt = step + 1
        for e in range(ne):
            window_copy(nxt // nblk, nxt % nblk, e, 1 - slot).start()

    t = x_ref.shape[1]

    def pick(experts, first, rows, windows):
        onehot = jnp.concatenate(
            [jnp.where(lax.broadcasted_iota(I32, (rows, t), 0) == g_ref[0, e] - (start_of(e) + first), 1.0, 0.0)
             .astype(BF16) for e in experts], axis=0)
        return lax.dot_general(onehot, windows, (((0,), (0,)), ((), ())), preferred_element_type=F32)

    if tail:
        tacc_ref[...] = jnp.zeros(tacc_ref.shape, F32)
        for e in range(ne):
            @pl.when(t_ref[(b * ne + e) * nblk + j] != 0)
            def _(e=e):
                cp = pltpu.make_async_copy(ye_ref.at[b, e, pl.ds(start_of(e) + main, tail)], tail_ref, tsem_ref.at[0])
                cp.start()
                cp.wait()
                tacc_ref[...] += pick([e], main, tail, tail_ref[...])

    for e in range(ne):
        window_copy(b, j, e, slot).wait()
    acc = pick(list(range(ne)), 0, main, win_ref[slot])
    if tail:
        acc = acc + tacc_ref[...]
    x = x_ref[0] + mod_ref[0, 0][5:6] * acc
    if final:
        x = x * lax.rsqrt(jnp.mean(x * x, axis=-1, keepdims=True) + EPS) * fg_ref[...]
    o_ref[0] = x


def _combine(xu, gpos, astart, need_tail, ye, modtab, final_g, tile_off, kind, final):
    b, m, d = xu.shape
    _, ne, n = gpos.shape
    cap = ye.shape[2]
    nblk = n // TILE
    main, tail, _ = _window(cap)
    out_spec = pl.BlockSpec((1, TILE, d), (lambda bb, j, a, nt: (bb, j, 0)) if final
                            else (lambda bb, j, a, nt: (bb, tile_off + j, 0)))
    return pl.pallas_call(
        functools.partial(_combine_kernel, ne=ne, nblk=nblk, main=main, tail=tail, final=final),
        out_shape=jax.ShapeDtypeStruct((b, n, d) if final else xu.shape, F32),
        grid_spec=pltpu.PrefetchScalarGridSpec(
            num_scalar_prefetch=2,
            grid=(b, nblk),
            in_specs=[pl.BlockSpec((1, TILE, d), lambda bb, j, a, nt: (bb, tile_off + j, 0)),
                      pl.BlockSpec((1, ne, 1, TILE), lambda bb, j, a, nt: (bb, 0, 0, j)),
                      pl.BlockSpec((1, 1, N_MOD, d), lambda bb, j, a, nt: (bb, kind, 0, 0)),
                      pl.BlockSpec((1, d), lambda bb, j, a, nt: (0, 0)),
                      pl.BlockSpec(memory_space=pl.ANY)],
            out_specs=out_spec,
            scratch_shapes=[pltpu.VMEM((2, ne * main, d), BF16), pltpu.VMEM((max(tail, BF16_ROWS), d), BF16),
                            pltpu.VMEM((TILE, d), F32), pltpu.SemaphoreType.DMA((2, ne)),
                            pltpu.SemaphoreType.DMA((1,))]),
        input_output_aliases={} if final else {2: 0},
        compiler_params=_params(2),
        name="expert_combine",
    )(astart, need_tail, xu, gpos.reshape(b, ne, 1, n), modtab, final_g, ye)


def _rope_tables(n):
    t = np.arange(n)
    row = (t // GRID_W).astype(np.float32)[:, None]
    col = (t % GRID_W).astype(np.float32)[:, None]
    inv = (ROPE_BASE ** (-np.arange(ROPE_PAIRS, dtype=np.float32) / ROPE_PAIRS)).astype(np.float32)
    ang = np.concatenate([row * inv, row * inv, col * inv, col * inv], axis=-1)
    cos, sin = np.cos(ang), np.sin(ang)
    half = (np.arange(HEAD_DIM) % (2 * ROPE_PAIRS)) < ROPE_PAIRS
    sin = np.where(half[None], -sin, sin)
    cos = np.concatenate([np.ones((CTX, HEAD_DIM)), cos], axis=0)
    sin = np.concatenate([np.zeros((CTX, HEAD_DIM)), sin], axis=0)
    tile2 = lambda a: jnp.asarray(np.concatenate([a, a], axis=1), F32)
    return tile2(cos), tile2(sin)


def _moe(xu, h2u, afft, modtab, weights, layer, final_g, final):
    b, m, d = xu.shape
    n = m - CTX
    w_gate, w_up, w_down = weights
    cap_lat = max(1, EC_CAPACITY * n // N_EXPERTS)
    cap_ctx = max(1, EC_CAPACITY * CTX // N_EXPERTS)
    sets = [afft[:, :, CTX:]]
    if not final:
        sets.append(jnp.concatenate([afft[:, :, :CTX], jnp.full((b, N_EXPERTS, n - CTX), -1.0, F32)], axis=2))
    gpos, gfull = _route(jnp.concatenate(sets, axis=0), b, cap_lat, cap_ctx)

    def run(idx, n_set, cap, tile_off, kind, x_in, fin):
        gp = gpos[idx * b:(idx + 1) * b, :, :n_set]
        gf = gfull[idx * b:(idx + 1) * b, :, :n_set]
        astart, need_tail = _windows(gf, cap)
        xs, ws = _gather(h2u, afft, gp, astart, need_tail, cap, tile_off, eg=8)
        ye = _expert_ffn(xs, ws, w_gate, w_up, w_down, layer)
        return _combine(x_in, gp, astart, need_tail, ye, modtab, final_g, tile_off, kind, fin)

    if final:
        return run(0, n, cap_lat, 1, 1, xu, True)
    xu = run(0, n, cap_lat, 1, 1, xu, False)
    return run(1, CTX, cap_ctx, 0, 0, xu, False)


def kernel(x, c, ctx, c_ctx, w_mod, b_mod, norm1_g, w_in, na_rpb, diff_lambda, diff_subln_g, swa_sink,
           w_branch_na, w_branch_diff, w_branch_swa, w_out, norm2_g, w_router, w_expert_gate, w_expert_up,
           w_expert_down, final_g):
    b, n, d = x.shape
    depth = w_mod.shape[0]
    assert ctx.shape[1] == CTX and d == D_MODEL and n % (2 * TILE) == 0 and n // TILE >= 3

    cin = jnp.concatenate([c, c_ctx[None], jnp.zeros((8 - b - 1, d), F32)], axis=0)
    mod_all = _modulation(cin, w_mod, b_mod)
    cos_u, sin_u = _rope_tables(n)
    xu = jnp.concatenate([ctx, x], axis=1)
    final_g2 = final_g.reshape(1, d)

    out = None
    for l in range(depth):
        final = l == depth - 1
        lam_init = 0.8 - 0.6 * math.exp(-0.3 * l)
        mod_l = mod_all[l].reshape(8, N_MOD, d)
        modtab = jnp.stack([jnp.broadcast_to(mod_l[b], (b, N_MOD, d)), mod_l[:b]], axis=1)
        lam_p = diff_lambda[l].astype(F32)
        lam = (jnp.exp(jnp.sum(lam_p[0] * lam_p[1])) - jnp.exp(jnp.sum(lam_p[2] * lam_p[3])) + lam_init).reshape(1)

        (q_na, k_na, v_na, q_d, k_d, v_d, q_s, k_s, v_s, gs, v_dt) = _project(
            xu, modtab, norm1_g[l].reshape(1, d), w_in[l].astype(BF16), cos_u, sin_u)
        o_na = _na_attention(q_na, k_na, v_na, _na_bias_table(na_rpb[l]))
        o_d = _diff_attention(q_d, k_d, v_d, v_dt, lam, diff_subln_g[l].reshape(1, DIFF_V), lam_init)
        o_s = _swa_attention(q_s, k_s, v_s, swa_sink[l].astype(F32))
        xu, h2u, afft = _merge(
            o_na, o_d, o_s, gs, xu, modtab, w_branch_na[l].astype(BF16), w_branch_diff[l].astype(BF16),
            w_branch_swa[l].astype(BF16), w_out[l].astype(BF16), norm2_g[l].reshape(1, d), w_router[l])
        weights = (w_expert_gate, w_expert_up, w_expert_down)
        res = _moe(xu, h2u, afft, modtab, weights, l, final_g2, final)
        if final:
            out = res
        else:
            xu = res
    return out
```

```python
import functools
import math

import numpy as np
import jax
import jax.numpy as jnp
from jax import lax
from jax.experimental import pallas as pl
from jax.experimental.pallas import tpu as pltpu

F32 = jnp.float32
BF16 = jnp.bfloat16
I32 = jnp.int32
HIGHEST = lax.Precision.HIGHEST

D_MODEL = 1024
CTX = 256
TILE = 256
GRID_W = 64
HEAD_DIM = 64
ROPE_PAIRS = HEAD_DIM // 4
ROPE_BASE = 10000.0
EPS = 1e-6
N_MOD = 6
NA_HEADS = 4
NA_WIN_ROWS = 8
NA_WIN_COLS = 16
NA_TILE_ROWS = TILE // GRID_W
DIFF_HEADS = 4
DIFF_V = 2 * HEAD_DIM
SWA_HEADS = 4
SWA_KV_HEADS = 2
SWA_WINDOW = 128
N_EXPERTS = 16
EC_CAPACITY = 2
NEG = -1e30
LANES = 128
MXU_TILE = 256
LOG2E = math.log2(math.e)
DIFF_VT_PAD = 16
DIFF_FAST_CHUNK = 4 * MXU_TILE
DIFF_TILES_PER_STEP = 2
DIFF_MAX_DENOMINATOR = 2.0 ** 64
BF16_ROWS = 16
WINDOW_MAIN_ROWS = 64

C_QNA, C_QD, C_QS = 0, 256, 768
C_KNA, C_KD, C_KS = 1024, 1280, 1792
C_VNA, C_VD, C_VS = 1920, 2176, 2688
C_GATE, C_END = 2816, 5888

VMEM_LIMIT = 56 * 1024 * 1024


def _params(n_axes, vmem=VMEM_LIMIT):
    return pltpu.CompilerParams(dimension_semantics=("arbitrary",) * n_axes, vmem_limit_bytes=vmem)


def _nt_dot(a, b):
    return lax.dot_general(a, b, (((1,), (1,)), ((), ())), preferred_element_type=F32)


def _dot(a, b):
    return jnp.dot(a, b, preferred_element_type=F32)


def _mod_kernel(c_ref, w_ref, b_ref, o_ref):
    c = c_ref[...]
    s = c / (1.0 + jnp.exp(-c))
    o_ref[0] = jnp.dot(s, w_ref[0], preferred_element_type=F32, precision=HIGHEST) + b_ref[0]


def _modulation(cin, w_mod, b_mod):
    depth, d, w = w_mod.shape
    tn = 1024
    return pl.pallas_call(
        _mod_kernel,
        out_shape=jax.ShapeDtypeStruct((depth, 8, w), F32),
        grid=(depth, w // tn),
        in_specs=[pl.BlockSpec((8, d), lambda l, j: (0, 0)),
                  pl.BlockSpec((1, d, tn), lambda l, j: (l, 0, j)),
                  pl.BlockSpec((1, 1, tn), lambda l, j: (l, 0, j))],
        out_specs=pl.BlockSpec((1, 8, tn), lambda l, j: (l, 0, j)),
        compiler_params=_params(2),
        name="modulation",
    )(cin, w_mod, b_mod.reshape(depth, 1, w))


def _row_modulation(mod_ref, first, rows):
    is_ctx = first + lax.broadcasted_iota(I32, (rows, 1), 0) < CTX
    return lambda k: jnp.where(is_ctx, mod_ref[0, 0, k:k + 1], mod_ref[0, 1, k:k + 1])


def _independent_parts(rows, parts=2):
    parts = parts if rows % (parts * BF16_ROWS) == 0 else 1
    return [slice(p * (rows // parts), (p + 1) * (rows // parts)) for p in range(parts)]


def _proj_kernel(x_ref, mod_ref, g_ref, w_ref, cos_ref, sin_ref,
                 qna_ref, kna_ref, vna_ref, qd_ref, kd_ref, vd_ref, qs_ref, ks_ref, vs_ref, gs_ref, vdt_ref):
    x = x_ref[0]
    mod = mod_ref[0, 0]
    y = x * lax.rsqrt(jnp.mean(x * x, axis=-1, keepdims=True) + EPS) * g_ref[...]
    h = (y * (1.0 + mod[1:2]) + mod[0:1]).astype(BF16)

    def proj(a, b):
        return _dot(h, w_ref[:, a:b])

    cos = cos_ref[...]
    sin = sin_ref[...]
    lane = lax.broadcasted_iota(I32, cos.shape, 1)
    first_half = (lane % (2 * ROPE_PAIRS)) < ROPE_PAIRS

    def rope(t):
        outs = []
        for j in range(t.shape[1] // LANES):
            c = t[:, j * LANES:(j + 1) * LANES]
            r = jnp.where(first_half, pltpu.roll(c, LANES - ROPE_PAIRS, 1), pltpu.roll(c, ROPE_PAIRS, 1))
            outs.append(c * cos + r * sin)
        return outs[0] if len(outs) == 1 else jnp.concatenate(outs, axis=1)

    scale = HEAD_DIM ** -0.5
    qna_ref[0] = (proj(C_QNA, C_QD) * scale).astype(BF16)
    qd_ref[0] = (rope(proj(C_QD, C_QS)) * (scale * LOG2E)).astype(BF16)
    qs_ref[0] = (rope(proj(C_QS, C_KNA)) * scale).astype(BF16)
    kna_ref[0] = proj(C_KNA, C_KD).astype(BF16)
    kd_ref[0] = rope(proj(C_KD, C_KS)).astype(BF16)
    ks_ref[0] = rope(proj(C_KS, C_VNA)).astype(BF16)
    vna_ref[0] = proj(C_VNA, C_VD).astype(BF16)
    vd_f32 = proj(C_VD, C_VS)
    pad_rows = (lax.broadcasted_iota(I32, (DIFF_VT_PAD, vd_f32.shape[0]), 0) == 0).astype(BF16)
    vdt_ref[0] = jnp.concatenate(
        [blk for hd in range(DIFF_HEADS)
         for blk in (vd_f32[:, hd * DIFF_V:(hd + 1) * DIFF_V].T.astype(BF16), pad_rows)], axis=0)
    vd = vd_f32.astype(BF16)
    ones = jnp.ones((vd.shape[0], DIFF_V), BF16)
    vd_ref[0] = jnp.concatenate(
        [blk for hd in range(DIFF_HEADS) for blk in (vd[:, hd * DIFF_V:(hd + 1) * DIFF_V], ones)], axis=1)
    vs_ref[0] = proj(C_VS, C_GATE).astype(BF16)
    gates = proj(C_GATE, C_END)
    gs_ref[0] = (1.0 / (1.0 + jnp.exp(-gates))).astype(BF16)


def _matmul_rows(m, limit):
    return max(t for t in range(LANES, limit + 1, LANES) if m % t == 0)


def _project(xu, modtab, g, w_in, cos_u, sin_u):
    b, m, d = xu.shape
    tm = TILE
    widths = (256, 256, 256, 512, 512, 2 * DIFF_HEADS * DIFF_V, 256, 128, 128, C_END - C_GATE)
    row = lambda w: pl.BlockSpec((1, tm, w), lambda bb, i: (bb, i, 0))
    vt_rows = DIFF_HEADS * (DIFF_V + DIFF_VT_PAD)
    return pl.pallas_call(
        _proj_kernel,
        out_shape=[jax.ShapeDtypeStruct((b, m, w), BF16) for w in widths]
                  + [jax.ShapeDtypeStruct((b, vt_rows, m), BF16)],
        grid=(b, m // tm),
        in_specs=[row(d),
                  pl.BlockSpec((1, 1, N_MOD, d), lambda bb, i: (bb, jnp.minimum(i, 1), 0, 0)),
                  pl.BlockSpec((1, d), lambda bb, i: (0, 0)),
                  pl.BlockSpec((d, C_END), lambda bb, i: (0, 0)),
                  pl.BlockSpec((tm, LANES), lambda bb, i: (i, 0)),
                  pl.BlockSpec((tm, LANES), lambda bb, i: (i, 0))],
        out_specs=[row(w) for w in widths] + [pl.BlockSpec((1, vt_rows, tm), lambda bb, i: (bb, 0, i))],
        compiler_params=_params(2),
        name="norm_project_rope",
    )(xu, modtab, g, w_in, cos_u, sin_u)


def _tile_kind(t, nb):
    return jnp.where(t == 0, 0, jnp.where(t == 1, 1, jnp.where(t == nb, 3, 2)))


def _local_tiles(prev_ref, cur_ref, next_ref, u):
    tiles = cur_ref.shape[1] // TILE

    def tile(s):
        if s < 0:
            return prev_ref[0]
        if s >= tiles:
            return next_ref[0]
        return cur_ref[0, s * TILE:(s + 1) * TILE]

    return jnp.concatenate([tile(u - 1), tile(u), tile(u + 1)], axis=0)


def _na_kernel(q_ref, kc_ref, kp_ref, kcur_ref, kn_ref, vc_ref, vp_ref, vcur_ref, vn_ref, bias_ref, o_ref, *, nb):
    tiles = q_ref.shape[1] // TILE
    kc, vc = kc_ref[0], vc_ref[0]
    for u in range(tiles):
        rows = slice(u * TILE, (u + 1) * TILE)
        kind = _tile_kind(pl.program_id(1) * tiles + u, nb)
        q = q_ref[0, rows]
        kl = _local_tiles(kp_ref, kcur_ref, kn_ref, u)
        vl = _local_tiles(vp_ref, vcur_ref, vn_ref, u)
        outs = []
        for hd in range(NA_HEADS):
            sl = slice(hd * HEAD_DIM, (hd + 1) * HEAD_DIM)
            qh = q[:, sl]
            s_c = _nt_dot(qh, kc[:, sl])
            s_l = _nt_dot(qh, kl[:, sl]) + bias_ref[kind, hd]
            mx = jnp.maximum(jnp.max(s_c, axis=1, keepdims=True), jnp.max(s_l, axis=1, keepdims=True))
            p_c = jnp.exp(s_c - mx)
            p_l = jnp.exp(s_l - mx)
            den = jnp.sum(p_c, axis=1, keepdims=True) + jnp.sum(p_l, axis=1, keepdims=True)
            o = _dot(p_c.astype(BF16), vc[:, sl]) + _dot(p_l.astype(BF16), vl[:, sl])
            outs.append(o / den)
        o_ref[0, rows] = jnp.concatenate(outs, axis=1).astype(BF16)


def _na_bias_table(rpb):
    tr, nk = NA_TILE_ROWS, 3 * NA_TILE_ROWS
    qc = np.arange(GRID_W)[:, None]
    kc = np.arange(GRID_W)[None, :]
    cstart = np.clip(qc - NA_WIN_COLS // 2, 0, GRID_W - NA_WIN_COLS)
    col_ok = (kc >= cstart) & (kc < cstart + NA_WIN_COLS)
    dc = np.clip(kc - qc, -(NA_WIN_COLS - 1), NA_WIN_COLS - 1) + NA_WIN_COLS - 1
    onehot = jnp.asarray(np.arange(2 * NA_WIN_COLS - 1)[:, None, None] == dc[None], F32)
    cols = jnp.einsum("hrc,cqk->hrqk", rpb.astype(F32), onehot, precision=HIGHEST)
    qr = np.arange(tr)[:, None]
    krow = np.arange(nk)[None, :] - tr
    dr = np.clip(krow - qr, -(NA_WIN_ROWS - 1), NA_WIN_ROWS - 1) + NA_WIN_ROWS - 1
    starts = (0 * qr, qr - NA_WIN_ROWS // 2, 0 * qr + tr - NA_WIN_ROWS)
    row_ok = np.stack([(krow >= st) & (krow < st + NA_WIN_ROWS) for st in starts])
    t = jnp.take(cols, jnp.asarray(dr.reshape(-1)), axis=1)
    t = t.reshape(NA_HEADS, tr, nk, GRID_W, GRID_W).transpose(0, 1, 3, 2, 4)
    ok = row_ok[:, None, :, None, :, None] & col_ok[None, None, None, :, None, :]
    t = jnp.where(ok, t[None], NEG).reshape(3, NA_HEADS, TILE, 3 * TILE)
    return jnp.concatenate([jnp.full_like(t[:1], NEG), t], axis=0)


def _local_attention_specs(nt, wq, wk):
    nb = nt - 1
    tiles = max(t for t in range(1, 5 + 1) if nt % t == 0)
    rows = pl.BlockSpec((1, tiles * TILE, wq), lambda bb, i: (bb, i, 0))
    one = lambda f: pl.BlockSpec((1, TILE, wk), f)
    kv = [one(lambda bb, i: (bb, 0, 0)),
          one(lambda bb, i: (bb, jnp.clip(i * tiles - 1, 1, nb), 0)),
          pl.BlockSpec((1, tiles * TILE, wk), lambda bb, i: (bb, i, 0)),
          one(lambda bb, i: (bb, jnp.clip((i + 1) * tiles, 1, nb), 0))]
    return tiles, rows, kv


def _resident(a):
    return pl.BlockSpec(a.shape, lambda bb, i: (0,) * a.ndim, pipeline_mode=pl.Buffered(1))


def _na_attention(q, k, v, bias):
    b, m, w = q.shape
    nt = m // TILE
    tiles, rows, kv_specs = _local_attention_specs(nt, w, w)
    return pl.pallas_call(
        functools.partial(_na_kernel, nb=nt - 1),
        out_shape=jax.ShapeDtypeStruct((b, m, w), BF16),
        grid=(b, nt // tiles),
        in_specs=[rows] + kv_specs + kv_specs + [_resident(bias)],
        out_specs=rows,
        compiler_params=_params(2),
        name="neighbourhood_attention",
    )(q, k, k, k, k, v, v, v, v, bias)


def _swa_kernel(sink_ref, q_ref, kc_ref, kp_ref, kcur_ref, kn_ref, vc_ref, vp_ref, vcur_ref, vn_ref, mask_ref,
                o_ref, *, nb):
    tiles = q_ref.shape[1] // TILE
    kc, vc = kc_ref[0], vc_ref[0]
    group = SWA_HEADS // SWA_KV_HEADS
    rows = group * TILE
    rid = lax.broadcasted_iota(I32, (rows, 1), 0)
    for u in range(tiles):
        kind = _tile_kind(pl.program_id(1) * tiles + u, nb)
        q = q_ref[0, u * TILE:(u + 1) * TILE]
        kl = _local_tiles(kp_ref, kcur_ref, kn_ref, u)
        vl = _local_tiles(vp_ref, vcur_ref, vn_ref, u)
        mask = jnp.concatenate([mask_ref[kind]] * group, axis=0)
        outs = []
        for g in range(SWA_KV_HEADS):
            ksl = slice(g * HEAD_DIM, (g + 1) * HEAD_DIM)
            qg = jnp.concatenate([q[:, (g * group + j) * HEAD_DIM:(g * group + j + 1) * HEAD_DIM]
                                  for j in range(group)], axis=0)
            sink = jnp.zeros((rows, 1), F32)
            for j in range(group):
                sink = jnp.where(rid // TILE == j, sink_ref[g * group + j], sink)
            s_c = _nt_dot(qg, kc[:, ksl])
            s_l = _nt_dot(qg, kl[:, ksl]) + mask
            mx = jnp.maximum(jnp.maximum(jnp.max(s_c, axis=1, keepdims=True), jnp.max(s_l, axis=1, keepdims=True)),
                             sink)
            p_c = jnp.exp(s_c - mx)
            p_l = jnp.exp(s_l - mx)
            den = jnp.sum(p_c, axis=1, keepdims=True) + jnp.sum(p_l, axis=1, keepdims=True) + jnp.exp(sink - mx)
            o = (_dot(p_c.astype(BF16), vc[:, ksl]) + _dot(p_l.astype(BF16), vl[:, ksl])) / den
            outs.extend(o[j * TILE:(j + 1) * TILE] for j in range(group))
        o_ref[0, u * TILE:(u + 1) * TILE] = jnp.concatenate(outs, axis=1).astype(BF16)


def _swa_mask_table():
    qpos = np.arange(TILE)[:, None]
    kpos = np.arange(3 * TILE)[None, :] - TILE
    near = np.abs(qpos - kpos) <= SWA_WINDOW
    kinds = [np.zeros_like(near), near & (kpos >= 0), near, near & (kpos < TILE)]
    return jnp.asarray(np.where(np.stack(kinds), 0.0, NEG), F32)


def _swa_attention(q, k, v, sink):
    b, m, wq = q.shape
    nt = m // TILE
    tiles, rows, kv_specs = _local_attention_specs(nt, wq, k.shape[2])
    mask = _swa_mask_table()
    return pl.pallas_call(
        functools.partial(_swa_kernel, nb=nt - 1),
        out_shape=jax.ShapeDtypeStruct((b, m, wq), BF16),
        grid=(b, nt // tiles),
        in_specs=[pl.BlockSpec(memory_space=pltpu.SMEM), rows] + kv_specs + kv_specs + [_resident(mask)],
        out_specs=rows,
        compiler_params=_params(2),
        name="windowed_attention",
    )(sink, q, k, k, k, k, v, v, v, v, mask)


def _diff_kernel(lam_ref, q_ref, k_ref, v_ref, vt_ref, g_ref, gt_ref, o_ref, acc_ref, m_ref, s_ref, cmax_ref,
                 *, n_chunks, kblk, fast_kblk, lam_init):
    i = pl.program_id(2)
    parts = q_ref.shape[1] // TILE

    def stacked(part):
        q = q_ref[0, part * TILE:(part + 1) * TILE]
        lane = lax.broadcasted_iota(I32, q.shape, 1)
        zero = jnp.zeros_like(q)
        return jnp.concatenate([jnp.where(lane < HEAD_DIM, q, zero), jnp.where(lane >= HEAD_DIM, q, zero)], axis=0)

    def chunk(c):
        return pl.ds(pl.multiple_of(c * kblk, LANES), kblk)

    def accumulate(s, cmax, v):
        m_prev = m_ref[...]
        m_new = jnp.maximum(m_prev, cmax)
        p = jnp.exp2(s - m_new).astype(BF16)
        acc_ref[...] = jnp.exp2(m_prev - m_new) * acc_ref[...] + _dot(p, v)
        m_ref[...] = m_new

    def reset():
        m_ref[...] = jnp.full(m_ref.shape, NEG, F32)
        acc_ref[...] = jnp.zeros(acc_ref.shape, F32)

    def safe_sweep(qq):
        def scores(c, slot):
            s = _nt_dot(qq, k_ref[0, chunk(c), :])
            s_ref[slot] = s
            cmax_ref[slot] = jnp.max(s, axis=1, keepdims=True)

        reset()
        scores(0, 0)

        def pair(c, last):
            for cur in (0, 1):
                if not (last and cur == 1):
                    scores(c + cur + 1, 1 - cur)
                accumulate(s_ref[cur], cmax_ref[cur], v_ref[0, chunk(c + cur), :])

        def body(c2, carry):
            pair(2 * c2, False)
            return carry

        lax.fori_loop(0, n_chunks // 2 - 1, body, 0)
        pair(n_chunks - 2, True)
        return acc_ref[...]

    def fast_sweep(qq):
        bounds = [0, CTX] + list(range(CTX + fast_kblk, n_chunks * kblk, fast_kblk)) + [n_chunks * kblk]
        keys = lambda c: slice(bounds[c], bounds[c + 1])
        s = _nt_dot(k_ref[0, keys(0), :], qq)
        m0 = jnp.max(s, axis=0, keepdims=True)
        acc = _dot(vt_ref[0, :, keys(0)], jnp.exp2(s - m0).astype(BF16))
        for c in range(1, len(bounds) - 1):
            s = _nt_dot(k_ref[0, keys(c), :], qq)
            acc = acc + _dot(vt_ref[0, :, keys(c)], jnp.exp2(s - m0).astype(BF16))
        return acc

    def finish_transposed(acc):
        o = acc[:DIFF_V] / acc[DIFF_V:DIFF_V + 1]
        d = o[:, :TILE] - lam_ref[0] * o[:, TILE:]
        y = d * lax.rsqrt(jnp.mean(d * d, axis=0, keepdims=True) + EPS) * gt_ref[...]
        return (y * (1.0 - lam_init)).T.astype(BF16)

    def finish(acc):
        o = acc[:, :DIFF_V] / acc[:, DIFF_V:]
        d = o[:TILE] - lam_ref[0] * o[TILE:]
        y = d * lax.rsqrt(jnp.mean(d * d, axis=-1, keepdims=True) + EPS) * g_ref[...]
        return (y * (1.0 - lam_init)).astype(BF16)

    def rows(part):
        return pl.ds(part * TILE, TILE)

    @pl.when(i == 0)
    def _():
        o_ref[0, 0:(parts - 1) * TILE] = jnp.zeros(((parts - 1) * TILE, DIFF_V), BF16)
        reset()
        s = _nt_dot(stacked(parts - 1), k_ref[0, 0:CTX, :])
        accumulate(s, jnp.max(s, axis=1, keepdims=True), v_ref[0, 0:CTX, :])
        o_ref[0, rows(parts - 1)] = finish(acc_ref[...])

    @pl.when(i > 0)
    def _():
        accs = [fast_sweep(stacked(part)) for part in range(parts)]
        for part in range(parts):
            o_ref[0, rows(part)] = finish_transposed(accs[part])
        bad = [jnp.logical_not(jnp.max(acc[DIFF_V:DIFF_V + 1]) < DIFF_MAX_DENOMINATOR) for acc in accs]
        for part in range(parts):
            @pl.when(bad[part])
            def _(part=part):
                o_ref[0, rows(part)] = finish(safe_sweep(stacked(part)))


def _diff_chunking(m, max_chunk=2048):
    for n_chunks in range(2, m // LANES + 1, 2):
        if m % (n_chunks * LANES) == 0 and m // n_chunks <= max_chunk:
            return n_chunks, m // n_chunks
    raise ValueError(f"no chunking for {m} keys")


def _diff_attention(q, k, v, vt, lam, g, lam_init):
    b, m, _ = q.shape
    vt_rows = DIFF_V + DIFF_VT_PAD
    n_chunks, kblk = _diff_chunking(m)
    fast_kblk = DIFF_FAST_CHUNK
    parts = DIFF_TILES_PER_STEP
    pad = (parts - 1) * TILE
    assert (m + pad) % (parts * TILE) == 0
    q_pad = jnp.pad(q, ((0, 0), (pad, 0), (0, 0)))
    out = pl.pallas_call(
        functools.partial(_diff_kernel, n_chunks=n_chunks, kblk=kblk, fast_kblk=fast_kblk, lam_init=lam_init),
        out_shape=jax.ShapeDtypeStruct((b, m + pad, DIFF_HEADS * DIFF_V), BF16),
        grid=(b, DIFF_HEADS, (m + pad) // (parts * TILE)),
        in_specs=[pl.BlockSpec(memory_space=pltpu.SMEM),
                  pl.BlockSpec((1, parts * TILE, 2 * HEAD_DIM), lambda bb, hh, i: (bb, i, hh)),
                  pl.BlockSpec((1, m, 2 * HEAD_DIM), lambda bb, hh, i: (bb, 0, hh)),
                  pl.BlockSpec((1, m, 2 * DIFF_V), lambda bb, hh, i: (bb, 0, hh), pipeline_mode=pl.Buffered(1)),
                  pl.BlockSpec((1, vt_rows, m), lambda bb, hh, i: (bb, hh, 0)),
                  pl.BlockSpec((1, DIFF_V), lambda bb, hh, i: (0, 0)),
                  pl.BlockSpec((DIFF_V, 1), lambda bb, hh, i: (0, 0))],
        out_specs=pl.BlockSpec((1, parts * TILE, DIFF_V), lambda bb, hh, i: (bb, i, hh)),
        scratch_shapes=[pltpu.VMEM((2 * TILE, 2 * DIFF_V), F32), pltpu.VMEM((2 * TILE, 1), F32),
                        pltpu.VMEM((2, 2 * TILE, kblk), F32), pltpu.VMEM((2, 2 * TILE, 1), F32)],
        compiler_params=_params(3),
        name="differential_attention",
    )(lam, q_pad, k, v, vt, g, g.reshape(DIFF_V, 1))
    return out[:, pad:]


def _merge_kernel(ona_ref, od_ref, osw_ref, gs_ref, x_ref, mod_ref, wna_ref, wd_ref, wsw_ref, wo_ref,
                  g2_ref, wrt_ref, xo_ref, h2_ref, afft_ref):
    tm, d = x_ref.shape[1:]
    h2_parts = []
    for rows in _independent_parts(tm):
        gs = gs_ref[0, rows]
        y = (gs[:, :d].astype(F32) * _dot(ona_ref[0, rows], wna_ref[...])
             + gs[:, d:2 * d].astype(F32) * _dot(od_ref[0, rows], wd_ref[...])
             + gs[:, 2 * d:].astype(F32) * _dot(osw_ref[0, rows], wsw_ref[...]))
        mod = _row_modulation(mod_ref, pl.program_id(1) * tm + rows.start, rows.stop - rows.start)
        xn = x_ref[0, rows] + mod(2) * _dot(y.astype(BF16), wo_ref[...])
        xo_ref[0, rows] = xn
        r = xn * lax.rsqrt(jnp.mean(xn * xn, axis=-1, keepdims=True) + EPS) * g2_ref[...]
        h2 = r * (1.0 + mod(4)) + mod(3)
        h2_ref[0, rows] = h2.astype(BF16)
        h2_parts.append(h2)
    h2 = jnp.concatenate(h2_parts, axis=0)
    lt = lax.dot_general(wrt_ref[...], h2, (((1,), (1,)), ((), ())),
                         preferred_element_type=F32, precision=HIGHEST)
    et = jnp.exp(lt - jnp.max(lt, axis=0, keepdims=True))
    afft_ref[0] = et / jnp.sum(et, axis=0, keepdims=True)


def _merge(o_na, o_d, o_sw, gs, xu, modtab, w_na, w_d, w_sw, w_o, g2, w_r):
    b, m, d = xu.shape
    tm = _matmul_rows(m, 640)
    ne = w_r.shape[1]
    row = lambda w: pl.BlockSpec((1, tm, w), lambda bb, i: (bb, i, 0))
    full = lambda a: pl.BlockSpec(a.shape, lambda bb, i: (0,) * a.ndim)
    w_rt = w_r.T
    return pl.pallas_call(
        _merge_kernel,
        out_shape=[jax.ShapeDtypeStruct((b, m, d), F32), jax.ShapeDtypeStruct((b, m, d), BF16),
                   jax.ShapeDtypeStruct((b, ne, m), F32)],
        grid=(b, m // tm),
        in_specs=[row(o_na.shape[2]), row(o_d.shape[2]), row(o_sw.shape[2]), row(gs.shape[2]), row(d),
                  pl.BlockSpec((1, 2, N_MOD, d), lambda bb, i: (bb, 0, 0, 0)),
                  full(w_na), full(w_d), full(w_sw), full(w_o), full(g2), full(w_rt)],
        out_specs=[row(d), row(d), pl.BlockSpec((1, ne, tm), lambda bb, i: (bb, 0, i))],
        compiler_params=_params(2),
        name="merge_residual_router",
    )(o_na, o_d, o_sw, gs, xu, modtab, w_na, w_d, w_sw, w_o, g2, w_rt)


def _route_kernel(a_ref, gpos_ref, gfull_ref, *, n_latent_sets, cap_lat, cap_ctx):
    ne, c, w = a_ref.shape[1:]
    a = a_ref[0]
    cap =jnp.where(pl.program_id(0) < n_latent_sets, cap_lat, cap_ctx).astype(F32)

    def count(mask):
        return jnp.sum(jnp.sum(mask.astype(F32), axis=2, keepdims=True), axis=1, keepdims=True)

    def search(it, thr):
        cand = thr | jnp.left_shift(jnp.int32(1), 30 - it)
        return jnp.where(count(a >= lax.bitcast_convert_type(cand, F32)) >= cap, cand, thr)

    thr = lax.fori_loop(0, 31, search, jnp.zeros((ne, 1, 1), I32))
    thr_f = lax.bitcast_convert_type(thr, F32)
    gt = a > thr_f
    eq = a == thr_f
    need = cap - count(gt)

    upper = (lax.broadcasted_iota(I32, (w, w), 0) <= lax.broadcasted_iota(I32, (w, w), 1)).astype(BF16)
    lower = (lax.broadcasted_iota(I32, (c, c), 1) < lax.broadcasted_iota(I32, (c, c), 0)).astype(BF16)

    def exclusive_cumsum(mask):
        x = mask.astype(F32).reshape(ne * c, w)
        within = _dot(x.astype(BF16), upper)
        tot = jnp.broadcast_to(within[:, w - 1:w], (ne * c, w)).astype(BF16)
        before = jnp.concatenate([_dot(lower, tot[e * c:(e + 1) * c]) for e in range(ne)], axis=0)
        return (before + within - x).reshape(ne, c, w)

    sel = gt | (eq & (exclusive_cumsum(eq) < need))
    g = exclusive_cumsum(sel).astype(I32)
    gfull_ref[0] = g
    gpos_ref[0] = jnp.where(sel, g, -1)


def _route(aff_sets, n_latent_sets, cap_lat, cap_ctx):
    s, ne, n = aff_sets.shape
    c = n // LANES
    a4 = aff_sets.reshape(s, ne, c, LANES)
    blk = pl.BlockSpec((1, ne, c, LANES), lambda i: (i, 0, 0, 0))
    gpos, gfull = pl.pallas_call(
        functools.partial(_route_kernel, n_latent_sets=n_latent_sets, cap_lat=cap_lat, cap_ctx=cap_ctx),
        out_shape=[jax.ShapeDtypeStruct(a4.shape, I32)] * 2,
        grid=(s,),
        in_specs=[blk],
        out_specs=[blk, blk],
        compiler_params=_params(1),
        name="expert_choice_select",
    )(a4)
    return gpos.reshape(s, ne, n), gfull.reshape(s, ne, n)


def _window(cap):
    w = min(TILE + BF16_ROWS, cap)
    main = min(WINDOW_MAIN_ROWS, w)
    return main, w - main, cap - w


def _windows(gfull, cap):
    b, ne, _ = gfull.shape
    main, _, max_start = _window(cap)
    seg_start = gfull[:, :, ::TILE]
    seg_end = jnp.concatenate([gfull[:, :, TILE::TILE], jnp.full((b, ne, 1), cap, I32)], axis=2)
    astart = jnp.minimum(seg_start // BF16_ROWS * BF16_ROWS, max_start)
    return astart.reshape(-1), (seg_end > astart + main).astype(I32).reshape(-1)


def _gather_kernel(a_ref, t_ref, g_ref, aff_ref, h_ref, o_ref, w_ref, *, ne, nblk, eg, main, tail):
    b, egi, j = pl.program_id(0), pl.program_id(1), pl.program_id(2)

    @pl.when(j == 0)
    def _():
        o_ref[...] = jnp.zeros(o_ref.shape, o_ref.dtype)
        w_ref[...] = jnp.zeros(w_ref.shape, w_ref.dtype)

    hb = h_ref[0]
    t = hb.shape[0]

    def idx(k):
        return (b * ne + egi * eg + k) * nblk + j

    def place(experts, first, rows):
        hits = [lax.broadcasted_iota(I32, (rows, t), 0) == g_ref[0, k] - (a_ref[idx(k)] + first) for k in experts]
        onehot = jnp.concatenate([jnp.where(h, 1.0, 0.0).astype(BF16) for h in hits], axis=0)
        picked = _dot(onehot, hb).astype(BF16)
        for n, (k, hit) in enumerate(zip(experts, hits)):
            weight = jnp.sum(jnp.where(hit, aff_ref[0, k], 0.0), axis=1, keepdims=True)
            sl = (0, k, pl.ds(pl.multiple_of(a_ref[idx(k)] + first, BF16_ROWS), rows), slice(None))
            o_ref[sl] = o_ref[sl] + picked[n * rows:(n + 1) * rows]
            w_ref[sl] = w_ref[sl] + weight

    place(list(range(eg)), 0, main)
    for k in range(eg if tail else 0):
        @pl.when(t_ref[idx(k)] != 0)
        def _(k=k):
            place([k], main, tail)


def _gather(h2u, afft, gpos, astart, need_tail, cap, tile_off, eg):
    b, ne, n = gpos.shape
    m, d = h2u.shape[1:]
    nblk = n // TILE
    main, tail, _ = _window(cap)
    lists = lambda w: pl.BlockSpec((1, eg, cap, w), lambda bb, e, j, a, nt: (bb, e, 0, 0),
                                   pipeline_mode=pl.Buffered(1))
    return pl.pallas_call(
        functools.partial(_gather_kernel, ne=ne, nblk=nblk, eg=eg, main=main, tail=tail),
        out_shape=[jax.ShapeDtypeStruct((b, ne, cap, d), BF16), jax.ShapeDtypeStruct((b, ne, cap, 1), F32)],
        grid_spec=pltpu.PrefetchScalarGridSpec(
            num_scalar_prefetch=2,
            grid=(b, ne // eg, nblk),
            in_specs=[pl.BlockSpec((1, eg, 1, TILE), lambda bb, e, j, a, nt: (bb, e, 0, j)),
                      pl.BlockSpec((1, eg, 1, TILE), lambda bb, e, j, a, nt: (bb, e, 0, tile_off + j)),
                      pl.BlockSpec((1, TILE, d), lambda bb, e, j, a, nt: (bb, tile_off + j, 0))],
            out_specs=[lists(d), lists(1)]),
        compiler_params=_params(3),
        name="expert_gather",
    )(astart, need_tail, gpos.reshape(b, ne, 1, n), afft.reshape(b, ne, 1, m), h2u)


def _ffn_kernel(x_ref, rw_ref, wg_ref, wu_ref, wd_ref, o_ref, wg_s, wu_s, wd_s):
    @pl.when((pl.program_id(1) == 0) & (pl.program_id(2) == 0))
    def _():
        wg_s[...] = wg_ref[0, 0].astype(BF16)
        wu_s[...] = wu_ref[0, 0].astype(BF16)
        wd_s[...] = wd_ref[0, 0].astype(BF16)

    x = x_ref[0, 0]
    gate = _dot(x, wg_s[...])
    up = _dot(x, wu_s[...])
    hid = (gate / (1.0 + jnp.exp(-gate)) * up).astype(BF16)
    o_ref[0, 0] = (_dot(hid, wd_s[...]) * rw_ref[0, 0]).astype(BF16)


def _expert_ffn(xs, ws, w_gate, w_up, w_down, layer):
    b, ne, cap, d = xs.shape
    tr = min(cap, 1024)
    wspec = lambda w: pl.BlockSpec((1, 1) + w.shape[2:], lambda e, bb, r: (layer, e, 0, 0))
    rows = lambda w: pl.BlockSpec((1, 1, tr, w), lambda e, bb, r: (bb, e, r, 0))
    return pl.pallas_call(
        _ffn_kernel,
        out_shape=jax.ShapeDtypeStruct(xs.shape, BF16),
        grid=(ne, b, cap // tr),
        in_specs=[rows(d), rows(1), wspec(w_gate), wspec(w_up), wspec(w_down)],
        out_specs=rows(d),
        scratch_shapes=[pltpu.VMEM(w.shape[2:], BF16) for w in (w_gate, w_up, w_down)],
        compiler_params=_params(3),
        name="expert_swiglu",
    )(xs, ws, w_gate, w_up, w_down)


def _combine_kernel(a_ref, t_ref, x_ref, g_ref, mod_ref, fg_ref, ye_ref, o_ref,
                    win_ref, tail_ref, tacc_ref, sem_ref, tsem_ref, *, ne, nblk, main, tail, final):
    b, j = pl.program_id(0), pl.program_id(1)
    step = b * nblk + j
    slot = step % 2

    def start_at(bb, jj, e):
        return pl.multiple_of(a_ref[(bb * ne + e) * nblk + jj], BF16_ROWS)

    def start_of(e):
        return start_at(b, j, e)

    def window_copy(bb, jj, e, sl):
        return pltpu.make_async_copy(ye_ref.at[bb, e, pl.ds(start_at(bb, jj, e), main)],
                                     win_ref.at[sl, e * main:(e + 1) * main], sem_ref.at[sl, e])

    @pl.when(step == 0)
    def _():
        for e in range(ne):
            window_copy(b, j, e, slot).start()

    @pl.when(step + 1 < pl.num_programs(0) * nblk)
    def _():
        nxt = step + 1
        for e in range(ne):
            window_copy(nxt // nblk, nxt % nblk, e, 1 - slot).start()

    t = x_ref.shape[1]

    def pick(experts, first, rows, windows):
        onehot = jnp.concatenate(
            [jnp.where(lax.broadcasted_iota(I32, (rows, t), 0) == g_ref[0, e] - (start_of(e) + first), 1.0, 0.0)
             .astype(BF16) for e in experts], axis=0)
        return lax.dot_general(onehot, windows, (((0,), (0,)), ((), ())), preferred_element_type=F32)

    if tail:
        tacc_ref[...] = jnp.zeros(tacc_ref.shape, F32)
        for e in range(ne):
            @pl.when(t_ref[(b * ne + e) * nblk + j] != 0)
            def _(e=e):
                cp = pltpu.make_async_copy(ye_ref.at[b, e, pl.ds(start_of(e) + main, tail)], tail_ref, tsem_ref.at[0])
                cp.start()
                cp.wait()
                tacc_ref[...] += pick([e], main, tail, tail_ref[...])

    for e in range(ne):
        window_copy(b, j, e, slot).wait()
    acc = pick(list(range(ne)), 0, main, win_ref[slot])
    if tail:
        acc = acc + tacc_ref[...]
    x = x_ref[0] + mod_ref[0, 0][5:6] * acc
    if final:
        x = x * lax.rsqrt(jnp.mean(x * x, axis=-1, keepdims=True) + EPS) * fg_ref[...]
    o_ref[0] = x


def _combine(xu, gpos, astart, need_tail, ye, modtab, final_g, tile_off, kind, final):
    b, m, d = xu.shape
    _, ne, n = gpos.shape
    cap = ye.shape[2]
    nblk = n // TILE
    main, tail, _ = _window(cap)
    out_spec = pl.BlockSpec((1, TILE, d), (lambda bb, j, a, nt: (bb, j, 0)) if final
                            else (lambda bb, j, a, nt: (bb, tile_off + j, 0)))
    return pl.pallas_call(
        functools.partial(_combine_kernel, ne=ne, nblk=nblk, main=main, tail=tail, final=final),
        out_shape=jax.ShapeDtypeStruct((b, n, d) if final else xu.shape, F32),
        grid_spec=pltpu.PrefetchScalarGridSpec(
            num_scalar_prefetch=2,
            grid=(b, nblk),
            in_specs=[pl.BlockSpec((1, TILE, d), lambda bb, j, a, nt: (bb, tile_off + j, 0)),
                      pl.BlockSpec((1, ne, 1, TILE), lambda bb, j, a, nt: (bb, 0, 0, j)),
                      pl.BlockSpec((1, 1, N_MOD, d), lambda bb, j, a, nt: (bb, kind, 0, 0)),
                      pl.BlockSpec((1, d), lambda bb, j, a, nt: (0, 0)),
                      pl.BlockSpec(memory_space=pl.ANY)],
            out_specs=out_spec,
            scratch_shapes=[pltpu.VMEM((2, ne * main, d), BF16), pltpu.VMEM((max(tail, BF16_ROWS), d), BF16),
                            pltpu.VMEM((TILE, d), F32), pltpu.SemaphoreType.DMA((2, ne)),
                            pltpu.SemaphoreType.DMA((1,))]),
        input_output_aliases={} if final else {2: 0},
        compiler_params=_params(2),
        name="expert_combine",
    )(astart, need_tail, xu, gpos.reshape(b, ne, 1, n), modtab, final_g, ye)


def _rope_tables(n):
    t = np.arange(n)
    row = (t // GRID_W).astype(np.float32)[:, None]
    col = (t % GRID_W).astype(np.float32)[:, None]
    inv = (ROPE_BASE ** (-np.arange(ROPE_PAIRS, dtype=np.float32) / ROPE_PAIRS)).astype(np.float32)
    ang = np.concatenate([row * inv, row * inv, col * inv, col * inv], axis=-1)
    cos, sin = np.cos(ang), np.sin(ang)
    half = (np.arange(HEAD_DIM) % (2 * ROPE_PAIRS)) < ROPE_PAIRS
    sin = np.where(half[None], -sin, sin)
    cos = np.concatenate([np.ones((CTX, HEAD_DIM)), cos], axis=0)
    sin = np.concatenate([np.zeros((CTX, HEAD_DIM)), sin], axis=0)
    tile2 = lambda a: jnp.asarray(np.concatenate([a, a], axis=1), F32)
    return tile2(cos), tile2(sin)


def _moe(xu, h2u, afft, modtab, weights, layer, final_g, final):
    b, m, d = xu.shape
    n = m - CTX
    w_gate, w_up, w_down = weights
    cap_lat = max(1, EC_CAPACITY * n // N_EXPERTS)
    cap_ctx = max(1, EC_CAPACITY * CTX // N_EXPERTS)
    sets = [afft[:, :, CTX:]]
    if not final:
        sets.append(jnp.concatenate([afft[:, :, :CTX], jnp.full((b, N_EXPERTS, n - CTX), -1.0, F32)], axis=2))
    gpos, gfull = _route(jnp.concatenate(sets, axis=0), b, cap_lat, cap_ctx)

    def run(idx, n_set, cap, tile_off, kind, x_in, fin):
        gp = gpos[idx * b:(idx + 1) * b, :, :n_set]
        gf = gfull[idx * b:(idx + 1) * b, :, :n_set]
        astart, need_tail = _windows(gf, cap)
        xs, ws = _gather(h2u, afft, gp, astart, need_tail, cap, tile_off, eg=8)
        ye = _expert_ffn(xs, ws, w_gate, w_up, w_down, layer)
        return _combine(x_in, gp, astart, need_tail, ye, modtab, final_g, tile_off, kind, fin)

    if final:
        return run(0, n, cap_lat, 1, 1, xu, True)
    xu = run(0, n, cap_lat, 1, 1, xu, False)
    return run(1, CTX, cap_ctx, 0, 0, xu, False)


def kernel(x, c, ctx, c_ctx, w_mod, b_mod, norm1_g, w_in, na_rpb, diff_lambda, diff_subln_g, swa_sink,
           w_branch_na, w_branch_diff, w_branch_swa, w_out, norm2_g, w_router, w_expert_gate, w_expert_up,
           w_expert_down, final_g):
    b, n, d = x.shape
    depth = w_mod.shape[0]
    assert ctx.shape[1] == CTX and d == D_MODEL and n % (2 * TILE) == 0 and n // TILE >= 3

    cin = jnp.concatenate([c, c_ctx[None], jnp.zeros((8 - b - 1, d), F32)], axis=0)
    mod_all = _modulation(cin, w_mod, b_mod)
    cos_u, sin_u = _rope_tables(n)
    xu = jnp.concatenate([ctx, x], axis=1)
    final_g2 = final_g.reshape(1, d)

    out = None
    for l in range(depth):
        final = l == depth - 1
        lam_init = 0.8 - 0.6 * math.exp(-0.3 * l)
        mod_l = mod_all[l].reshape(8, N_MOD, d)
        modtab = jnp.stack([jnp.broadcast_to(mod_l[b], (b, N_MOD, d)), mod_l[:b]], axis=1)
        lam_p = diff_lambda[l].astype(F32)
        lam = (jnp.exp(jnp.sum(lam_p[0] * lam_p[1])) - jnp.exp(jnp.sum(lam_p[2] * lam_p[3])) + lam_init).reshape(1)

        (q_na, k_na, v_na, q_d, k_d, v_d, q_s, k_s, v_s, gs, v_dt) = _project(
            xu, modtab, norm1_g[l].reshape(1, d), w_in[l].astype(BF16), cos_u, sin_u)
        o_na = _na_attention(q_na, k_na, v_na, _na_bias_table(na_rpb[l]))
        o_d = _diff_attention(q_d, k_d, v_d, v_dt, lam, diff_subln_g[l].reshape(1, DIFF_V), lam_init)
        o_s = _swa_attention(q_s, k_s, v_s, swa_sink[l].astype(F32))
        xu, h2u, afft = _merge(
            o_na, o_d, o_s, gs, xu, modtab, w_branch_na[l].astype(BF16), w_branch_diff[l].astype(BF16),
            w_branch_swa[l].astype(BF16), w_out[l].astype(BF16), norm2_g[l].reshape(1, d), w_router[l])
        weights = (w_expert_gate, w_expert_up, w_expert_down)
        res = _moe(xu, h2u, afft, modtab, weights, l, final_g2, final)
        if final:
            out = res
        else:
            xu = res
    return out
```
